```python
import math
import jax
import jax.numpy as jnp
from jax import lax
import numpy as np

D_MODEL = 1024
BATCH = 16
SEQ = 256
DEPTH = 2
DEC_BATCH = 4
DEC_SEQ = 4096
PAST_LEN = 512

GRID_W = 64
Q_BLOCK = 128
ROPE_BASE = 10000.0
RMS_EPS = 1e-6
NEG_INF = -1e30
MIX_W = D_MODEL // 2
N_BRANCH = 4
A_HEADS = 4
A_DIM = 64
B_HEADS = 8
B_NOPE = 64
B_ROPE = 32
B_V = 64
B_QLORA = 256
B_KVLORA = 128
C_HEADS = 8
C_KV_HEADS = 2
C_DIM = 64
C_WINDOW = 128
D_GROUP = 16
D_GROUPS = MIX_W // D_GROUP
D_STATE = 64
N_DIRS = 2
DT_MIN = 1e-3
DT_MAX = 1e-1
FF_HIDDEN = ((8 * D_MODEL // 3 + 255) // 256) * 256
IN_SIZES = (A_HEADS * 2 * A_DIM, A_HEADS * 2 * A_DIM, A_HEADS * 2 * A_DIM,
            B_QLORA, B_KVLORA, B_ROPE,
            C_HEADS * C_DIM, C_KV_HEADS * C_DIM, C_KV_HEADS * C_DIM,
            MIX_W)
IN_WIDTH = 3 * A_HEADS * 2 * A_DIM + B_QLORA + B_KVLORA + B_ROPE + (C_HEADS + 2 * C_KV_HEADS) * C_DIM + MIX_W

kernel_name = 'hybrid_diffusion_prefix_trunk_step'


def rms_norm(x, g, eps=RMS_EPS):
    x32 = x.astype(jnp.float32)
    y = x32 * lax.rsqrt(jnp.mean(x32 * x32, axis=-1, keepdims=True) + eps)
    return (y * g.astype(jnp.float32)).astype(x.dtype)


def axial_rope_tables(n_tokens, rot_dim):
    rows = n_tokens // GRID_W
    pos_row = jnp.repeat(jnp.arange(rows, dtype=jnp.float32), GRID_W)
    pos_col = jnp.tile(jnp.arange(GRID_W, dtype=jnp.float32), rows)
    n_freq = rot_dim // 4
    inv_freq = ROPE_BASE ** (-jnp.arange(n_freq, dtype=jnp.float32) / n_freq)
    ang = jnp.concatenate([pos_row[:, None] * inv_freq, pos_col[:, None] * inv_freq], axis=-1)
    return jnp.cos(ang), jnp.sin(ang)


def apply_rope(x, cos, sin):
    half = x.shape[-1] // 2
    shp = (cos.shape[0],) + (1,) * (x.ndim - 3) + (half,)
    c_, s_ = cos.reshape(shp), sin.reshape(shp)
    x32 = x.astype(jnp.float32)
    x1, x2 = x32[..., :half], x32[..., half:]
    return jnp.concatenate([x1 * c_ - x2 * s_, x1 * s_ + x2 * c_], axis=-1).astype(x.dtype)


def sweep_query_blocks(fn, *qs):
    bsz, n = qs[0].shape[0], qs[0].shape[1]
    nb = n // Q_BLOCK
    blocks = tuple(jnp.moveaxis(q.reshape((bsz, nb, Q_BLOCK) + q.shape[2:]), 1, 0) for q in qs)
    out = lax.map(lambda args: fn(args[0], *args[1:]), (jnp.arange(nb),) + blocks)
    out = jnp.moveaxis(out, 0, 1)
    return out.reshape((bsz, n) + out.shape[3:])


def split_cols(proj):
    outs = []
    start = 0
    for w in IN_SIZES:
        outs.append(proj[..., start:start + w])
        start += w
    return outs


def diff_attention(q, k, v, lam, lam_init, g_sub):
    bsz, n = q.shape[0], q.shape[1]
    scale = A_DIM ** -0.5

    def block(j, qb):
        s = jnp.einsum('bqhcd,bkhcd->bchqk', qb, k, preferred_element_type=jnp.float32) * scale
        p = jax.nn.softmax(s, axis=-1)
        att = (p[:, 0] - lam * p[:, 1]).astype(v.dtype)
        return jnp.einsum('bhqk,bkhe->bqhe', att, v)

    o = sweep_query_blocks(block, q)
    o = rms_norm(o, g_sub) * (1.0 - lam_init)
    return o.reshape(bsz, n, A_HEADS * 2 * A_DIM)


def mla_attention(q_nope, q_rope, k_nope, k_rope, v):
    bsz, n = q_nope.shape[0], q_nope.shape[1]
    scale = (B_NOPE + B_ROPE) ** -0.5

    def block(j, qn, qr):
        s = (jnp.einsum('bqhd,bkhd->bhqk', qn, k_nope, preferred_element_type=jnp.float32)
             + jnp.einsum('bqhd,bkd->bhqk', qr, k_rope, preferred_element_type=jnp.float32)) * scale
        p = jax.nn.softmax(s, axis=-1).astype(v.dtype)
        return jnp.einsum('bhqk,bkhe->bqhe', p, v)

    o = sweep_query_blocks(block, q_nope, q_rope)
    return o.reshape(bsz, n, B_HEADS * B_V)


def window_attention(q, k_ctx, v_ctx, sink, k_lat=None, v_lat=None):
    bsz, n = q.shape[0], q.shape[1]
    scale = C_DIM ** -0.5
    sink_logit = sink.astype(jnp.float32).reshape(1, C_KV_HEADS, C_HEADS // C_KV_HEADS, 1, 1)
    latent = k_lat is not None
    n_ctx = k_ctx.shape[1]
    if latent:
        pad = ((0, 0), (Q_BLOCK, Q_BLOCK), (0, 0), (0, 0))
        k_pad = jnp.pad(k_lat, pad)
        v_pad = jnp.pad(v_lat, pad)

    def block(j, qb):
        scores = [jnp.einsum('bqgrd,bkgd->bgrqk', qb, k_ctx, preferred_element_type=jnp.float32) * scale]
        if latent:
            kb = lax.dynamic_slice_in_dim(k_pad, j * Q_BLOCK, 3 * Q_BLOCK, axis=1)
            vb = lax.dynamic_slice_in_dim(v_pad, j * Q_BLOCK, 3 * Q_BLOCK, axis=1)
            s = jnp.einsum('bqgrd,bkgd->bgrqk', qb, kb, preferred_element_type=jnp.float32) * scale
            qpos = j * Q_BLOCK + jnp.arange(Q_BLOCK)
            kpos = j * Q_BLOCK - Q_BLOCK + jnp.arange(3 * Q_BLOCK)
            ok = ((jnp.abs(qpos[:, None] - kpos[None, :]) <= C_WINDOW)
                  & (kpos >= 0)[None, :] & (kpos < n)[None, :])
            scores.append(jnp.where(ok, s, NEG_INF))
        sink_col = jnp.broadcast_to(sink_logit, scores[0].shape[:-1] + (1,))
        p = jax.nn.softmax(jnp.concatenate(scores + [sink_col], axis=-1), axis=-1).astype(v_ctx.dtype)
        out = jnp.einsum('bgrqk,bkgd->bqgrd', p[..., :n_ctx], v_ctx)
        if latent:
            out = out + jnp.einsum('bgrqk,bkgd->bqgrd', p[..., n_ctx:n_ctx + 3 * Q_BLOCK], vb.astype(v_ctx.dtype))
        return out

    o = sweep_query_blocks(block, q)
    return o.reshape(bsz, n, C_HEADS * C_DIM)


def _scan_combine(e1, e2):
    a1, b1 = e1
    a2, b2 = e2
    return a2 * a1, a2 * b1 + b2


def diag_linear_scan(a_bar, bu, h0):
    if h0 is not None:
        bu = bu.at[:, 0].add(a_bar * h0)
    _, h = lax.associative_scan(_scan_combine, (jnp.broadcast_to(a_bar, bu.shape), bu), axis=1)
    return h


def s5_mixer(u, a_re, a_im, log_dt, b_re, b_im, c_re, c_im, d_skip, w_glu, h0):
    bsz, n, _ = u.shape
    f32 = jnp.float32
    u32 = u.astype(f32)
    ug = u32.reshape(bsz, n, D_GROUPS, D_GROUP)
    y = d_skip.astype(f32) * u32
    finals = []
    for di in range(N_DIRS):
        lam = lax.complex(a_re[di].astype(f32), a_im[di].astype(f32))
        dt = jnp.exp(log_dt[di].astype(f32))[:, None]
        a_bar = jnp.exp(lam * dt)
        b_bar = ((a_bar - 1.0) / lam)[..., None] * lax.complex(b_re[di].astype(f32), b_im[di].astype(f32))
        seq = ug if di == 0 else jnp.flip(ug, axis=1)
        bu = jnp.einsum('blgp,gnp->blgn', seq.astype(jnp.complex64), b_bar)
        h_init = None if h0 is None else lax.complex(h0[0][:, di].astype(f32), h0[1][:, di].astype(f32))
        h = diag_linear_scan(a_bar, bu, h_init)
        finals.append(h[:, -1])
        if di == 1:
            h = jnp.flip(h, axis=1)
        cmat = lax.complex(c_re[di].astype(f32), c_im[di].astype(f32))
        y = y + jnp.einsum('blgn,gpn->blgp', h, cmat).real.reshape(bsz, n, MIX_W)
    fin = jnp.stack(finals, axis=1)
    yg = jax.nn.gelu(y)
    out = yg * jax.nn.sigmoid(yg @ w_glu.astype(f32))
    return out.astype(u.dtype), fin.real, fin.imag


def trunk_layer(l, x, cond, lp, ctx):
    latent = ctx is not None
    bsz, n, _ = x.shape
    dt = x.dtype
    mod = (jax.nn.silu(cond.astype(jnp.float32)) @ lp['w_mod'].astype(jnp.float32)
           + lp['b_mod'].astype(jnp.float32)).astype(dt)
    shift_m, scale_m, gate_m, shift_f, scale_f, gate_f = jnp.split(mod[:, None, :], 6, axis=-1)

    h = rms_norm(x, lp['g_pre_mix']) * (1 + scale_m) + shift_m
    aq, ak, av, bcq, bckv, bkr, cq, ck, cv, du = split_cols(h @ lp['w_in'])
    if latent:
        cos_a, sin_a = axial_rope_tables(n, A_DIM)
        cos_b, sin_b = axial_rope_tables(n, B_ROPE)

    aq = aq.reshape(bsz, n, A_HEADS, 2, A_DIM)
    ak = ak.reshape(bsz, n, A_HEADS, 2, A_DIM)
    av = av.reshape(bsz, n, A_HEADS, 2 * A_DIM)
    lam_init = 0.8 - 0.6 * math.exp(-0.3 * l)
    lam = (jnp.exp(jnp.sum(lp['lam_q1'].astype(jnp.float32) * lp['lam_k1'].astype(jnp.float32)))
           - jnp.exp(jnp.sum(lp['lam_q2'].astype(jnp.float32) * lp['lam_k2'].astype(jnp.float32))) + lam_init)
    if latent:
        ka_ctx = ctx['a_k'].reshape(bsz, ctx['a_k'].shape[1], A_HEADS, 2, A_DIM).astype(dt)
        keys_a = jnp.concatenate([apply_rope(ak, cos_a, sin_a), ka_ctx], axis=1)
        vals_a = jnp.concatenate([av, ctx['a_v'].astype(dt)], axis=1)
        q_a = apply_rope(aq, cos_a, sin_a)
    else:
        keys_a, vals_a, q_a = ak, av, aq
    o_a = diff_attention(q_a, keys_a, vals_a, lam, lam_init, lp['g_a_sub'])

    q_b = (rms_norm(bcq, lp['g_b_q']) @ lp['w_b_uq']).reshape(bsz, n, B_HEADS, B_NOPE + B_ROPE)
    qn, qr = q_b[..., :B_NOPE], q_b[..., B_NOPE:]
    ckv = rms_norm(bckv, lp['g_b_kv'])
    if latent:
        qr = apply_rope(qr, cos_b, sin_b)
        ckv_all = jnp.concatenate([ckv, ctx['b_ckv'].astype(dt)], axis=1)
        kr_all = jnp.concatenate([apply_rope(bkr, cos_b, sin_b), ctx['b_kr'].astype(dt)], axis=1)
    else:
        ckv_all, kr_all = ckv, bkr
    kv_b = (ckv_all @ lp['w_b_ukv']).reshape(bsz, ckv_all.shape[1], B_HEADS, B_NOPE + B_V)
    o_b = mla_attention(qn, qr, kv_b[..., :B_NOPE], kr_all, kv_b[..., B_NOPE:])

    cq = cq.reshape(bsz, n, C_KV_HEADS, C_HEADS // C_KV_HEADS, C_DIM)
    ck = ck.reshape(bsz, n, C_KV_HEADS, C_DIM)
    cv = cv.reshape(bsz, n, C_KV_HEADS, C_DIM)
    if latent:
        o_c = window_attention(apply_rope(cq, cos_a, sin_a), ctx['c_k'].astype(dt), ctx['c_v'].astype(dt),
                               lp['sink_c'], apply_rope(ck, cos_a, sin_a), cv)
    else:
        o_c = window_attention(cq, ck, cv, lp['sink_c'])

    h0 = (ctx['d_re'], ctx['d_im']) if latent else None
    o_d, d_re, d_im = s5_mixer(du, lp['ssm_a_re'], lp['ssm_a_im'], lp['ssm_log_dt'], lp['ssm_b_re'],
                               lp['ssm_b_im'], lp['ssm_c_re'], lp['ssm_c_im'], lp['ssm_d'], lp['w_glu'], h0)

    terms = [jax.nn.sigmoid(h @ lp['w_gate'][:, k * D_MODEL:(k + 1) * D_MODEL]) * (o @ lp['w_branch'][k])
             for k, o in enumerate((o_a, o_b, o_c, o_d))]
    merged = terms[0] + terms[1] + terms[2] + terms[3]
    x = x + gate_m * rms_norm(merged @ lp['w_o'], lp['g_post_mix'])

    h2 = rms_norm(x, lp['g_pre_ffn']) * (1 + scale_f) + shift_f
    f = (jax.nn.silu(h2 @ lp['w_ff1']) * (h2 @ lp['w_ff3'])) @ lp['w_ff2']
    x = x + gate_f * rms_norm(f, lp['g_post_ffn'])

    if latent:
        return x, None
    new_ctx = {'a_k': ak.reshape(bsz, n, A_HEADS, 2 * A_DIM), 'a_v': av, 'b_ckv': ckv, 'b_kr': bkr,
               'c_k': ck, 'c_v': cv, 'd_re': d_re, 'd_im': d_im}
    return x, new_ctx


def setup_inputs(seed: int = 0) -> dict:
    keys = list(jax.random.split(jax.random.key(seed), 64))

    def nrm(shape, scale=1.0):
        return jax.random.normal(keys.pop(), shape, jnp.float32) * scale

    def gain(shape):
        return 1.0 + 0.01 * jax.random.normal(keys.pop(), shape, jnp.float32)

    L, D = DEPTH, D_MODEL
    G, N, P = D_GROUPS, D_STATE, D_GROUP
    return {
        'x_prompt': nrm((BATCH, SEQ, D)),
        'x_sample': nrm((DEC_BATCH, DEC_SEQ, D)),
        'cache_a_k': nrm((DEC_BATCH, L, PAST_LEN, A_HEADS, 2 * A_DIM)),
        'cache_a_v': nrm((DEC_BATCH, L, PAST_LEN, A_HEADS, 2 * A_DIM)),
        'cache_b_ckv': nrm((DEC_BATCH, L, PAST_LEN, B_KVLORA)),
        'cache_b_kr': nrm((DEC_BATCH, L, PAST_LEN, B_ROPE)),
        'cache_c_k': nrm((DEC_BATCH, L, PAST_LEN, C_KV_HEADS, C_DIM)),
        'cache_c_v': nrm((DEC_BATCH, L, PAST_LEN, C_KV_HEADS, C_DIM)),
        'state_d_re': nrm((DEC_BATCH, L, N_DIRS, G, N), 0.5),
        'state_d_im': nrm((DEC_BATCH, L, N_DIRS, G, N), 0.5),
        'c': nrm((DEC_BATCH, D)),
        'c_ctx': nrm((D,)),
        'w_mod': nrm((L, D, 6 * D), 0.5 * D ** -0.5),
        'b_mod': nrm((L, 6 * D), 0.01),
        'g_pre_mix': gain((L, D)),
        'g_post_mix': gain((L, D)),
        'g_pre_ffn': gain((L, D)),
        'g_post_ffn': gain((L, D)),
        'w_in': nrm((L, D, IN_WIDTH), D ** -0.5),
        'w_gate': nrm((L, D, N_BRANCH * D), D ** -0.5),
        'lam_q1': nrm((L, A_DIM), 0.1),
        'lam_k1': nrm((L, A_DIM), 0.1),
        'lam_q2': nrm((L, A_DIM), 0.1),
        'lam_k2': nrm((L, A_DIM), 0.1),
        'g_a_sub': gain((L, 2 * A_DIM)),
        'g_b_q': gain((L, B_QLORA)),
        'g_b_kv': gain((L, B_KVLORA)),
        'w_b_uq': nrm((L, B_QLORA, B_HEADS * (B_NOPE + B_ROPE)), B_QLORA ** -0.5),
        'w_b_ukv': nrm((L, B_KVLORA, B_HEADS * (B_NOPE + B_V)), B_KVLORA ** -0.5),
        'sink_c': nrm((L, C_HEADS), 0.1),
        'ssm_a_re': -0.5 + nrm((L, N_DIRS, G, N), 0.01),
        'ssm_a_im': math.pi * jnp.arange(N, dtype=jnp.float32) + nrm((L, N_DIRS, G, N), 0.01),
        'ssm_log_dt': jax.random.uniform(keys.pop(), (L, N_DIRS, G), jnp.float32,
                                         minval=math.log(DT_MIN), maxval=math.log(DT_MAX)),
        'ssm_b_re': nrm((L, N_DIRS, G, N, P), (2 * P) ** -0.5),
        'ssm_b_im': nrm((L, N_DIRS, G, N, P), (2 * P) ** -0.5),
        'ssm_c_re': nrm((L, N_DIRS, G, P, N), N ** -0.5),
        'ssm_c_im': nrm((L, N_DIRS, G, P, N), N ** -0.5),
        'ssm_d': nrm((L, MIX_W)),
        'w_glu': nrm((L, MIX_W, MIX_W), MIX_W ** -0.5),
        'w_branch': nrm((L, N_BRANCH, MIX_W, D), MIX_W ** -0.5),
        'w_o': nrm((L, D, D), D ** -0.5),
        'w_ff1': nrm((L, D, FF_HIDDEN), D ** -0.5),
        'w_ff3': nrm((L, D, FF_HIDDEN), D ** -0.5),
        'w_ff2': nrm((L, FF_HIDDEN, D), FF_HIDDEN ** -0.5),
    }


def reference(x_prompt, x_sample, cache_a_k, cache_a_v, cache_b_ckv, cache_b_kr, cache_c_k, cache_c_v,
              state_d_re, state_d_im, c, c_ctx, w_mod, b_mod, g_pre_mix, g_post_mix, g_pre_ffn, g_post_ffn,
              w_in, w_gate, lam_q1, lam_k1, lam_q2, lam_k2, g_a_sub, g_b_q, g_b_kv, w_b_uq, w_b_ukv, sink_c,
              ssm_a_re, ssm_a_im, ssm_log_dt, ssm_b_re, ssm_b_im, ssm_c_re, ssm_c_im, ssm_d, w_glu,
              w_branch, w_o, w_ff1, w_ff3, w_ff2):
    def layer_params(l):
        return {'w_mod': w_mod[l], 'b_mod': b_mod[l], 'g_pre_mix': g_pre_mix[l], 'g_post_mix': g_post_mix[l],
                'g_pre_ffn': g_pre_ffn[l], 'g_post_ffn': g_post_ffn[l], 'w_in': w_in[l], 'w_gate': w_gate[l],
                'lam_q1': lam_q1[l], 'lam_k1': lam_k1[l], 'lam_q2': lam_q2[l], 'lam_k2': lam_k2[l],
                'g_a_sub': g_a_sub[l], 'g_b_q': g_b_q[l], 'g_b_kv': g_b_kv[l], 'w_b_uq': w_b_uq[l],
                'w_b_ukv': w_b_ukv[l], 'sink_c': sink_c[l], 'ssm_a_re': ssm_a_re[l], 'ssm_a_im': ssm_a_im[l],
                'ssm_log_dt': ssm_log_dt[l], 'ssm_b_re': ssm_b_re[l], 'ssm_b_im': ssm_b_im[l],
                'ssm_c_re': ssm_c_re[l], 'ssm_c_im': ssm_c_im[l], 'ssm_d': ssm_d[l], 'w_glu': w_glu[l],
                'w_branch': w_branch[l], 'w_o': w_o[l], 'w_ff1': w_ff1[l], 'w_ff3': w_ff3[l], 'w_ff2': w_ff2[l]}

    y_prompt = x_prompt
    ctx_out = []
    for l in range(DEPTH):
        y_prompt, new_ctx = trunk_layer(l, y_prompt, c_ctx[None, :], layer_params(l), None)
        ctx_out.append(new_ctx)

    y_sample = x_sample
    for l in range(DEPTH):
        cached = {'a_k': cache_a_k[:, l], 'a_v': cache_a_v[:, l], 'b_ckv': cache_b_ckv[:, l],
                  'b_kr': cache_b_kr[:, l], 'c_k': cache_c_k[:, l], 'c_v': cache_c_v[:, l],
                  'd_re': state_d_re[:, l], 'd_im': state_d_im[:, l]}
        y_sample, _ = trunk_layer(l, y_sample, c, layer_params(l), cached)

    def stack_layers(name):
        return jnp.stack([cx[name] for cx in ctx_out], axis=1)

    return (y_prompt, y_sample, stack_layers('a_k'), stack_layers('a_v'), stack_layers('b_ckv'),
            stack_layers('b_kr'), stack_layers('c_k'), stack_layers('c_v'), stack_layers('d_re'),
            stack_layers('d_im'))
```

```python
import functools
import math

import jax
import jax.numpy as jnp
from jax import lax
from jax.experimental import pallas as pl
from jax.experimental.pallas import tpu as pltpu

F32 = jnp.float32
BF16 = jnp.bfloat16

D_MODEL = 1024
DEPTH = 2
GRID_W = 64
Q_BLOCK = 128
ROPE_BASE = 10000.0
RMS_EPS = 1e-6
NEG_INF = -1e30
MIX_W = D_MODEL // 2
N_BRANCH = 4
A_HEADS = 4
A_DIM = 64
B_HEADS = 8
B_NOPE = 64
B_ROPE = 32
B_V = 64
B_QLORA = 256
B_KVLORA = 128
C_HEADS = 8
C_KV_HEADS = 2
C_REP = C_HEADS // C_KV_HEADS
C_DIM = 64
D_GROUP = 16
D_GROUPS = MIX_W // D_GROUP
D_STATE = 64
N_DIRS = 2
FF_HIDDEN = ((8 * D_MODEL // 3 + 255) // 256) * 256

LANES = 128
S5_CHUNK = 16
S5_CW = S5_CHUNK * D_GROUP
VMEM_LIMIT = 56 * 1024 * 1024
ROW_TILE = 256
Q_TILE = 256

COL_AQ, COL_AK, COL_AV = 0, 512, 1024
COL_CQ, COL_CK, COL_CV = 1536, 2048, 2176
COL_DU = 2304
COL_BCQ, COL_BCKV, COL_BKR = 2816, 3072, 3200
IN_COLS = 3328
B_KR_LANE = B_NOPE


def _cparams(sem):
    return pltpu.CompilerParams(dimension_semantics=sem, vmem_limit_bytes=VMEM_LIMIT)


def _dot(a, b):
    return jnp.dot(a, b, preferred_element_type=F32)


def _dot_nt(a, b):
    return lax.dot_general(a, b, (((1,), (1,)), ((), ())), preferred_element_type=F32)


def _rms(x, g):
    return x * lax.rsqrt(jnp.mean(x * x, axis=-1, keepdims=True) + RMS_EPS) * g


def _rope(x, c, sa, sb, half):
    w = x.shape[-1]
    return x * c + pltpu.roll(x, w - half, 1) * sa + pltpu.roll(x, half, 1) * sb


def _mod_kernel(c_ref, w_ref, b_ref, o_ref):
    c = c_ref[...]
    o_ref[0] = _dot(jax.nn.silu(c).astype(BF16), w_ref[0].astype(BF16)) + b_ref[0]


def _modulation(cond, w_mod, b_mod):
    nblk = 6
    return pl.pallas_call(
        _mod_kernel,
        grid=(DEPTH, nblk),
        in_specs=[pl.BlockSpec((8, D_MODEL), lambda l, j: (0, 0)),
                  pl.BlockSpec((1, D_MODEL, D_MODEL), lambda l, j: (l, 0, j)),
                  pl.BlockSpec((1, 1, D_MODEL), lambda l, j: (l, 0, j))],
        out_specs=pl.BlockSpec((1, 8, D_MODEL), lambda l, j: (l, 0, j)),
        out_shape=jax.ShapeDtypeStruct((DEPTH, 8, 6 * D_MODEL), F32),
        compiler_params=_cparams(("arbitrary", "arbitrary")),
        name="modulation",
    )(cond, w_mod, b_mod.reshape(DEPTH, 1, 6 * D_MODEL))


def _premix_kernel(latent, *refs):
    if latent:
        (x_ref, mod_ref, g_ref, win_ref, gbq_ref, wuq_ref, gbkv_ref, wukv_ref,
         ca_ref, saa_ref, sba_ref, cb_ref, sab_ref, sbb_ref,
         hb_ref, qa_ref, ka_ref, va_ref, qb_ref, kb_ref, vb_ref, qc_ref, kc_ref, vc_ref, du_ref) = refs
    else:
        (x_ref, mod_ref, g_ref, win_ref, gbq_ref, wuq_ref, gbkv_ref, wukv_ref,
         hb_ref, qa_ref, ka_ref, va_ref, qb_ref, kb_ref, vb_ref, qc_ref, kc_ref, vc_ref, du_ref,
         akf_ref, avf_ref, ckvf_ref, krf_ref, ckf_ref, cvf_ref) = refs

    x = x_ref[...]
    shift = mod_ref[0, 0:1, :]
    scale = mod_ref[0, 1:2, :]
    h = _rms(x, g_ref[...]) * (1.0 + scale) + shift
    hb = h.astype(BF16)
    hb_ref[...] = hb
    proj = _dot(hb, win_ref[...])

    if latent:
        ca, saa, sba = ca_ref[...], saa_ref[...], sba_ref[...]
        cb, sab, sbb = cb_ref[...], sab_ref[...], sbb_ref[...]
        rope_a = lambda t: _rope(t, ca, saa, sba, A_DIM // 2)
        rope_b = lambda t: _rope(t, cb, sab, sbb, B_ROPE // 2)
    else:
        rope_a = rope_b = lambda t: t

    def blk(col, i):
        return proj[:, col + i * LANES: col + (i + 1) * LANES]

    a_scale = A_DIM ** -0.5
    for i in range(A_HEADS):
        sl = slice(i * LANES, (i + 1) * LANES)
        qa_ref[:, sl] = (rope_a(blk(COL_AQ, i)) * a_scale).astype(BF16)
        ka_ref[:, sl] = rope_a(blk(COL_AK, i)).astype(BF16)
    va_ref[...] = proj[:, COL_AV:COL_AV + 512].astype(BF16)

    c_scale = C_DIM ** -0.5
    for i in range(C_REP):
        sl = slice(i * LANES, (i + 1) * LANES)
        qc_ref[:, sl] = (rope_a(blk(COL_CQ, i)) * c_scale).astype(BF16)
    kc_ref[...] = rope_a(blk(COL_CK, 0)).astype(BF16)
    vc_ref[...] = blk(COL_CV, 0).astype(BF16)

    du_ref[...] = proj[:, COL_DU:COL_DU + MIX_W].astype(BF16)

    b_scale = (B_NOPE + B_ROPE) ** -0.5
    cqn = _rms(proj[:, COL_BCQ:COL_BCQ + B_QLORA], gbq_ref[...])
    qb = _dot(cqn.astype(BF16), wuq_ref[...])
    ckv = _rms(proj[:, COL_BCKV:COL_BCKV + B_KVLORA], gbkv_ref[...])
    kvb = _dot(ckv.astype(BF16), wukv_ref[...])
    kr_pad = rope_b(blk(COL_BKR, 0))
    for i in range(B_HEADS):
        sl = slice(i * LANES, (i + 1) * LANES)
        qb_ref[:, sl] = (rope_b(qb[:, sl]) * b_scale).astype(BF16)
        kb_ref[:, sl] = (kvb[:, sl] + kr_pad).astype(BF16)
    vb_ref[...] = kvb[:, B_HEADS * LANES:].astype(BF16)

    if not latent:
        akf_ref[...] = proj[:, COL_AK:COL_AK + 512]
        avf_ref[...] = proj[:, COL_AV:COL_AV + 512]
        ckvf_ref[...] = ckv
        krf_ref[...] = kr_pad[:, B_KR_LANE:B_KR_LANE + B_ROPE]
        ckf_ref[...] = blk(COL_CK, 0)
        cvf_ref[...] = blk(COL_CV, 0)


def _premix(latent, x2, mod6, g_pre, w_in, g_bq, w_uq, g_bkv, w_ukv, rope_tabs, seq):
    m = x2.shape[0]
    tm = ROW_TILE
    tiles_per_seq = seq // tm
    nb_mod = mod6.shape[0]
    row = lambda w: pl.BlockSpec((tm, w), lambda i: (i, 0))
    full = lambda a: pl.BlockSpec(a.shape, lambda i: (0,) * a.ndim)
    if nb_mod == 1:
        mod_spec = pl.BlockSpec((1, 6, D_MODEL), lambda i: (0, 0, 0))
    else:
        mod_spec = pl.BlockSpec((1, 6, D_MODEL), lambda i: (i // tiles_per_seq, 0, 0))
    in_specs = [row(D_MODEL), mod_spec, full(g_pre), full(w_in), full(g_bq), full(w_uq), full(g_bkv), full(w_ukv)]
    args = [x2, mod6, g_pre, w_in, g_bq, w_uq, g_bkv, w_ukv]
    if latent:
        tab_spec = pl.BlockSpec((tm, LANES), lambda i: (i % tiles_per_seq, 0))
        in_specs += [tab_spec] * 6
        args += list(rope_tabs)
    widths = [(D_MODEL, BF16), (512, BF16), (512, BF16), (512, BF16), (B_HEADS * LANES, BF16),
              (B_HEADS * LANES, BF16), (B_HEADS * B_V, BF16), (512, BF16), (LANES, BF16), (LANES, BF16),
              (MIX_W, BF16)]
    if not latent:
        widths += [(512, F32), (512, F32), (B_KVLORA, F32), (B_ROPE, F32), (LANES, F32), (LANES, F32)]
    out_specs = [row(w) for w, _ in widths]
    out_shape = [jax.ShapeDtypeStruct((m, w), dt) for w, dt in widths]
    return pl.pallas_call(
        functools.partial(_premix_kernel, latent),
        grid=(m // tm,),
        in_specs=in_specs, out_specs=out_specs, out_shape=out_shape,
        compiler_params=_cparams(("arbitrary",)),
        name="premix_latent" if latent else "premix_context",
    )(*args)


def _kvprep_kernel(ckv_ref, kr_ref, wukv_ref, kb_ref, vb_ref):
    kvb = _dot(ckv_ref[...].astype(BF16), wukv_ref[...])
    kr_pad = kr_ref[...]
    for i in range(B_HEADS):
        sl = slice(i * LANES, (i + 1) * LANES)
        kb_ref[:, sl] = (kvb[:, sl] + kr_pad).astype(BF16)
    vb_ref[...] = kvb[:, B_HEADS * LANES:].astype(BF16)


def _kvprep(ckv2, kr_pad2, w_ukv):
    m = ckv2.shape[0]
    tm = ROW_TILE
    return pl.pallas_call(
        _kvprep_kernel,
        grid=(m // tm,),
        in_specs=[pl.BlockSpec((tm, B_KVLORA), lambda i: (i, 0)),
                  pl.BlockSpec((tm, LANES), lambda i: (i, 0)),
                  pl.BlockSpec(w_ukv.shape, lambda i: (0, 0))],
        out_specs=[pl.BlockSpec((tm, B_HEADS * LANES), lambda i: (i, 0)),
                   pl.BlockSpec((tm, B_HEADS * B_V), lambda i: (i, 0))],
        out_shape=[jax.ShapeDtypeStruct((m, B_HEADS * LANES), BF16),
                   jax.ShapeDtypeStruct((m, B_HEADS * B_V), BF16)],
        compiler_params=_cparams(("arbitrary",)),
        name="kvprep_b",
    )(ckv2, kr_pad2, w_ukv)


def _softmax_pv(q, k, v):
    s = _dot_nt(q, k)
    m = jnp.max(s, axis=-1, keepdims=True)
    p = jnp.exp(s - m)
    l = jnp.sum(p, axis=-1, keepdims=True)
    return _dot(p.astype(BF16), v) / l


def _attn_a_kernel(lam_init, q_ref, k_ref, v_ref, lq1_ref, lk1_ref, lq2_ref, lk2_ref, gsub_ref, o_ref):
    q = q_ref[0]
    k = k_ref[0]
    v = v_ref[0]
    lane = lax.broadcasted_iota(jnp.int32, q.shape, 1)
    zero = jnp.zeros_like(q)
    o0 = _softmax_pv(jnp.where(lane < A_DIM, q, zero), k, v)
    o1 = _softmax_pv(jnp.where(lane >= A_DIM, q, zero), k, v)
    lam = (jnp.exp(jnp.sum(lq1_ref[...] * lk1_ref[...], axis=-1, keepdims=True))
           - jnp.exp(jnp.sum(lq2_ref[...] * lk2_ref[...], axis=-1, keepdims=True)) + lam_init)
    o = o0 - lam * o1
    o_ref[0] = (_rms(o, gsub_ref[...]) * (1.0 - lam_init)).astype(BF16)


def _attn_a(lam_init, q, k, v, lq1, lk1, lq2, lk2, g_sub):
    bsz, n, _ = q.shape
    t = k.shape[1]
    tq = Q_TILE
    small = lambda a: pl.BlockSpec(a.shape, lambda b, h, i: (0, 0))
    return pl.pallas_call(
        functools.partial(_attn_a_kernel, lam_init),
        grid=(bsz, A_HEADS, n // tq),
        in_specs=[pl.BlockSpec((1, tq, LANES), lambda b, h, i: (b, i, h)),
                  pl.BlockSpec((1, t, LANES), lambda b, h, i: (b, 0, h)),
                  pl.BlockSpec((1, t, LANES), lambda b, h, i: (b, 0, h)),
                  small(lq1), small(lk1), small(lq2), small(lk2), small(g_sub)],
        out_specs=pl.BlockSpec((1, tq, LANES), lambda b, h, i: (b, i, h)),
        out_shape=jax.ShapeDtypeStruct((bsz, n, A_HEADS * LANES), BF16),
        compiler_params=_cparams(("arbitrary", "arbitrary", "arbitrary")),
        name="attn_a",
    )(q, k, v, lq1, lk1, lq2, lk2, g_sub)


def _attn_b_kernel(q_ref, k_ref, v_ref, o_ref):
    q = q_ref[0]
    k = k_ref[0]
    v = v_ref[0]
    o0 = _softmax_pv(q[:, :LANES], k[:, :LANES], v)
    o1 = _softmax_pv(q[:, LANES:], k[:, LANES:], v)
    lane = lax.broadcasted_iota(jnp.int32, o0.shape, 1)
    o_ref[0] = jnp.where(lane < B_V, o0, o1).astype(BF16)


def _attn_b(q, k, v):
    bsz, n, _ = q.shape
    t = k.shape[1]
    tq = Q_TILE
    return pl.pallas_call(
        _attn_b_kernel,
        grid=(bsz, B_HEADS // 2, n // tq),
        in_specs=[pl.BlockSpec((1, tq, 2 * LANES), lambda b, h, i: (b, i, h)),
                  pl.BlockSpec((1, t, 2 * LANES), lambda b, h, i: (b, 0, h)),
                  pl.BlockSpec((1, t, LANES), lambda b, h, i: (b, 0, h))],
        out_specs=pl.BlockSpec((1, tq, LANES), lambda b, h, i: (b, i, h)),
        out_shape=jax.ShapeDtypeStruct((bsz, n, B_HEADS * B_V), BF16),
        compiler_params=_cparams(("arbitrary", "arbitrary", "arbitrary")),
        name="attn_b",
    )(q, k, v)


def _attn_c_kernel(latent, nblk, *refs):
    if latent:
        q_ref, kx_ref, vx_ref, sink_ref, kp_ref, kc_ref, kn_ref, vp_ref, vc_ref, vn_ref, o_ref = refs
    else:
        q_ref, kx_ref, vx_ref, sink_ref, o_ref = refs
    j = pl.program_id(1)
    q = q_ref[0]
    kx = kx_ref[0]
    vx = vx_ref[0]
    lane = lax.broadcasted_iota(jnp.int32, (Q_BLOCK, LANES), 1)
    if latent:
        row = lax.broadcasted_iota(jnp.int32, (Q_BLOCK, Q_BLOCK), 0)
        col = lax.broadcasted_iota(jnp.int32, (Q_BLOCK, Q_BLOCK), 1)
        ok_prev = (col >= row) & (j > 0)
        ok_next = (col <= row) & (j < nblk - 1)
        kp, kc, kn = kp_ref[0], kc_ref[0], kn_ref[0]
        vp, vc, vn = vp_ref[0], vc_ref[0], vn_ref[0]
    for r in range(C_REP):
        qr = q[:, r * LANES:(r + 1) * LANES]
        outs = []
        for g in range(C_KV_HEADS):
            in_half = (lane >= g * C_DIM) & (lane < (g + 1) * C_DIM)
            qh = jnp.where(in_half, qr, jnp.zeros_like(qr))
            sink = sink_ref[r * C_KV_HEADS + g: r * C_KV_HEADS + g + 1, 0:1]
            sx = _dot_nt(qh, kx)
            m = jnp.maximum(jnp.max(sx, axis=-1, keepdims=True), sink)
            if latent:
                sp = jnp.where(ok_prev, _dot_nt(qh, kp), NEG_INF)
                sc = _dot_nt(qh, kc)
                sn = jnp.where(ok_next, _dot_nt(qh, kn), NEG_INF)
                m = jnp.maximum(m, jnp.max(jnp.maximum(jnp.maximum(sp, sc), sn), axis=-1, keepdims=True))
            px = jnp.exp(sx - m)
            l = jnp.sum(px, axis=-1, keepdims=True) + jnp.exp(sink - m)
            o = _dot(px.astype(BF16), vx)
            if latent:
                pp, pc, pn = jnp.exp(sp - m), jnp.exp(sc - m), jnp.exp(sn - m)
                l = l + jnp.sum(pp + pc + pn, axis=-1, keepdims=True)
                o = o + _dot(pp.astype(BF16), vp) + _dot(pc.astype(BF16), vc) + _dot(pn.astype(BF16), vn)
            outs.append(o / l)
        o_ref[0, :, r * LANES:(r + 1) * LANES] = jnp.where(lane < C_DIM, outs[0], outs[1]).astype(BF16)


def _attn_c(q, kx, vx, sink_rows, k_lat=None, v_lat=None):
    bsz, n, _ = q.shape
    tc = kx.shape[1]
    nblk = n // Q_BLOCK
    latent = k_lat is not None
    in_specs = [pl.BlockSpec((1, Q_BLOCK, 512), lambda b, j: (b, j, 0)),
                pl.BlockSpec((1, tc, LANES), lambda b, j: (b, 0, 0)),
                pl.BlockSpec((1, tc, LANES), lambda b, j: (b, 0, 0)),
                pl.BlockSpec(sink_rows.shape, lambda b, j: (0, 0))]
    args = [q, kx, vx, sink_rows]
    if latent:
        prev = pl.BlockSpec((1, Q_BLOCK, LANES), lambda b, j: (b, jnp.maximum(j - 1, 0), 0))
        cur = pl.BlockSpec((1, Q_BLOCK, LANES), lambda b, j: (b, j, 0))
        nxt = pl.BlockSpec((1, Q_BLOCK, LANES), lambda b, j: (b, jnp.minimum(j + 1, nblk - 1), 0))
        in_specs += [prev, cur, nxt, prev, cur, nxt]
        args += [k_lat, k_lat, k_lat, v_lat, v_lat, v_lat]
    return pl.pallas_call(
        functools.partial(_attn_c_kernel, latent, nblk),
        grid=(bsz, nblk),
        in_specs=in_specs,
        out_specs=pl.BlockSpec((1, Q_BLOCK, 512), lambda b, j: (b, j, 0)),
        out_shape=jax.ShapeDtypeStruct((bsz, n, 512), BF16),
        compiler_params=_cparams(("arbitrary", "arbitrary")),
        name="attn_c_latent" if latent else "attn_c_context",
    )(*args)


def _s5_prepare_kernel(are_ref, aim_ref, ldt_ref, btre_ref, btim_ref, cre_ref, cim_ref,
                       k_ref, wre_ref, wim_ref, vre_ref, vim_ref, a16re_ref, a16im_ref):
    hi = lax.Precision.HIGHEST
    are = are_ref[0, 0]
    aim = aim_ref[0, 0]
    dt = jnp.exp(ldt_ref[0, 0])
    nj = S5_CHUNK + 1
    jj = lax.broadcasted_iota(jnp.int32, (nj, D_STATE), 0).astype(F32)
    mag = jnp.exp(jj * (are * dt))
    ang = jj * (aim * dt)
    pre = mag * jnp.cos(ang)
    pim = mag * jnp.sin(ang)
    xr = pre[1:2] - 1.0
    xi = pim[1:2]
    den = are * are + aim * aim
    fr = (xr * are + xi * aim) / den
    fi = (xi * are - xr * aim) / den
    btre = btre_ref[0, 0]
    btim = btim_ref[0, 0]
    bbre = fr * btre - fi * btim
    bbim = fr * btim + fi * btre
    cre = cre_ref[0, 0]
    cim = cim_ref[0, 0]
    cj_re, cj_im = [], []
    for j in range(nj):
        pr = pre[j:j + 1]
        pi = pim[j:j + 1]
        cr = cre * pr - cim * pi
        ci = cre * pi + cim * pr
        if j < S5_CHUNK:
            cj_re.append(cr)
            cj_im.append(ci)
            wre_ref[0, 0, j] = pr * bbre - pi * bbim
            wim_ref[0, 0, j] = pr * bbim + pi * bbre
        if j >= 1:
            vre_ref[0, 0, j - 1] = cr
            vim_ref[0, 0, j - 1] = -ci
    call_re = jnp.concatenate(cj_re, axis=0)
    call_im = jnp.concatenate(cj_im, axis=0)
    dn = (((1,), (1,)), ((), ()))
    k_ref[0, 0] = (lax.dot_general(call_re, bbre, dn, precision=hi, preferred_element_type=F32)
                   - lax.dot_general(call_im, bbim, dn, precision=hi, preferred_element_type=F32))
    a16re_ref[0, 0] = pre[S5_CHUNK:S5_CHUNK + 1]
    a16im_ref[0, 0] = pim[S5_CHUNK:S5_CHUNK + 1]


def _s5_prepare(a_re, a_im, log_dt, b_re, b_im, c_re, c_im):
    g, n, p = D_GROUPS, D_STATE, D_GROUP
    v4 = lambda a: a.reshape(N_DIRS, g, 1, a.shape[-1])
    spec = lambda *s: pl.BlockSpec((1, 1) + s, lambda d, i: (d, i) + (0,) * len(s))
    bt_re = jnp.swapaxes(b_re, -1, -2)
    bt_im = jnp.swapaxes(b_im, -1, -2)
    outs = pl.pallas_call(
        _s5_prepare_kernel,
        grid=(N_DIRS, g),
        in_specs=[spec(1, n), spec(1, n), spec(1, 1), spec(p, n), spec(p, n), spec(p, n), spec(p, n)],
        out_specs=[spec(S5_CHUNK * p, p), spec(S5_CHUNK, p, n), spec(S5_CHUNK, p, n),
                   spec(S5_CHUNK, p, n), spec(S5_CHUNK, p, n), spec(1, n), spec(1, n)],
        out_shape=[jax.ShapeDtypeStruct((N_DIRS, g, S5_CHUNK * p, p), F32)]
        + [jax.ShapeDtypeStruct((N_DIRS, g, S5_CHUNK, p, n), F32)] * 4
        + [jax.ShapeDtypeStruct((N_DIRS, g, 1, n), F32)] * 2,
        compiler_params=_cparams(("arbitrary", "arbitrary")),
        name="s5_prepare",
    )(v4(a_re), v4(a_im), log_dt.reshape(N_DIRS, g, 1, 1), bt_re, bt_im, c_re, c_im)
    kmat, wre, wim, vre, vim, a16re, a16im = outs
    kmat = kmat.reshape(N_DIRS, g, S5_CHUNK, p, p)

    s_idx = jnp.arange(S5_CHUNK)[:, None]
    t_idx = jnp.arange(S5_CHUNK)[None, :]

    def toeplitz(kd, lag):
        blocks = jnp.where((lag >= 0)[None, :, :, None, None], kd[:, jnp.clip(lag, 0, S5_CHUNK - 1)], 0.0)
        return blocks.transpose(0, 1, 4, 2, 3).reshape(g, S5_CW, S5_CW).astype(BF16)

    t_fwd = toeplitz(kmat[0], t_idx - s_idx)
    t_bwd = toeplitz(kmat[1], s_idx - t_idx)

    w_parts = [wre[0][:, ::-1], wim[0][:, ::-1], wre[1], wim[1]]
    w_cat = jnp.stack([w.reshape(g, S5_CW, n) for w in w_parts], axis=2)
    slot = (jnp.arange(g) % 2)[:, None, None, None, None] == jnp.arange(2)[None, None, None, :, None]
    w_pair = jnp.where(slot, w_cat[:, :, :, None, :], 0.0).reshape(g, S5_CW, 4 * 2 * n).astype(BF16)

    v_parts = [vre[0], vim[0], vre[1][:, ::-1], vim[1][:, ::-1]]
    v_cat = jnp.stack([v.transpose(0, 3, 1, 2).reshape(g, n, S5_CW) for v in v_parts], axis=1)
    v_cat = v_cat.reshape(g // 2, 2, 4, n, S5_CW)
    eye = (jnp.arange(2)[:, None] == jnp.arange(2)[None, :])[None, :, None, None, :, None]
    v_pair = jnp.where(eye, v_cat[:, :, :, :, None, :], 0.0)
    v_pair = v_pair.transpose(0, 2, 1, 3, 4, 5).reshape(g // 2, 4, 2 * n, 2 * S5_CW).astype(BF16)

    a16 = [a16re[0].reshape(1, g * n), a16im[0].reshape(1, g * n),
           a16re[1].reshape(1, g * n), a16im[1].reshape(1, g * n)]
    return t_fwd, t_bwd, w_pair, v_pair, a16


def _s5_state_in_kernel(u_ref, w_ref, o0_ref, o1_ref, o2_ref, o3_ref):
    s = _dot(u_ref[0], w_ref[0]) + _dot(u_ref[1], w_ref[1])
    for i, o_ref in enumerate((o0_ref, o1_ref, o2_ref, o3_ref)):
        o_ref[...] = s[:, i * LANES:(i + 1) * LANES]


def _s5_state_in(u_r, w_pair):
    g, rows, _ = u_r.shape
    gn = g * D_STATE
    return pl.pallas_call(
        _s5_state_in_kernel,
        grid=(g // 2,),
        in_specs=[pl.BlockSpec((2, rows, S5_CW), lambda i: (i, 0, 0)),
                  pl.BlockSpec((2, S5_CW, 4 * LANES), lambda i: (i, 0, 0))],
        out_specs=[pl.BlockSpec((rows, LANES), lambda i: (0, i))] * 4,
        out_shape=[jax.ShapeDtypeStruct((rows, gn), F32)] * 4,
        compiler_params=_cparams(("arbitrary",)),
        name="s5_state_in",
    )(u_r, w_pair)


def _s5_scan_kernel(nchunk, s0re_ref, s0im_ref, s1re_ref, s1im_ref, a0re_ref, a0im_ref, a1re_ref, a1im_ref,
                    h0re_ref, h0im_ref, h1re_ref, h1im_ref,
                    p0re_ref, p0im_ref, p1re_ref, p1im_ref, f0re_ref, f0im_ref, f1re_ref, f1im_ref):
    a0re, a0im, a1re, a1im = a0re_ref[...], a0im_ref[...], a1re_ref[...], a1im_ref[...]

    def body(k, carry):
        r0, i0, r1, i1 = carry
        kb = nchunk - 1 - k
        p0re_ref[k] = r0
        p0im_ref[k] = i0
        p1re_ref[kb] = r1
        p1im_ref[kb] = i1
        n_r0 = a0re * r0 - a0im * i0 + s0re_ref[k]
        n_i0 = a0re * i0 + a0im * r0 + s0im_ref[k]
        n_r1 = a1re * r1 - a1im * i1 + s1re_ref[kb]
        n_i1 = a1re * i1 + a1im * r1 + s1im_ref[kb]
        return n_r0, n_i0, n_r1, n_i1

    r0, i0, r1, i1 = lax.fori_loop(0, nchunk, body,
                                   (h0re_ref[...], h0im_ref[...], h1re_ref[...], h1im_ref[...]))
    f0re_ref[...] = r0
    f0im_ref[...] = i0
    f1re_ref[...] = r1
    f1im_ref[...] = i1


def _s5_scan(s_in, a16, h0, nchunk, bsz):
    gn = D_GROUPS * D_STATE
    lb = 2 * LANES
    seq_spec = pl.BlockSpec((nchunk, bsz, lb), lambda i: (0, 0, i))
    a_spec = pl.BlockSpec((1, lb), lambda i: (0, i))
    h_spec = pl.BlockSpec((bsz, lb), lambda i: (0, i))
    outs = pl.pallas_call(
        functools.partial(_s5_scan_kernel, nchunk),
        grid=(gn // lb,),
        in_specs=[seq_spec] * 4 + [a_spec] * 4 + [h_spec] * 4,
        out_specs=[seq_spec] * 4 + [h_spec] * 4,
        out_shape=[jax.ShapeDtypeStruct((nchunk, bsz, gn), F32)] * 4 + [jax.ShapeDtypeStruct((bsz, gn), F32)] * 4,
        compiler_params=_cparams(("arbitrary",)),
        name="s5_scan",
    )(*s_in, *a16, *h0)
    return outs[:4], outs[4:]


def _s5_output_kernel(u_ref, tf_ref, tb_ref, p0_ref, p1_ref, p2_ref, p3_ref, v_ref, y_ref):
    ys = None
    for i, p_ref in enumerate((p0_ref, p1_ref, p2_ref, p3_ref)):
        t = _dot(p_ref[...].astype(BF16), v_ref[0, i])
        ys = t if ys is None else ys + t
    for i in range(2):
        u = u_ref[i]
        y = _dot(u, tf_ref[i]) + _dot(u, tb_ref[i]) + ys[:, i * S5_CW:(i + 1) * S5_CW]
        y_ref[i] = y.astype(BF16)


def _s5_output(u_r, t_fwd, t_bwd, p_states, v_pair):
    g, rows, _ = u_r.shape
    pair = lambda w: pl.BlockSpec((2, w, S5_CW), lambda i: (i, 0, 0))
    return pl.pallas_call(
        _s5_output_kernel,
        grid=(g // 2,),
        in_specs=[pair(rows), pair(S5_CW), pair(S5_CW)]
        + [pl.BlockSpec((rows, LANES), lambda i: (0, i))] * 4
        + [pl.BlockSpec((1, 4, LANES, 2 * S5_CW), lambda i: (i, 0, 0, 0))],
        out_specs=pair(rows),
        out_shape=jax.ShapeDtypeStruct((g, rows, S5_CW), BF16),
        compiler_params=_cparams(("arbitrary",)),
        name="s5_output",
    )(u_r, t_fwd, t_bwd, *p_states, v_pair)


def _s5_mixer(du, mats, h0):
    t_fwd, t_bwd, w_pair, v_pair, a16 = mats
    bsz, n, _ = du.shape
    nchunk = n // S5_CHUNK
    rows = nchunk * bsz
    gn = D_GROUPS * D_STATE
    u_r = du.reshape(bsz, nchunk, S5_CHUNK, D_GROUPS, D_GROUP).transpose(3, 1, 0, 2, 4).reshape(D_GROUPS, rows, S5_CW)
    s_in = _s5_state_in(u_r, w_pair)
    s_in = [s.reshape(nchunk, bsz, gn) for s in s_in]
    p_states, finals = _s5_scan(s_in, a16, h0, nchunk, bsz)
    p_states = [p.reshape(rows, gn) for p in p_states]
    y_r = _s5_output(u_r, t_fwd, t_bwd, p_states, v_pair)
    y = y_r.reshape(D_GROUPS, nchunk, bsz, S5_CHUNK, D_GROUP).transpose(2, 1, 3, 0, 4).reshape(bsz, n, MIX_W)
    return y, finals


def _merge_kernel(x_ref, hb_ref, oa_ref, ob_ref, oc_ref, y_ref, u_ref, mod_ref, dskip_ref, wglu_ref,
                  wgate_ref, wbr_ref, wo_ref, gpost_ref, xo_ref):
    yv = y_ref[...].astype(F32) + dskip_ref[...] * u_ref[...].astype(F32)
    yg = jax.nn.gelu(yv)
    od = yg * jax.nn.sigmoid(_dot(yg.astype(BF16), wglu_ref[...]))
    hb = hb_ref[...]
    branches = (oa_ref[...], ob_ref[...], oc_ref[...], od.astype(BF16))
    merged = None
    for k, o in enumerate(branches):
        gate = jax.nn.sigmoid(_dot(hb, wgate_ref[:, k * D_MODEL:(k + 1) * D_MODEL]))
        term = gate * _dot(o, wbr_ref[k])
        merged = term if merged is None else merged + term
    z = _dot(merged.astype(BF16), wo_ref[...])
    gate_m = mod_ref[0, 2:3, :]
    xo_ref[...] = x_ref[...] + gate_m * _rms(z, gpost_ref[...])


def _merge(x2, hb, oa, ob, oc, y, du, mod6, d_skip, w_glu, w_gate, w_branch, w_o, g_post, seq):
    m = x2.shape[0]
    tm = ROW_TILE
    tiles_per_seq = seq // tm
    row = lambda w: pl.BlockSpec((tm, w), lambda i: (i, 0))
    full = lambda a: pl.BlockSpec(a.shape, lambda i: (0,) * a.ndim)
    if mod6.shape[0] == 1:
        mod_spec = pl.BlockSpec((1, 6, D_MODEL), lambda i: (0, 0, 0))
    else:
        mod_spec = pl.BlockSpec((1, 6, D_MODEL), lambda i: (i // tiles_per_seq, 0, 0))
    return pl.pallas_call(
        _merge_kernel,
        grid=(m // tm,),
        in_specs=[row(D_MODEL), row(D_MODEL), row(MIX_W), row(MIX_W), row(MIX_W), row(MIX_W), row(MIX_W),
                  mod_spec, full(d_skip), full(w_glu), full(w_gate), full(w_branch), full(w_o), full(g_post)],
        out_specs=row(D_MODEL),
        out_shape=jax.ShapeDtypeStruct((m, D_MODEL), F32),
        compiler_params=_cparams(("arbitrary",)),
        name="merge",
    )(x2, hb, oa, ob, oc, y, du, mod6, d_skip, w_glu, w_gate, w_branch, w_o, g_post)


def _ffn_kernel(x_ref, mod_ref, gpre_ref, w1_ref, w3_ref, w2_ref, gpost_ref, xo_ref):
    x = x_ref[...]
    shift = mod_ref[0, 3:4, :]
    scale = mod_ref[0, 4:5, :]
    gate = mod_ref[0, 5:6, :]
    h2 = (_rms(x, gpre_ref[...]) * (1.0 + scale) + shift).astype(BF16)
    a = _dot(h2, w1_ref[...])
    b = _dot(h2, w3_ref[...])
    f = _dot((jax.nn.silu(a) * b).astype(BF16), w2_ref[...])
    xo_ref[...] = x + gate * _rms(f, gpost_ref[...])


def _ffn(x2, mod6, g_pre, w1, w3, w2, g_post, seq):
    m = x2.shape[0]
    tm = ROW_TILE
    tiles_per_seq = seq // tm
    row = lambda w: pl.BlockSpec((tm, w), lambda i: (i, 0))
    full = lambda a: pl.BlockSpec(a.shape, lambda i: (0,) * a.ndim)
    if mod6.shape[0] == 1:
        mod_spec = pl.BlockSpec((1, 6, D_MODEL), lambda i: (0, 0, 0))
    else:
        mod_spec = pl.BlockSpec((1, 6, D_MODEL), lambda i: (i // tiles_per_seq, 0, 0))
    return pl.pallas_call(
        _ffn_kernel,
        grid=(m // tm,),
        in_specs=[row(D_MODEL), mod_spec, full(g_pre), full(w1), full(w3), full(w2), full(g_post)],
        out_specs=row(D_MODEL),
        out_shape=jax.ShapeDtypeStruct((m, D_MODEL), F32),
        compiler_params=_cparams(("arbitrary",)),
        name="ffn",
    )(x2, mod6, g_pre, w1, w3, w2, g_post)


def _rope_tables(n, rot_dim, lane_off):
    rows = n // GRID_W
    pos_row = jnp.repeat(jnp.arange(rows, dtype=F32), GRID_W)
    pos_col = jnp.tile(jnp.arange(GRID_W, dtype=F32), rows)
    n_freq = rot_dim // 4
    inv_freq = ROPE_BASE ** (-jnp.arange(n_freq, dtype=F32) / n_freq)
    ang = jnp.concatenate([pos_row[:, None] * inv_freq, pos_col[:, None] * inv_freq], axis=-1)
    cos, sin = jnp.cos(ang), jnp.sin(ang)
    zero = jnp.zeros_like(sin)
    c = jnp.concatenate([cos, cos], axis=-1)
    sa = jnp.concatenate([-sin, zero], axis=-1)
    sb = jnp.concatenate([zero, sin], axis=-1)
    if rot_dim == LANES // 2 and lane_off == 0:
        return tuple(jnp.tile(t, (1, 2)) for t in (c, sa, sb))
    pad = lambda t, fill: jnp.concatenate(
        [jnp.full((n, lane_off), fill, F32), t, jnp.full((n, LANES - lane_off - rot_dim), fill, F32)], axis=-1)
    return pad(c, 1.0), pad(sa, 0.0), pad(sb, 0.0)


def _layer_weights(l, w_in, w_gate, w_b_uq, w_b_ukv, sink_c, w_glu, w_branch, w_o, w_ff1, w_ff3, w_ff2):
    wi = w_in[l]
    aq, ak, av = wi[:, 0:512], wi[:, 512:1024], wi[:, 1024:1536]
    bcq, bckv, bkr = wi[:, 1536:1792], wi[:, 1792:1920], wi[:, 1920:1952]
    cq, ck, cv, du = wi[:, 1952:2464], wi[:, 2464:2592], wi[:, 2592:2720], wi[:, 2720:3232]
    cq_perm = cq.reshape(D_MODEL, C_KV_HEADS, C_REP, C_DIM).transpose(0, 2, 1, 3).reshape(D_MODEL, 512)
    zeros = lambda w: jnp.zeros((D_MODEL, w), F32)
    bkr_pad = jnp.concatenate([zeros(B_KR_LANE), bkr, zeros(LANES - B_KR_LANE - B_ROPE)], axis=1)
    w_in_r = jnp.concatenate([aq, ak, av, cq_perm, ck, cv, du, bcq, bckv, bkr_pad], axis=1).astype(BF16)

    uq = w_b_uq[l].reshape(B_QLORA, B_HEADS, B_NOPE + B_ROPE)
    uq = jnp.concatenate([uq, jnp.zeros((B_QLORA, B_HEADS, LANES - B_NOPE - B_ROPE), F32)], axis=-1)
    w_uq_r = uq.reshape(B_QLORA, B_HEADS * LANES).astype(BF16)
    ukv = w_b_ukv[l].reshape(B_KVLORA, B_HEADS, B_NOPE + B_V)
    kn = jnp.concatenate([ukv[..., :B_NOPE], jnp.zeros((B_KVLORA, B_HEADS, LANES - B_NOPE), F32)], axis=-1)
    w_ukv_r = jnp.concatenate([kn.reshape(B_KVLORA, B_HEADS * LANES),
                               ukv[..., B_NOPE:].reshape(B_KVLORA, B_HEADS * B_V)], axis=1).astype(BF16)

    sink_perm = sink_c[l].reshape(C_KV_HEADS, C_REP).T.reshape(C_HEADS)
    sink_rows = jnp.broadcast_to(sink_perm[:, None], (C_HEADS, LANES))

    wbr = w_branch[l]
    wbr_c = wbr[2].reshape(C_KV_HEADS, C_REP, C_DIM, D_MODEL).transpose(1, 0, 2, 3).reshape(MIX_W, D_MODEL)
    w_branch_r = jnp.stack([wbr[0], wbr[1], wbr_c, wbr[3]], axis=0).astype(BF16)
    return dict(w_in=w_in_r, w_uq=w_uq_r, w_ukv=w_ukv_r, sink_rows=sink_rows,
                w_gate=w_gate[l].astype(BF16), w_glu=w_glu[l].astype(BF16), w_branch=w_branch_r,
                w_o=w_o[l].astype(BF16), w_ff1=w_ff1[l].astype(BF16), w_ff3=w_ff3[l].astype(BF16),
                w_ff2=w_ff2[l].astype(BF16))


def _trunk_layer(l, x, mod6, lw, sp, s5_mats, rope_tabs, ctx):
    latent = ctx is not None
    bsz, n, _ = x.shape
    m = bsz * n
    x2 = x.reshape(m, D_MODEL)
    row1 = lambda v: v.reshape(1, -1)

    outs = _premix(latent, x2, mod6, row1(sp['g_pre_mix']), lw['w_in'], row1(sp['g_b_q']), lw['w_uq'],
                   row1(sp['g_b_kv']), lw['w_ukv'], rope_tabs, n)
    hb, qa, ka, va, qb, kb, vb, qc, kc, vc, du = outs[:11]
    r3 = lambda t: t.reshape(bsz, n, t.shape[-1])
    qa, ka, va, qb, kb, vb, qc, kc, vc, du = map(r3, (qa, ka, va, qb, kb, vb, qc, kc, vc, du))

    lam_init = 0.8 - 0.6 * math.exp(-0.3 * l)
    lam_args = (row1(sp['lam_q1']), row1(sp['lam_k1']), row1(sp['lam_q2']), row1(sp['lam_k2']), row1(sp['g_a_sub']))
    if latent:
        past = ctx['a_k'].shape[1]
        ka_all = jnp.concatenate([ka, ctx['a_k'].reshape(bsz, past, 512).astype(BF16)], axis=1)
        va_all = jnp.concatenate([va, ctx['a_v'].reshape(bsz, past, 512).astype(BF16)], axis=1)
        kr_pad = jnp.pad(ctx['b_kr'].reshape(bsz * past, B_ROPE),
                         ((0, 0), (B_KR_LANE, LANES - B_KR_LANE - B_ROPE)))
        kb_ctx, vb_ctx = _kvprep(ctx['b_ckv'].reshape(bsz * past, B_KVLORA), kr_pad, lw['w_ukv'])
        kb_all = jnp.concatenate([kb, kb_ctx.reshape(bsz, past, -1)], axis=1)
        vb_all = jnp.concatenate([vb, vb_ctx.reshape(bsz, past, -1)], axis=1)
        o_a = _attn_a(lam_init, qa, ka_all, va_all, *lam_args)
        o_b = _attn_b(qb, kb_all, vb_all)
        o_c = _attn_c(qc, ctx['c_k'].reshape(bsz, past, LANES).astype(BF16),
                      ctx['c_v'].reshape(bsz, past, LANES).astype(BF16), lw['sink_rows'], kc, vc)
        gn = D_GROUPS * D_STATE
        h0 = [ctx['d_re'][:, 0].reshape(bsz, gn), ctx['d_im'][:, 0].reshape(bsz, gn),
              ctx['d_re'][:, 1].reshape(bsz, gn), ctx['d_im'][:, 1].reshape(bsz, gn)]
    else:
        o_a = _attn_a(lam_init, qa, ka, va, *lam_args)
        o_b = _attn_b(qb, kb, vb)
        o_c = _attn_c(qc, kc, vc, lw['sink_rows'])
        h0 = [jnp.zeros((bsz, D_GROUPS * D_STATE), F32)] * 4
    y_s5, finals = _s5_mixer(du, s5_mats, h0)

    f2 = lambda t: t.reshape(m, t.shape[-1])
    x2 = _merge(x2, hb, f2(o_a), f2(o_b), f2(o_c), f2(y_s5), f2(du), mod6, row1(sp['ssm_d']), lw['w_glu'],
                lw['w_gate'], lw['w_branch'], lw['w_o'], row1(sp['g_post_mix']), n)
    x2 = _ffn(x2, mod6, row1(sp['g_pre_ffn']), lw['w_ff1'], lw['w_ff3'], lw['w_ff2'], row1(sp['g_post_ffn']), n)
    x = x2.reshape(bsz, n, D_MODEL)
    if latent:
        return x, None
    akf, avf, ckvf, krf, ckf, cvf = outs[11:]
    st = lambda t: t.reshape(bsz, D_GROUPS, D_STATE)
    new_ctx = {'a_k': akf.reshape(bsz, n, A_HEADS, 2 * A_DIM), 'a_v': avf.reshape(bsz, n, A_HEADS, 2 * A_DIM),
               'b_ckv': ckvf.reshape(bsz, n, B_KVLORA), 'b_kr': krf.reshape(bsz, n, B_ROPE),
               'c_k': ckf.reshape(bsz, n, C_KV_HEADS, C_DIM), 'c_v': cvf.reshape(bsz, n, C_KV_HEADS, C_DIM),
               'd_re': jnp.stack([st(finals[0]), st(finals[2])], axis=1),
               'd_im': jnp.stack([st(finals[1]), st(finals[3])], axis=1)}
    return x, new_ctx


def kernel(x_prompt, x_sample, cache_a_k, cache_a_v, cache_b_ckv, cache_b_kr, cache_c_k, cache_c_v, state_d_re, state_d_im, c, c_ctx, w_mod, b_mod, g_pre_mix, g_post_mix, g_pre_ffn, g_post_ffn, w_in, w_gate, lam_q1, lam_k1, lam_q2, lam_k2, g_a_sub, g_b_q, g_b_kv, w_b_uq, w_b_ukv, sink_c, ssm_a_re, ssm_a_im, ssm_log_dt, ssm_b_re, ssm_b_im, ssm_c_re, ssm_c_im, ssm_d, w_glu, w_branch, w_o, w_ff1, w_ff3, w_ff2):
    dec_b, dec_n, _ = x_sample.shape
    assert dec_b + 1 <= 8

    cond = jnp.concatenate([c_ctx[None, :], c, jnp.zeros((8 - 1 - dec_b, D_MODEL), F32)], axis=0)
    mod = _modulation(cond, w_mod, b_mod)

    small = dict(g_pre_mix=g_pre_mix, g_post_mix=g_post_mix, g_pre_ffn=g_pre_ffn, g_post_ffn=g_post_ffn,
                 lam_q1=lam_q1, lam_k1=lam_k1, lam_q2=lam_q2, lam_k2=lam_k2, g_a_sub=g_a_sub, g_b_q=g_b_q,
                 g_b_kv=g_b_kv, ssm_d=ssm_d)
    rope_tabs = _rope_tables(dec_n, A_DIM, 0) + _rope_tables(dec_n, B_ROPE, B_KR_LANE)

    layers = []
    for l in range(DEPTH):
        lw = _layer_weights(l, w_in, w_gate, w_b_uq, w_b_ukv, sink_c, w_glu, w_branch, w_o, w_ff1, w_ff3, w_ff2)
        sp = {k: v[l] for k, v in small.items()}
        s5_mats = _s5_prepare(ssm_a_re[l], ssm_a_im[l], ssm_log_dt[l], ssm_b_re[l], ssm_b_im[l],
                              ssm_c_re[l], ssm_c_im[l])
        mod_l = mod[l].reshape(8, 6, D_MODEL)
        layers.append((lw, sp, s5_mats, mod_l))

    y_prompt = x_prompt
    ctx_out = []
    for l, (lw, sp, s5_mats, mod_l) in enumerate(layers):
        y_prompt, new_ctx = _trunk_layer(l, y_prompt, mod_l[0:1], lw, sp, s5_mats, None, None)
        ctx_out.append(new_ctx)

    y_sample = x_sample
    for l, (lw, sp, s5_mats, mod_l) in enumerate(layers):
        cached = {'a_k': cache_a_k[:, l], 'a_v': cache_a_v[:, l], 'b_ckv': cache_b_ckv[:, l],
                  'b_kr': cache_b_kr[:, l], 'c_k': cache_c_k[:, l], 'c_v': cache_c_v[:, l],
                  'd_re': state_d_re[:, l], 'd_im': state_d_im[:, l]}
        y_sample, _ = _trunk_layer(l, y_sample, mod_l[1:1 + dec_b], lw, sp, s5_mats, rope_tabs, cached)

    stack = lambda name: jnp.stack([cx[name] for cx in ctx_out], axis=1)
    return (y_prompt, y_sample, stack('a_k'), stack('a_v'), stack('b_ckv'), stack('b_kr'),
            stack('c_k'), stack('c_v'), stack('d_re'), stack('d_im'))
```

```python
import functools
import math

import jax
import jax.numpy as jnp
from jax import lax
from jax.experimental import pallas as pl
from jax.experimental.pallas import tpu as pltpu

F32 = jnp.float32
BF16 = jnp.bfloat16

D_MODEL = 1024
DEPTH = 2
GRID_W = 64
Q_BLOCK = 128
ROPE_BASE = 10000.0
RMS_EPS = 1e-6
NEG_INF = -1e30
LOG2E = math.log2(math.e)
MIX_W = D_MODEL // 2
N_BRANCH = 4
A_HEADS = 4
A_DIM = 64
B_HEADS = 8
B_NOPE = 64
B_ROPE = 32
B_V = 64
B_QLORA = 256
B_KVLORA = 128
C_HEADS = 8
C_KV_HEADS = 2
C_REP = C_HEADS // C_KV_HEADS
C_DIM = 64
D_GROUP = 16
D_GROUPS = MIX_W // D_GROUP
D_STATE = 64
N_DIRS = 2
FF_HIDDEN = ((8 * D_MODEL // 3 + 255) // 256) * 256

LANES = 128
S5_CHUNK = 16
S5_SG = LANES // D_GROUP
S5_FLAT = S5_CHUNK * LANES
S5_SLB = S5_SG * D_STATE // LANES
S5_ROWS = 256
VMEM_LIMIT = 56 * 1024 * 1024
ROW_TILE = 512
Q_TILE = 256
KEY_CHUNK = 512
A_HPS = 2
B_HPS = 4

COL_AQ, COL_AK, COL_AV = 0, 512, 1024
COL_CQ, COL_CK, COL_CV = 1536, 2048, 2176
COL_DU = 2304
COL_BCQ, COL_BCKV, COL_BKR = 2816, 3072, 3200
IN_COLS = 3328
B_KR_LANE = B_NOPE


def _cparams(sem):
    return pltpu.CompilerParams(dimension_semantics=sem, vmem_limit_bytes=VMEM_LIMIT)


def _resident(a):
    return pl.BlockSpec(a.shape, lambda i: (0,) * a.ndim, pipeline_mode=pl.Buffered(1))


def _row_tile(m):
    return min(ROW_TILE, m)


def _dot(a, b):
    return jnp.dot(a, b, preferred_element_type=F32)


def _dot_nt(a, b):
    return lax.dot_general(a, b, (((1,), (1,)), ((), ())), preferred_element_type=F32)


def _rms(x, g):
    return x * lax.rsqrt(jnp.mean(x * x, axis=-1, keepdims=True) + RMS_EPS) * g


def _rope(x, c, sa, sb, half):
    w = x.shape[-1]
    return x * c + pltpu.roll(x, w - half, 1) * sa + pltpu.roll(x, half, 1) * sb


def _mod_kernel(c_ref, w_ref, b_ref, o_ref):
    c = c_ref[...]
    o_ref[0] = _dot(jax.nn.silu(c).astype(BF16), w_ref[0].astype(BF16)) + b_ref[0]


def _modulation(cond, w_mod, b_mod):
    nblk = 6
    return pl.pallas_call(
        _mod_kernel,
        grid=(DEPTH, nblk),
        in_specs=[pl.BlockSpec((8, D_MODEL), lambda l, j: (0, 0)),
                  pl.BlockSpec((1, D_MODEL, D_MODEL), lambda l, j: (l, 0, j)),
                  pl.BlockSpec((1, 1, D_MODEL), lambda l, j: (l, 0, j))],
        out_specs=pl.BlockSpec((1, 8, D_MODEL), lambda l, j: (l, 0, j)),
        out_shape=jax.ShapeDtypeStruct((DEPTH, 8, 6 * D_MODEL), F32),
        compiler_params=_cparams(("arbitrary", "arbitrary")),
        name="modulation",
    )(cond, w_mod, b_mod.reshape(DEPTH, 1, 6 * D_MODEL))


def _premix_kernel(latent, *refs):
    if latent:
        (x_ref, mod_ref, g_ref, win_ref, gbq_ref, wuq_ref, gbkv_ref, wukv_ref,
         ca_ref, saa_ref, sba_ref, cb_ref, sab_ref, sbb_ref,
         hb_ref, qa_ref, ka_ref, va_ref, qb_ref, kb_ref, vb_ref, qc_ref, kc_ref, vc_ref, du_ref) = refs
    else:
        (x_ref, mod_ref, g_ref, win_ref, gbq_ref, wuq_ref, gbkv_ref, wukv_ref,
         hb_ref, qa_ref, ka_ref, va_ref, qb_ref, kb_ref, vb_ref, qc_ref, kc_ref, vc_ref, du_ref,
         akf_ref, avf_ref, ckvf_ref, krf_ref, ckf_ref, cvf_ref) = refs

    x = x_ref[...]
    shift = mod_ref[0, 0:1, :]
    scale = mod_ref[0, 1:2, :]
    h = _rms(x, g_ref[...]) * (1.0 + scale) + shift
    hb = h.astype(BF16)
    hb_ref[...] = hb
    proj = _dot(hb, win_ref[...])

    if latent:
        ca, saa, sba = ca_ref[...], saa_ref[...], sba_ref[...]
        cb, sab, sbb = cb_ref[...], sab_ref[...], sbb_ref[...]
        rope_a = lambda t: _rope(t, ca, saa, sba, A_DIM // 2)
        rope_b = lambda t: _rope(t, cb, sab, sbb, B_ROPE // 2)
    else:
        rope_a = rope_b = lambda t: t

    def blk(col, i):
        return proj[:, col + i * LANES: col + (i + 1) * LANES]

    a_scale = A_DIM ** -0.5 * LOG2E
    for i in range(A_HEADS):
        sl = slice(i * LANES, (i + 1) * LANES)
        qa_ref[:, sl] = (rope_a(blk(COL_AQ, i)) * a_scale).astype(BF16)
        ka_ref[:, sl] = rope_a(blk(COL_AK, i)).astype(BF16)
    va_ref[...] = proj[:, COL_AV:COL_AV + 512].astype(BF16)

    c_scale = C_DIM ** -0.5 * LOG2E
    for i in range(C_REP):
        sl = slice(i * LANES, (i + 1) * LANES)
        qc_ref[:, sl] = (rope_a(blk(COL_CQ, i)) * c_scale).astype(BF16)
    kc_ref[...] = rope_a(blk(COL_CK, 0)).astype(BF16)
    vc_ref[...] = blk(COL_CV, 0).astype(BF16)

    du_ref[...] = proj[:, COL_DU:COL_DU + MIX_W]

    b_scale = (B_NOPE + B_ROPE) ** -0.5 * LOG2E
    cqn = _rms(proj[:, COL_BCQ:COL_BCQ + B_QLORA], gbq_ref[...])
    qb = _dot(cqn.astype(BF16), wuq_ref[...])
    ckv = _rms(proj[:, COL_BCKV:COL_BCKV + B_KVLORA], gbkv_ref[...])
    kvb = _dot(ckv.astype(BF16), wukv_ref[...])
    kr_pad = rope_b(blk(COL_BKR, 0))
    for i in range(B_HEADS):
        sl = slice(i * LANES, (i + 1) * LANES)
        qb_ref[:, sl] = (rope_b(qb[:, sl]) * b_scale).astype(BF16)
        kb_ref[:, sl] = (kvb[:, sl] + kr_pad).astype(BF16)
    vb_ref[...] = kvb[:, B_HEADS * LANES:].astype(BF16)

    if not latent:
        akf_ref[...] = proj[:, COL_AK:COL_AK + 512]
        avf_ref[...] = proj[:, COL_AV:COL_AV + 512]
        ckvf_ref[...] = ckv
        krf_ref[...] = kr_pad[:, B_KR_LANE:B_KR_LANE + B_ROPE]
        ckf_ref[...] = blk(COL_CK, 0)
        cvf_ref[...] = blk(COL_CV, 0)


def _premix(latent, x2, mod6, g_pre, w_in, g_bq, w_uq, g_bkv, w_ukv, rope_tabs, seq):
    m = x2.shape[0]
    tm = _row_tile(m)
    tiles_per_seq = max(seq // tm, 1)
    nb_mod = mod6.shape[0]
    assert nb_mod == 1 or seq % tm == 0
    row = lambda w: pl.BlockSpec((tm, w), lambda i: (i, 0))
    full = _resident
    if nb_mod == 1:
        mod_spec = pl.BlockSpec((1, 6, D_MODEL), lambda i: (0, 0, 0))
    else:
        mod_spec = pl.BlockSpec((1, 6, D_MODEL), lambda i: (i // tiles_per_seq, 0, 0))
    in_specs = [row(D_MODEL), mod_spec, full(g_pre), full(w_in), full(g_bq), full(w_uq), full(g_bkv), full(w_ukv)]
    args = [x2, mod6, g_pre, w_in, g_bq, w_uq, g_bkv, w_ukv]
    if latent:
        tab_spec = pl.BlockSpec((tm, LANES), lambda i: (i % tiles_per_seq, 0))
        in_specs += [tab_spec] * 6
        args += list(rope_tabs)
    widths = [(D_MODEL, BF16), (512, BF16), (512, BF16), (512, BF16), (B_HEADS * LANES, BF16),
              (B_HEADS * LANES, BF16), (B_HEADS * B_V, BF16), (512, BF16), (LANES, BF16), (LANES, BF16),
              (MIX_W, F32)]
    if not latent:
        widths += [(512, F32), (512, F32), (B_KVLORA, F32), (B_ROPE, F32), (LANES, F32), (LANES, F32)]
    out_specs = [row(w) for w, _ in widths]
    out_shape = [jax.ShapeDtypeStruct((m, w), dt) for w, dt in widths]
    return pl.pallas_call(
        functools.partial(_premix_kernel, latent),
        grid=(m // tm,),
        in_specs=in_specs, out_specs=out_specs, out_shape=out_shape,
        compiler_params=_cparams(("arbitrary",)),
        name="premix_latent" if latent else "premix_context",
    )(*args)


def _kvprep_kernel(ckv_ref, kr_ref, wukv_ref, kb_ref, vb_ref):
    kvb = _dot(ckv_ref[...].astype(BF16), wukv_ref[...])
    kr_pad = kr_ref[...]
    for i in range(B_HEADS):
        sl = slice(i * LANES, (i + 1) * LANES)
        kb_ref[:, sl] = (kvb[:, sl] + kr_pad).astype(BF16)
    vb_ref[...] = kvb[:, B_HEADS * LANES:].astype(BF16)


def _kvprep(ckv2, kr_pad2, w_ukv):
    m = ckv2.shape[0]
    tm = _row_tile(m)
    return pl.pallas_call(
        _kvprep_kernel,
        grid=(m // tm,),
        in_specs=[pl.BlockSpec((tm, B_KVLORA), lambda i: (i, 0)),
                  pl.BlockSpec((tm, LANES), lambda i: (i, 0)),
                  _resident(w_ukv)],
        out_specs=[pl.BlockSpec((tm, B_HEADS * LANES), lambda i: (i, 0)),
                   pl.BlockSpec((tm, B_HEADS * B_V), lambda i: (i, 0))],
        out_shape=[jax.ShapeDtypeStruct((m, B_HEADS * LANES), BF16),
                   jax.ShapeDtypeStruct((m, B_HEADS * B_V), BF16)],
        compiler_params=_cparams(("arbitrary",)),
        name="kvprep_b",
    )(ckv2, kr_pad2, w_ukv)


def _attend(maps, chunks):
    def scores(mp, ch):
        s = _dot_nt(mp[0], ch[0](mp[1]))
        return s if ch[2] is None else s + ch[2](mp[1])

    outs = []
    cur = [scores(maps[0], ch) for ch in chunks]
    for mi, (_, tag, sink) in enumerate(maps):
        m = functools.reduce(jnp.maximum, [jnp.max(s, axis=-1, keepdims=True) for s in cur])
        l = None
        if sink is not None:
            m = jnp.maximum(m, sink)
            l = jnp.exp2(sink - m)
        nxt, acc = [], None
        for ci, ch in enumerate(chunks):
            if mi + 1 < len(maps):
                nxt.append(scores(maps[mi + 1], ch))
            p = jnp.exp2(cur[ci] - m)
            ls = jnp.sum(p, axis=-1, keepdims=True)
            pv = _dot(p.astype(BF16), ch[1](tag))
            l = ls if l is None else l + ls
            acc = pv if acc is None else acc + pv
        outs.append(acc / l)
        cur = nxt
    return outs


def _ref_chunks(kv_refs):
    chunks = []
    for i in range(0, len(kv_refs), 2):
        k_ref, v_ref = kv_refs[i], kv_refs[i + 1]
        t = k_ref.shape[1]
        assert t % KEY_CHUNK == 0 or t < KEY_CHUNK
        size = min(KEY_CHUNK, t)
        for c in range(0, t, size):
            chunks.append((lambda tag, r=k_ref, c=c, n=size: r[0, c:c + n, tag[0]],
                           lambda tag, r=v_ref, c=c, n=size: r[0, c:c + n, tag[1]], None))
    return chunks


def _attn_a_kernel(lam_init, nkv, q_ref, *refs):
    kv_refs = refs[:2 * nkv]
    lq1_ref, lk1_ref, lq2_ref, lk2_ref, gsub_ref, o_ref = refs[2 * nkv:]
    lam = (jnp.exp(jnp.sum(lq1_ref[...] * lk1_ref[...], axis=-1, keepdims=True))
           - jnp.exp(jnp.sum(lq2_ref[...] * lk2_ref[...], axis=-1, keepdims=True)) + lam_init)
    lane = lax.broadcasted_iota(jnp.int32, (q_ref.shape[1], LANES), 1)
    maps = []
    for h in range(A_HPS):
        sl = slice(h * LANES, (h + 1) * LANES)
        q = q_ref[0, :, sl]
        zero = jnp.zeros_like(q)
        maps.append((jnp.where(lane < A_DIM, q, zero), (sl, sl), None))
        maps.append((jnp.where(lane >= A_DIM, q, zero), (sl, sl), None))
    outs = _attend(maps, _ref_chunks(kv_refs))
    for h in range(A_HPS):
        o = outs[2 * h] - lam * outs[2 * h + 1]
        o_ref[0, :, h * LANES:(h + 1) * LANES] = (_rms(o, gsub_ref[...]) * (1.0 - lam_init)).astype(BF16)


def _attn_a(lam_init, q, kv, lq1, lk1, lq2, lk2, g_sub):
    bsz, n, _ = q.shape
    tq = min(Q_TILE, n)
    w = A_HPS * LANES
    small = lambda a: pl.BlockSpec(a.shape, lambda b, h, i: (0, 0))
    return pl.pallas_call(
        functools.partial(_attn_a_kernel, lam_init, len(kv) // 2),
        grid=(bsz, A_HEADS // A_HPS, n // tq),
        in_specs=[pl.BlockSpec((1, tq, w), lambda b, h, i: (b, i, h))]
        + [pl.BlockSpec((1, a.shape[1], w), lambda b, h, i: (b, 0, h)) for a in kv]
        + [small(lq1), small(lk1), small(lq2), small(lk2), small(g_sub)],
        out_specs=pl.BlockSpec((1, tq, w), lambda b, h, i: (b, i, h)),
        out_shape=jax.ShapeDtypeStruct((bsz, n, A_HEADS * LANES), BF16),
        compiler_params=_cparams(("arbitrary", "arbitrary", "arbitrary")),
        name="attn_a",
    )(q, *kv, lq1, lk1, lq2, lk2, g_sub)


def _attn_b_kernel(nkv, q_ref, *refs):
    kv_refs, o_ref = refs[:2 * nkv], refs[2 * nkv]
    lane = lax.broadcasted_iota(jnp.int32, (q_ref.shape[1], LANES), 1)
    maps = []
    for h in range(B_HPS):
        sl = slice(h * LANES, (h + 1) * LANES)
        vsl = slice((h // 2) * LANES, (h // 2 + 1) * LANES)
        maps.append((q_ref[0, :, sl], (sl, vsl), None))
    outs = _attend(maps, _ref_chunks(kv_refs))
    for pair in range(B_HPS // 2):
        o_ref[0, :, pair * LANES:(pair + 1) * LANES] = (
            jnp.where(lane < B_V, outs[2 * pair], outs[2 * pair + 1]).astype(BF16))


def _attn_b(q, kv):
    bsz, n, _ = q.shape
    tq = min(Q_TILE, n)
    kv_specs = []
    for i, a in enumerate(kv):
        w = B_HPS * (LANES if i % 2 == 0 else B_V)
        kv_specs.append(pl.BlockSpec((1, a.shape[1], w), lambda b, h, i: (b, 0, h)))
    return pl.pallas_call(
        functools.partial(_attn_b_kernel, len(kv) // 2),
        grid=(bsz, B_HEADS // B_HPS, n // tq),
        in_specs=[pl.BlockSpec((1, tq, B_HPS * LANES), lambda b, h, i: (b, i, h))] + kv_specs,
        out_specs=pl.BlockSpec((1, tq, B_HPS * B_V), lambda b, h, i: (b, i, h)),
        out_shape=jax.ShapeDtypeStruct((bsz, n, B_HEADS * B_V), BF16),
        compiler_params=_cparams(("arbitrary", "arbitrary", "arbitrary")),
        name="attn_b",
    )(q, *kv)


def _attn_c_kernel(latent, nblk, *refs):
    if latent:
        q_ref, kx_ref, vx_ref, sink_ref, kp_ref, kc_ref, kn_ref, vp_ref, vc_ref, vn_ref, bias_ref, o_ref = refs
    else:
        q_ref, kx_ref, vx_ref, sink_ref, o_ref = refs
    q = q_ref[0]
    lane = lax.broadcasted_iota(jnp.int32, (Q_BLOCK, LANES), 1)
    parts = []
    for r in range(C_REP):
        qr = q[:, r * LANES:(r + 1) * LANES]
        for g in range(C_KV_HEADS):
            in_half = (lane >= g * C_DIM) & (lane < (g + 1) * C_DIM)
            parts.append(jnp.where(in_half, qr, jnp.zeros_like(qr)))
    chunks = [(lambda tag: kx_ref[0], lambda tag: vx_ref[0], None)]
    if latent:
        kw = jnp.concatenate([kp_ref[0], kc_ref[0], kn_ref[0]], axis=0)
        vw = jnp.concatenate([vp_ref[0], vc_ref[0], vn_ref[0]], axis=0)
        chunks.append((lambda tag: kw, lambda tag: vw, lambda tag: bias_ref[0]))
    maps = []
    for r in range(C_REP):
        rows = slice(2 * r * Q_BLOCK, (2 * r + 2) * Q_BLOCK)
        maps.append((jnp.concatenate(parts[2 * r:2 * r + 2], axis=0), None, sink_ref[rows, :] * LOG2E))
    outs = _attend(maps, chunks)
    for r, o in enumerate(outs):
        o_ref[0, :, r * LANES:(r + 1) * LANES] = jnp.where(lane < C_DIM, o[:Q_BLOCK], o[Q_BLOCK:]).astype(BF16)


def _attn_c(q, kx, vx, sink_rows, k_lat=None, v_lat=None):
    bsz, n, _ = q.shape
    tc = kx.shape[1]
    nblk = n // Q_BLOCK
    latent = k_lat is not None
    in_specs = [pl.BlockSpec((1, Q_BLOCK, 512), lambda b, j: (b, j, 0)),
                pl.BlockSpec((1, tc, LANES), lambda b, j: (b, 0, 0)),
                pl.BlockSpec((1, tc, LANES), lambda b, j: (b, 0, 0)),
                pl.BlockSpec(sink_rows.shape, lambda b, j: (0, 0))]
    args = [q, kx, vx, sink_rows]
    if latent:
        prev = pl.BlockSpec((1, Q_BLOCK, LANES), lambda b, j: (b, jnp.maximum(j - 1, 0), 0))
        cur = pl.BlockSpec((1, Q_BLOCK, LANES), lambda b, j: (b, j, 0))
        nxt = pl.BlockSpec((1, Q_BLOCK, LANES), lambda b, j: (b, jnp.minimum(j + 1, nblk - 1), 0))
        row = (jnp.arange(C_KV_HEADS * Q_BLOCK) % Q_BLOCK)[:, None]
        col = jnp.arange(3 * Q_BLOCK)[None, :]
        in_prev, in_next = col < Q_BLOCK, col >= 2 * Q_BLOCK
        band = jnp.where(in_prev, col >= row, jnp.where(in_next, col - 2 * Q_BLOCK <= row, True))
        variants = [band & ~in_prev, band, band & ~in_next, band & ~in_prev & ~in_next]
        bias = jnp.stack([jnp.where(ok, 0.0, NEG_INF).astype(F32) for ok in variants], axis=0)
        last = nblk - 1

        def bias_map(b, j):
            first_i = (j == 0).astype(jnp.int32)
            last_i = (j == last).astype(jnp.int32)
            return (1 - first_i + last_i + 2 * first_i * last_i, 0, 0)

        in_specs += [prev, cur, nxt, prev, cur, nxt, pl.BlockSpec((1,) + bias.shape[1:], bias_map)]
        args += [k_lat, k_lat, k_lat, v_lat, v_lat, v_lat, bias]
    return pl.pallas_call(
        functools.partial(_attn_c_kernel, latent, nblk),
        grid=(bsz, nblk),
        in_specs=in_specs,
        out_specs=pl.BlockSpec((1, Q_BLOCK, 512), lambda b, j: (b, j, 0)),
        out_shape=jax.ShapeDtypeStruct((bsz, n, 512), BF16),
        compiler_params=_cparams(("arbitrary", "arbitrary")),
        name="attn_c_latent" if latent else "attn_c_context",
    )(*args)


def _s5_prepare_kernel(are_ref, aim_ref, ldt_ref, btre_ref, btim_ref, cre_ref, cim_ref,
                       k_ref, wre_ref, wim_ref, vre_ref, vim_ref, a16re_ref, a16im_ref):
    p = D_GROUP
    refs = (are_ref, aim_ref, ldt_ref, btre_ref, btim_ref, cre_ref, cim_ref,
            wre_ref, wim_ref, vre_ref, vim_ref, a16re_ref, a16im_ref)
    k_fwd = _s5_prepare_direction(0, *refs)
    k_bwd = _s5_prepare_direction(1, *refs)
    pieces = [k_bwd[(S5_CHUNK - 1 - i) * p:(S5_CHUNK - i) * p] for i in range(S5_CHUNK - 1)]
    pieces.append(k_fwd[0:p] + k_bwd[0:p])
    pieces.append(k_fwd[p:])
    k_ref[0] = jnp.concatenate(pieces, axis=0)


def _s5_prepare_direction(d, are_ref, aim_ref, ldt_ref, btre_ref, btim_ref, cre_ref, cim_ref,
                          wre_ref, wim_ref, vre_ref, vim_ref, a16re_ref, a16im_ref):
    hi = lax.Precision.HIGHEST
    are = are_ref[d, 0]
    aim = aim_ref[d, 0]
    dt = jnp.exp(ldt_ref[d, 0])
    nj = S5_CHUNK + 1
    jj = lax.broadcasted_iota(jnp.int32, (nj, D_STATE), 0).astype(F32)
    mag = jnp.exp(jj * (are * dt))
    ang = jj * (aim * dt)
    pre = mag * jnp.cos(ang)
    pim = mag * jnp.sin(ang)
    xr = pre[1:2] - 1.0
    xi = pim[1:2]
    den = are * are + aim * aim
    fr = (xr * are + xi * aim) / den
    fi = (xi * are - xr * aim) / den
    btre = btre_ref[d, 0]
    btim = btim_ref[d, 0]
    bbre = fr * btre - fi * btim
    bbim = fr * btim + fi * btre
    cre = cre_ref[d, 0]
    cim = cim_ref[d, 0]
    cj_re, cj_im = [], []
    for j in range(nj):
        pr = pre[j:j + 1]
        pi = pim[j:j + 1]
        cr = cre * pr - cim * pi
        ci = cre * pi + cim * pr
        if j < S5_CHUNK:
            cj_re.append(cr)
            cj_im.append(ci)
            wre_ref[d, 0, j] = pr * bbre - pi * bbim
            wim_ref[d, 0, j] = pr * bbim + pi * bbre
        if j >= 1:
            vre_ref[d, 0, j - 1] = cr
            vim_ref[d, 0, j - 1] = -ci
    call_re = jnp.concatenate(cj_re, axis=0)
    call_im = jnp.concatenate(cj_im, axis=0)
    dn = (((1,), (1,)), ((), ()))
    a16re_ref[d, 0] = pre[S5_CHUNK:S5_CHUNK + 1]
    a16im_ref[d, 0] = pim[S5_CHUNK:S5_CHUNK + 1]
    return (lax.dot_general(call_re, bbre, dn, precision=hi, preferred_element_type=F32)
            - lax.dot_general(call_im, bbim, dn, precision=hi, preferred_element_type=F32))


def _s5_prepare(a_re, a_im, log_dt, b_re, b_im, c_re, c_im):
    g, n, p = D_GROUPS, D_STATE, D_GROUP
    v4 = lambda a: a.reshape(N_DIRS, g, 1, a.shape[-1])
    spec = lambda *s: pl.BlockSpec((N_DIRS, 1) + s, lambda i: (0, i) + (0,) * len(s))
    bt_re = jnp.swapaxes(b_re, -1, -2)
    bt_im = jnp.swapaxes(b_im, -1, -2)
    nlag = 2 * S5_CHUNK - 1
    outs = pl.pallas_call(
        _s5_prepare_kernel,
        grid=(g,),
        in_specs=[spec(1, n), spec(1, n), spec(1, 1), spec(p, n), spec(p, n), spec(p, n), spec(p, n)],
        out_specs=[pl.BlockSpec((1, nlag * p, p), lambda i: (i, 0, 0)), spec(S5_CHUNK, p, n), spec(S5_CHUNK, p, n),
                   spec(S5_CHUNK, p, n), spec(S5_CHUNK, p, n), spec(1, n), spec(1, n)],
        out_shape=[jax.ShapeDtypeStruct((g, nlag * p, p), F32)]
        + [jax.ShapeDtypeStruct((N_DIRS, g, S5_CHUNK, p, n), F32)] * 4
        + [jax.ShapeDtypeStruct((N_DIRS, g, 1, n), F32)] * 2,
        compiler_params=_cparams(("arbitrary",)),
        name="s5_prepare",
    )(v4(a_re), v4(a_im), log_dt.reshape(N_DIRS, g, 1, 1), bt_re, bt_im, c_re, c_im)
    kmat, wre, wim, vre, vim, a16re, a16im = outs

    nsg, sgg = g // S5_SG, S5_SG
    eye = jnp.arange(sgg)[:, None] == jnp.arange(sgg)[None, :]
    kmat = kmat.reshape(nsg, sgg, nlag, p, p)
    lag_idx = jnp.arange(S5_CHUNK)[None, :] - jnp.arange(S5_CHUNK)[:, None] + (S5_CHUNK - 1)
    t_g = kmat[:, :, lag_idx]
    t_g = t_g.transpose(0, 2, 1, 5, 3, 4)[:, :, :, :, :, None, :]
    t_sg = jnp.where(eye[None, None, :, None, None, :, None], t_g, 0.0)
    t_sg = t_sg.reshape(nsg, S5_FLAT, S5_FLAT).astype(BF16)

    w_x = jnp.stack([wre[0][:, ::-1], wim[0][:, ::-1], wre[1], wim[1]], axis=0)
    w_x = w_x.reshape(4, nsg, sgg, S5_CHUNK, p, n).transpose(1, 3, 2, 4, 0, 5)
    w_sg = jnp.where(eye[None, None, :, None, None, :, None], w_x[:, :, :, :, :, None, :], 0.0)
    w_sg = w_sg.reshape(nsg, S5_FLAT, 4 * sgg * n).astype(BF16)

    v_x = jnp.stack([vre[0], vim[0], vre[1][:, ::-1], vim[1][:, ::-1]], axis=0)
    v_x = v_x.reshape(4, nsg, sgg, S5_CHUNK, p, n).transpose(1, 0, 2, 5, 3, 4)
    v_sg = jnp.where(eye[None, None, :, None, None, :, None], v_x[:, :, :, :, :, None, :], 0.0)
    v_sg = v_sg.reshape(nsg, 4 * sgg * n, S5_FLAT).astype(BF16)

    a16 = [a16re[0].reshape(1, g * n), a16im[0].reshape(1, g * n),
           a16re[1].reshape(1, g * n), a16im[1].reshape(1, g * n)]
    return t_sg, w_sg, v_sg, a16


def _s5_gather_chunks(du_ref, u_scr, bsz, ck):
    for b in range(bsz):
        for t in range(S5_CHUNK):
            u_scr[b * ck:(b + 1) * ck, t * LANES:(t + 1) * LANES] = (
                du_ref[b, pl.ds(t, ck, stride=S5_CHUNK), :].astype(BF16))


def _s5_state_in_kernel(bsz, bpad, ck, du_ref, w_ref, o0_ref, o1_ref, o2_ref, o3_ref, u_scr):
    _s5_gather_chunks(du_ref, u_scr, bsz, ck)
    s = _dot(u_scr[...], w_ref[0])
    for x, o_ref in enumerate((o0_ref, o1_ref, o2_ref, o3_ref)):
        for j in range(S5_SLB):
            col = (x * S5_SLB + j) * LANES
            for b in range(bpad):
                if b < bsz:
                    o_ref[j, pl.ds(b, ck, stride=bpad), :] = s[b * ck:(b + 1) * ck, col:col + LANES]
                else:
                    o_ref[j, pl.ds(b, ck, stride=bpad), :] = jnp.zeros((ck, LANES), F32)


def _s5_state_in(du, w_sg, bpad, ck):
    bsz, n, _ = du.shape
    nchunk = n // S5_CHUNK
    nsg = w_sg.shape[0]
    return pl.pallas_call(
        functools.partial(_s5_state_in_kernel, bsz, bpad, ck),
        grid=(nsg, nchunk // ck),
        in_specs=[pl.BlockSpec((bsz, ck * S5_CHUNK, LANES), lambda s, r: (0, r, s)),
                  pl.BlockSpec((1, S5_FLAT, 4 * S5_SLB * LANES), lambda s, r: (s, 0, 0),
                               pipeline_mode=pl.Buffered(1))],
        out_specs=[pl.BlockSpec((S5_SLB, ck * bpad, LANES), lambda s, r: (s, r, 0))] * 4,
        out_shape=[jax.ShapeDtypeStruct((nsg * S5_SLB, nchunk * bpad, LANES), F32)] * 4,
        scratch_shapes=[pltpu.VMEM((bsz * ck, S5_FLAT), BF16)],
        compiler_params=_cparams(("arbitrary", "arbitrary")),
        name="s5_state_in",
    )(du, w_sg)


def _s5_scan_kernel(nchunk, bpad, s0re_ref, s0im_ref, s1re_ref, s1im_ref, a0re_ref, a0im_ref, a1re_ref, a1im_ref,
                    h0re_ref, h0im_ref, h1re_ref, h1im_ref,
                    p0re_ref, p0im_ref, p1re_ref, p1im_ref, f0re_ref, f0im_ref, f1re_ref, f1im_ref):
    a0re, a0im, a1re, a1im = a0re_ref[...], a0im_ref[...], a1re_ref[...], a1im_ref[...]

    def body(i, carry):
        r0, i0, r1, i1 = carry
        k = pl.ds(pl.multiple_of(i * bpad, bpad), bpad)
        kb = pl.ds(pl.multiple_of((nchunk - 1 - i) * bpad, bpad), bpad)
        p0re_ref[:, k, :] = r0
        p0im_ref[:, k, :] = i0
        p1re_ref[:, kb, :] = r1
        p1im_ref[:, kb, :] = i1
        n_r0 = a0re * r0 - a0im * i0 + s0re_ref[:, k, :]
        n_i0 = a0re * i0 + a0im * r0 + s0im_ref[:, k, :]
        n_r1 = a1re * r1 - a1im * i1 + s1re_ref[:, kb, :]
        n_i1 = a1re * i1 + a1im * r1 + s1im_ref[:, kb, :]
        return n_r0, n_i0, n_r1, n_i1

    r0, i0, r1, i1 = lax.fori_loop(0, nchunk, body,
                                   (h0re_ref[...], h0im_ref[...], h1re_ref[...], h1im_ref[...]))
    f0re_ref[...] = r0
    f0im_ref[...] = i0
    f1re_ref[...] = r1
    f1im_ref[...] = i1


def _s5_scan(s_in, a16, h0, nchunk, bpad):
    nlb = D_GROUPS * D_STATE // LANES
    lb = 2
    seq_spec = pl.BlockSpec((lb, nchunk * bpad, LANES), lambda i: (i, 0, 0))
    a_spec = pl.BlockSpec((lb, 1, LANES), lambda i: (i, 0, 0))
    h_spec = pl.BlockSpec((lb, bpad, LANES), lambda i: (i, 0, 0))
    outs = pl.pallas_call(
        functools.partial(_s5_scan_kernel, nchunk, bpad),
        grid=(nlb // lb,),
        in_specs=[seq_spec] * 4 + [a_spec] * 4 + [h_spec] * 4,
        out_specs=[seq_spec] * 4 + [h_spec] * 4,
        out_shape=[jax.ShapeDtypeStruct((nlb, nchunk * bpad, LANES), F32)] * 4
        + [jax.ShapeDtypeStruct((nlb, bpad, LANES), F32)] * 4,
        compiler_params=_cparams(("arbitrary",)),
        name="s5_scan",
    )(*s_in, *a16, *h0)
    return outs[:4], outs[4:]


def _s5_output_kernel(bsz, bpad, ck, du_ref, t_ref, v_ref, p0_ref, p1_ref, p2_ref, p3_ref, y_ref, u_scr, p_scr):
    _s5_gather_chunks(du_ref, u_scr, bsz, ck)
    for x, p_ref in enumerate((p0_ref, p1_ref, p2_ref, p3_ref)):
        for j in range(S5_SLB):
            col = (x * S5_SLB + j) * LANES
            for b in range(bsz):
                p_scr[b * ck:(b + 1) * ck, col:col + LANES] = (
                    p_ref[j, pl.ds(b, ck, stride=bpad), :].astype(BF16))
    y = _dot(u_scr[...], t_ref[0]) + _dot(p_scr[...], v_ref[0])
    for b in range(bsz):
        for t in range(S5_CHUNK):
            y_ref[b, pl.ds(t, ck, stride=S5_CHUNK), :] = y[b * ck:(b + 1) * ck, t * LANES:(t + 1) * LANES]


def _s5_output(du, t_sg, v_sg, p_states, bpad, ck):
    bsz, n, _ = du.shape
    nchunk = n // S5_CHUNK
    nsg = t_sg.shape[0]
    tok = pl.BlockSpec((bsz, ck * S5_CHUNK, LANES), lambda s, r: (0, r, s))
    mat = pl.BlockSpec((1, S5_FLAT, S5_FLAT), lambda s, r: (s, 0, 0), pipeline_mode=pl.Buffered(1))
    return pl.pallas_call(
        functools.partial(_s5_output_kernel, bsz, bpad, ck),
        grid=(nsg, nchunk // ck),
        in_specs=[tok, mat, mat] + [pl.BlockSpec((S5_SLB, ck * bpad, LANES), lambda s, r: (s, r, 0))] * 4,
        out_specs=tok,
        out_shape=jax.ShapeDtypeStruct((bsz, n, MIX_W), F32),
        scratch_shapes=[pltpu.VMEM((bsz * ck, S5_FLAT), BF16), pltpu.VMEM((bsz * ck, S5_FLAT), BF16)],
        compiler_params=_cparams(("arbitrary", "arbitrary")),
        name="s5_output",
    )(du, t_sg, v_sg, *p_states)


def _s5_mixer(du, mats, h0):
    t_sg, w_sg, v_sg, a16 = mats
    bsz, n, _ = du.shape
    nchunk = n // S5_CHUNK
    bpad = -(-bsz // 8) * 8
    ck = min(nchunk, max(S5_ROWS // bsz, 1))
    assert nchunk % ck == 0 and (ck * S5_CHUNK) % 8 == 0
    to_blocks = lambda h: h.reshape(h.shape[0], -1, LANES).transpose(1, 0, 2)
    h0 = [to_blocks(jnp.pad(h, ((0, bpad - bsz), (0, 0)))) for h in h0]
    a16 = [to_blocks(a) for a in a16]
    s_in = _s5_state_in(du, w_sg, bpad, ck)
    p_states, finals = _s5_scan(s_in, a16, h0, nchunk, bpad)
    y = _s5_output(du, t_sg, v_sg, p_states, bpad, ck)
    return y, [f.transpose(1, 0, 2).reshape(bpad, -1)[:bsz] for f in finals]


def _merge_kernel(x_ref, hb_ref, oa_ref, ob_ref, oc_ref, y_ref, u_ref, mod_ref, dskip_ref, wglu_ref,
                  wgate_ref, wbr_ref, wo_ref, gpost_ref, xo_ref):
    yv = y_ref[...] + dskip_ref[...] * u_ref[...]
    yg = jax.nn.gelu(yv)
    od = yg * jax.nn.sigmoid(_dot(yg.astype(BF16), wglu_ref[...]))
    hb = hb_ref[...]
    branches = (oa_ref[...], ob_ref[...], oc_ref[...], od.astype(BF16))
    merged = None
    for k, o in enumerate(branches):
        gate = jax.nn.sigmoid(_dot(hb, wgate_ref[:, k * D_MODEL:(k + 1) * D_MODEL]))
        term = gate * _dot(o, wbr_ref[k])
        merged = term if merged is None else merged + term
    z = _dot(merged.astype(BF16), wo_ref[...])
    gate_m = mod_ref[0, 2:3, :]
    xo_ref[...] = x_ref[...] + gate_m * _rms(z, gpost_ref[...])


def _merge(x2, hb, oa, ob, oc, y, du, mod6, d_skip, w_glu, w_gate, w_branch, w_o, g_post, seq):
    m = x2.shape[0]
    tm = _row_tile(m)
    tiles_per_seq = max(seq // tm, 1)
    assert mod6.shape[0] == 1 or seq % tm == 0
    row = lambda w: pl.BlockSpec((tm, w), lambda i: (i, 0))
    full = _resident
    if mod6.shape[0] == 1:
        mod_spec = pl.BlockSpec((1, 6, D_MODEL), lambda i: (0, 0, 0))
    else:
        mod_spec = pl.BlockSpec((1, 6, D_MODEL), lambda i: (i // tiles_per_seq, 0, 0))
    return pl.pallas_call(
        _merge_kernel,
        grid=(m // tm,),
        in_specs=[row(D_MODEL), row(D_MODEL), row(MIX_W), row(MIX_W), row(MIX_W), row(MIX_W), row(MIX_W),
                  mod_spec, full(d_skip), full(w_glu), full(w_gate), full(w_branch), full(w_o), full(g_post)],
        out_specs=row(D_MODEL),
        out_shape=jax.ShapeDtypeStruct((m, D_MODEL), F32),
        compiler_params=_cparams(("arbitrary",)),
        name="merge",
    )(x2, hb, oa, ob, oc, y, du, mod6, d_skip, w_glu, w_gate, w_branch, w_o, g_post)


def _ffn_kernel(x_ref, mod_ref, gpre_ref, w1_ref, w3_ref, w2_ref, gpost_ref, xo_ref):
    x = x_ref[...]
    shift = mod_ref[0, 3:4, :]
    scale = mod_ref[0, 4:5, :]
    gate = mod_ref[0, 5:6, :]
    h2 = (_rms(x, gpre_ref[...]) * (1.0 + scale) + shift).astype(BF16)
    a = _dot(h2, w1_ref[...])
    b = _dot(h2, w3_ref[...])
    f = _dot((jax.nn.silu(a) * b).astype(BF16), w2_ref[...])
    xo_ref[...] = x + gate * _rms(f, gpost_ref[...])


def _ffn(x2, mod6, g_pre, w1, w3, w2, g_post, seq):
    m = x2.shape[0]
    tm = _row_tile(m)
    tiles_per_seq = max(seq // tm, 1)
    assert mod6.shape[0] == 1 or seq % tm == 0
    row = lambda w: pl.BlockSpec((tm, w), lambda i: (i, 0))
    full = _resident
    if mod6.shape[0] == 1:
        mod_spec = pl.BlockSpec((1, 6, D_MODEL), lambda i: (0, 0, 0))
    else:
        mod_spec = pl.BlockSpec((1, 6, D_MODEL), lambda i: (i // tiles_per_seq, 0, 0))
    return pl.pallas_call(
        _ffn_kernel,
        grid=(m // tm,),
        in_specs=[row(D_MODEL), mod_spec, full(g_pre), full(w1), full(w3), full(w2), full(g_post)],
        out_specs=row(D_MODEL),
        out_shape=jax.ShapeDtypeStruct((m, D_MODEL), F32),
        compiler_params=_cparams(("arbitrary",)),
        name="ffn",
    )(x2, mod6, g_pre, w1, w3, w2, g_post)


def _rope_tables(n, rot_dim, lane_off):
    rows = n // GRID_W
    pos_row = jnp.repeat(jnp.arange(rows, dtype=F32), GRID_W)
    pos_col = jnp.tile(jnp.arange(GRID_W, dtype=F32), rows)
    n_freq = rot_dim // 4
    inv_freq = ROPE_BASE ** (-jnp.arange(n_freq, dtype=F32) / n_freq)
    ang = jnp.concatenate([pos_row[:, None] * inv_freq, pos_col[:, None] * inv_freq], axis=-1)
    cos, sin = jnp.cos(ang), jnp.sin(ang)
    zero = jnp.zeros_like(sin)
    c = jnp.concatenate([cos, cos], axis=-1)
    sa = jnp.concatenate([-sin, zero], axis=-1)
    sb = jnp.concatenate([zero, sin], axis=-1)
    if rot_dim == LANES // 2 and lane_off == 0:
        return tuple(jnp.tile(t, (1, 2)) for t in (c, sa, sb))
    pad = lambda t, fill: jnp.concatenate(
        [jnp.full((n, lane_off), fill, F32), t, jnp.full((n, LANES - lane_off - rot_dim), fill, F32)], axis=-1)
    return pad(c, 1.0), pad(sa, 0.0), pad(sb, 0.0)


def _layer_weights(l, w_in, w_gate, w_b_uq, w_b_ukv, sink_c, w_glu, w_branch, w_o, w_ff1, w_ff3, w_ff2):
    wi = w_in[l]
    aq, ak, av = wi[:, 0:512], wi[:, 512:1024], wi[:, 1024:1536]
    bcq, bckv, bkr = wi[:, 1536:1792], wi[:, 1792:1920], wi[:, 1920:1952]
    cq, ck, cv, du = wi[:, 1952:2464], wi[:, 2464:2592], wi[:, 2592:2720], wi[:, 2720:3232]
    cq_perm = cq.reshape(D_MODEL, C_KV_HEADS, C_REP, C_DIM).transpose(0, 2, 1, 3).reshape(D_MODEL, 512)
    zeros = lambda w: jnp.zeros((D_MODEL, w), F32)
    bkr_pad = jnp.concatenate([zeros(B_KR_LANE), bkr, zeros(LANES - B_KR_LANE - B_ROPE)], axis=1)
    w_in_r = jnp.concatenate([aq, ak, av, cq_perm, ck, cv, du, bcq, bckv, bkr_pad], axis=1).astype(BF16)

    uq = w_b_uq[l].reshape(B_QLORA, B_HEADS, B_NOPE + B_ROPE)
    uq = jnp.concatenate([uq, jnp.zeros((B_QLORA, B_HEADS, LANES - B_NOPE - B_ROPE), F32)], axis=-1)
    w_uq_r = uq.reshape(B_QLORA, B_HEADS * LANES).astype(BF16)
    ukv = w_b_ukv[l].reshape(B_KVLORA, B_HEADS, B_NOPE + B_V)
    kn = jnp.concatenate([ukv[..., :B_NOPE], jnp.zeros((B_KVLORA, B_HEADS, LANES - B_NOPE), F32)], axis=-1)
    w_ukv_r = jnp.concatenate([kn.reshape(B_KVLORA, B_HEADS * LANES),
                               ukv[..., B_NOPE:].reshape(B_KVLORA, B_HEADS * B_V)], axis=1).astype(BF16)

    sink_perm = sink_c[l].reshape(C_KV_HEADS, C_REP).T.reshape(C_HEADS)
    sink_rows = jnp.repeat(sink_perm, Q_BLOCK)[:, None]

    wbr = w_branch[l]
    wbr_c = wbr[2].reshape(C_KV_HEADS, C_REP, C_DIM, D_MODEL).transpose(1, 0, 2, 3).reshape(MIX_W, D_MODEL)
    w_branch_r = jnp.stack([wbr[0], wbr[1], wbr_c, wbr[3]], axis=0).astype(BF16)
    return dict(w_in=w_in_r, w_uq=w_uq_r, w_ukv=w_ukv_r, sink_rows=sink_rows,
                w_gate=w_gate[l].astype(BF16), w_glu=w_glu[l].astype(BF16), w_branch=w_branch_r,
                w_o=w_o[l].astype(BF16), w_ff1=w_ff1[l].astype(BF16), w_ff3=w_ff3[l].astype(BF16),
                w_ff2=w_ff2[l].astype(BF16))


def _trunk_layer(l, x, mod6, lw, sp, s5_mats, rope_tabs, ctx):
    latent = ctx is not None
    bsz, n, _ = x.shape
    m = bsz * n
    x2 = x.reshape(m, D_MODEL)
    row1 = lambda v: v.reshape(1, -1)

    outs = _premix(latent, x2, mod6, row1(sp['g_pre_mix']), lw['w_in'], row1(sp['g_b_q']), lw['w_uq'],
                   row1(sp['g_b_kv']), lw['w_ukv'], rope_tabs, n)
    hb, qa, ka, va, qb, kb, vb, qc, kc, vc, du = outs[:11]
    r3 = lambda t: t.reshape(bsz, n, t.shape[-1])
    qa, ka, va, qb, kb, vb, qc, kc, vc, du = map(r3, (qa, ka, va, qb, kb, vb, qc, kc, vc, du))

    lam_init = 0.8 - 0.6 * math.exp(-0.3 * l)
    lam_args = (row1(sp['lam_q1']), row1(sp['lam_k1']), row1(sp['lam_q2']), row1(sp['lam_k2']), row1(sp['g_a_sub']))
    if latent:
        past = ctx['a_k'].shape[1]
        ka_ctx = ctx['a_k'].reshape(bsz, past, 512).astype(BF16)
        va_ctx = ctx['a_v'].reshape(bsz, past, 512).astype(BF16)
        kr_pad = jnp.pad(ctx['b_kr'].reshape(bsz * past, B_ROPE),
                         ((0, 0), (B_KR_LANE, LANES - B_KR_LANE - B_ROPE)))
        kb_ctx, vb_ctx = _kvprep(ctx['b_ckv'].reshape(bsz * past, B_KVLORA), kr_pad, lw['w_ukv'])
        o_a = _attn_a(lam_init, qa, [ka, va, ka_ctx, va_ctx], *lam_args)
        o_b = _attn_b(qb, [kb, vb, kb_ctx.reshape(bsz, past, -1), vb_ctx.reshape(bsz, past, -1)])
        o_c = _attn_c(qc, ctx['c_k'].reshape(bsz, past, LANES).astype(BF16),
                      ctx['c_v'].reshape(bsz, past, LANES).astype(BF16), lw['sink_rows'], kc, vc)
        gn = D_GROUPS * D_STATE
        h0 = [ctx['d_re'][:, 0].reshape(bsz, gn), ctx['d_im'][:, 0].reshape(bsz, gn),
              ctx['d_re'][:, 1].reshape(bsz, gn), ctx['d_im'][:, 1].reshape(bsz, gn)]
    else:
        o_a = _attn_a(lam_init, qa, [ka, va], *lam_args)
        o_b = _attn_b(qb, [kb, vb])
        o_c = _attn_c(qc, kc, vc, lw['sink_rows'])
        h0 = [jnp.zeros((bsz, D_GROUPS * D_STATE), F32)] * 4
    y_s5, finals = _s5_mixer(du, s5_mats, h0)

    f2 = lambda t: t.reshape(m, t.shape[-1])
    x2 = _merge(x2, hb, f2(o_a), f2(o_b), f2(o_c), f2(y_s5), f2(du), mod6, row1(sp['ssm_d']), lw['w_glu'],
                lw['w_gate'], lw['w_branch'], lw['w_o'], row1(sp['g_post_mix']), n)
    x2 = _ffn(x2, mod6, row1(sp['g_pre_ffn']), lw['w_ff1'], lw['w_ff3'], lw['w_ff2'], row1(sp['g_post_ffn']), n)
    x = x2.reshape(bsz, n, D_MODEL)
    if latent:
        return x, None
    akf, avf, ckvf, krf, ckf, cvf = outs[11:]
    st = lambda t: t.reshape(bsz, D_GROUPS, D_STATE)
    new_ctx = {'a_k': akf.reshape(bsz, n, A_HEADS, 2 * A_DIM), 'a_v': avf.reshape(bsz, n, A_HEADS, 2 * A_DIM),
               'b_ckv': ckvf.reshape(bsz, n, B_KVLORA), 'b_kr': krf.reshape(bsz, n, B_ROPE),
               'c_k': ckf.reshape(bsz, n, C_KV_HEADS, C_DIM), 'c_v': cvf.reshape(bsz, n, C_KV_HEADS, C_DIM),
               'd_re': jnp.stack([st(finals[0]), st(finals[2])], axis=1),
               'd_im': jnp.stack([st(finals[1]), st(finals[3])], axis=1)}
    return x, new_ctx


def kernel(x_prompt, x_sample, cache_a_k, cache_a_v, cache_b_ckv, cache_b_kr, cache_c_k, cache_c_v, state_d_re, state_d_im, c, c_ctx, w_mod, b_mod, g_pre_mix, g_post_mix, g_pre_ffn, g_post_ffn, w_in, w_gate, lam_q1, lam_k1, lam_q2, lam_k2, g_a_sub, g_b_q, g_b_kv, w_b_uq, w_b_ukv, sink_c, ssm_a_re, ssm_a_im, ssm_log_dt, ssm_b_re, ssm_b_im, ssm_c_re, ssm_c_im, ssm_d, w_glu, w_branch, w_o, w_ff1, w_ff3, w_ff2):
    dec_b, dec_n, _ = x_sample.shape
    assert dec_b + 1 <= 8

    cond = jnp.concatenate([c_ctx[None, :], c, jnp.zeros((8 - 1 - dec_b, D_MODEL), F32)], axis=0)
    mod = _modulation(cond, w_mod, b_mod)

    small = dict(g_pre_mix=g_pre_mix, g_post_mix=g_post_mix, g_pre_ffn=g_pre_ffn, g_post_ffn=g_post_ffn,
                 lam_q1=lam_q1, lam_k1=lam_k1, lam_q2=lam_q2, lam_k2=lam_k2, g_a_sub=g_a_sub, g_b_q=g_b_q,
                 g_b_kv=g_b_kv, ssm_d=ssm_d)
    rope_tabs = _rope_tables(dec_n, A_DIM, 0) + _rope_tables(dec_n, B_ROPE, B_KR_LANE)

    layers = []
    for l in range(DEPTH):
        lw = _layer_weights(l, w_in, w_gate, w_b_uq, w_b_ukv, sink_c, w_glu, w_branch, w_o, w_ff1, w_ff3, w_ff2)
        sp = {k: v[l] for k, v in small.items()}
        s5_mats = _s5_prepare(ssm_a_re[l], ssm_a_im[l], ssm_log_dt[l], ssm_b_re[l], ssm_b_im[l],
                              ssm_c_re[l], ssm_c_im[l])
        mod_l = mod[l].reshape(8, 6, D_MODEL)
        layers.append((lw, sp, s5_mats, mod_l))

    y_prompt = x_prompt
    ctx_out = []
    for l, (lw, sp, s5_mats, mod_l) in enumerate(layers):
        y_prompt, new_ctx = _trunk_layer(l, y_prompt, mod_l[0:1], lw, sp, s5_mats, None, None)
        ctx_out.append(new_ctx)

    y_sample = x_sample
    for l, (lw, sp, s5_mats, mod_l) in enumerate(layers):
        cached = {'a_k': cache_a_k[:, l], 'a_v': cache_a_v[:, l], 'b_ckv': cache_b_ckv[:, l],
                  'b_kr': cache_b_kr[:, l], 'c_k': cache_c_k[:, l], 'c_v': cache_c_v[:, l],
                  'd_re': state_d_re[:, l], 'd_im': state_d_im[:, l]}
        y_sample, _ = _trunk_layer(l, y_sample, mod_l[1:1 + dec_b], lw, sp, s5_mats, rope_tabs, cached)

    stack = lambda name: jnp.stack([cx[name] for cx in ctx_out], axis=1)
    return (y_prompt, y_sample, stack('a_k'), stack('a_v'), stack('b_ckv'), stack('b_kr'),
            stack('c_k'), stack('c_v'), stack('d_re'), stack('d_im'))
```

```python
import functools
import math

import jax
import jax.numpy as jnp
from jax import lax
from jax.experimental import pallas as pl
from jax.experimental.pallas import tpu as pltpu

F32 = jnp.float32
BF16 = jnp.bfloat16

D_MODEL = 1024
DEPTH = 2
GRID_W = 64
Q_BLOCK = 128
ROPE_BASE = 10000.0
RMS_EPS = 1e-6
NEG_INF = -1e30
LOG2E = math.log2(math.e)
MIX_W = D_MODEL // 2
N_BRANCH = 4
A_HEADS = 4
A_DIM = 64
B_HEADS = 8
B_NOPE = 64
B_ROPE = 32
B_V = 64
B_QLORA = 256
B_KVLORA = 128
C_HEADS = 8
C_KV_HEADS = 2
C_REP = C_HEADS // C_KV_HEADS
C_DIM = 64
D_GROUP = 16
D_GROUPS = MIX_W // D_GROUP
D_STATE = 64
N_DIRS = 2
FF_HIDDEN = ((8 * D_MODEL // 3 + 255) // 256) * 256

LANES = 128
S5_CHUNK = 16
S5_SG = LANES // D_GROUP
S5_FLAT = S5_CHUNK * LANES
S5_SLB = S5_SG * D_STATE // LANES
S5_ROWS = 256
VMEM_LIMIT = 56 * 1024 * 1024
ROW_TILE = 512
Q_TILE = 256
KEY_CHUNK = 512
A_HPS = 2
B_HPS = 4

COL_AQ, COL_AK, COL_AV = 0, 512, 1024
COL_CQ, COL_CK, COL_CV = 1536, 2048, 2176
COL_DU = 2304
COL_BCQ, COL_BCKV, COL_BKR = 2816, 3072, 3200
IN_COLS = 3328
B_KR_LANE = B_NOPE


def _cparams(sem):
    return pltpu.CompilerParams(dimension_semantics=sem, vmem_limit_bytes=VMEM_LIMIT)


def _resident(a):
    return pl.BlockSpec(a.shape, lambda i: (0,) * a.ndim, pipeline_mode=pl.Buffered(1))


def _row_tile(m):
    return min(ROW_TILE, m)


def _dot(a, b):
    return jnp.dot(a, b, preferred_element_type=F32)


def _dot_nt(a, b):
    return lax.dot_general(a, b, (((1,), (1,)), ((), ())), preferred_element_type=F32)


def _rms(x, g):
    return x * lax.rsqrt(jnp.mean(x * x, axis=-1, keepdims=True) + RMS_EPS) * g


def _rope(x, c, sa, sb, half):
    w = x.shape[-1]
    return x * c + pltpu.roll(x, w - half, 1) * sa + pltpu.roll(x, half, 1) * sb


def _mod_kernel(c_ref, w_ref, b_ref, o_ref):
    c = c_ref[...]
    o_ref[0] = _dot(jax.nn.silu(c).astype(BF16), w_ref[0].astype(BF16)) + b_ref[0]


def _modulation(cond, w_mod, b_mod):
    nblk = 6
    return pl.pallas_call(
        _mod_kernel,
        grid=(DEPTH, nblk),
        in_specs=[pl.BlockSpec((8, D_MODEL), lambda l, j: (0, 0)),
                  pl.BlockSpec((1, D_MODEL, D_MODEL), lambda l, j: (l, 0, j)),
                  pl.BlockSpec((1, 1, D_MODEL), lambda l, j: (l, 0, j))],
        out_specs=pl.BlockSpec((1, 8, D_MODEL), lambda l, j: (l, 0, j)),
        out_shape=jax.ShapeDtypeStruct((DEPTH, 8, 6 * D_MODEL), F32),
        compiler_params=_cparams(("arbitrary", "arbitrary")),
        name="modulation",
    )(cond, w_mod, b_mod.reshape(DEPTH, 1, 6 * D_MODEL))


def _premix_kernel(latent, *refs):
    if latent:
        (x_ref, mod_ref, g_ref, win_ref, gbq_ref, wuq_ref, gbkv_ref, wukv_ref,
         ca_ref, saa_ref, sba_ref, cb_ref, sab_ref, sbb_ref,
         hb_ref, qa_ref, ka_ref, va_ref, qb_ref, kb_ref, vb_ref, qc_ref, kc_ref, vc_ref, du_ref) = refs
    else:
        (x_ref, mod_ref, g_ref, win_ref, gbq_ref, wuq_ref, gbkv_ref, wukv_ref,
         hb_ref, qa_ref, ka_ref, va_ref, qb_ref, kb_ref, vb_ref, qc_ref, kc_ref, vc_ref, du_ref,
         akf_ref, avf_ref, ckvf_ref, krf_ref, ckf_ref, cvf_ref) = refs

    x = x_ref[...]
    shift = mod_ref[0, 0:1, :]
    scale = mod_ref[0, 1:2, :]
    h = _rms(x, g_ref[...]) * (1.0 + scale) + shift
    hb = h.astype(BF16)
    hb_ref[...] = hb
    proj = _dot(hb, win_ref[...])

    if latent:
        ca, saa, sba = ca_ref[...], saa_ref[...], sba_ref[...]
        cb, sab, sbb = cb_ref[...], sab_ref[...], sbb_ref[...]
        rope_a = lambda t: _rope(t, ca, saa, sba, A_DIM // 2)
        rope_b = lambda t: _rope(t, cb, sab, sbb, B_ROPE // 2)
    else:
        rope_a = rope_b = lambda t: t

    def blk(col, i):
        return proj[:, col + i * LANES: col + (i + 1) * LANES]

    a_scale = A_DIM ** -0.5 * LOG2E
    for i in range(A_HEADS):
        sl = slice(i * LANES, (i + 1) * LANES)
        qa_ref[:, sl] = (rope_a(blk(COL_AQ, i)) * a_scale).astype(BF16)
        ka_ref[:, sl] = rope_a(blk(COL_AK, i)).astype(BF16)
    va_ref[...] = proj[:, COL_AV:COL_AV + 512].astype(BF16)

    c_scale = C_DIM ** -0.5 * LOG2E
    for i in range(C_REP):
        sl = slice(i * LANES, (i + 1) * LANES)
        qc_ref[:, sl] = (rope_a(blk(COL_CQ, i)) * c_scale).astype(BF16)
    kc_ref[...] = rope_a(blk(COL_CK, 0)).astype(BF16)
    vc_ref[...] = blk(COL_CV, 0).astype(BF16)

    du_ref[...] = proj[:, COL_DU:COL_DU + MIX_W]

    b_scale = (B_NOPE + B_ROPE) ** -0.5 * LOG2E
    cqn = _rms(proj[:, COL_BCQ:COL_BCQ + B_QLORA], gbq_ref[...])
    qb = _dot(cqn.astype(BF16), wuq_ref[...])
    ckv = _rms(proj[:, COL_BCKV:COL_BCKV + B_KVLORA], gbkv_ref[...])
    kvb = _dot(ckv.astype(BF16), wukv_ref[...])
    kr_pad = rope_b(blk(COL_BKR, 0))
    for i in range(B_HEADS):
        sl = slice(i * LANES, (i + 1) * LANES)
        qb_ref[:, sl] = (rope_b(qb[:, sl]) * b_scale).astype(BF16)
        kb_ref[:, sl] = (kvb[:, sl] + kr_pad).astype(BF16)
    vb_ref[...] = kvb[:, B_HEADS * LANES:].astype(BF16)

    if not latent:
        akf_ref[...] = proj[:, COL_AK:COL_AK + 512]
        avf_ref[...] = proj[:, COL_AV:COL_AV + 512]
        ckvf_ref[...] = ckv
        krf_ref[...] = kr_pad[:, B_KR_LANE:B_KR_LANE + B_ROPE]
        ckf_ref[...] = blk(COL_CK, 0)
        cvf_ref[...] = blk(COL_CV, 0)


def _premix(latent, x2, mod6, g_pre, w_in, g_bq, w_uq, g_bkv, w_ukv, rope_tabs, seq):
    m = x2.shape[0]
    tm = _row_tile(m)
    tiles_per_seq = max(seq // tm, 1)
    nb_mod = mod6.shape[0]
    assert nb_mod == 1 or seq % tm == 0
    row = lambda w: pl.BlockSpec((tm, w), lambda i: (i, 0))
    full = _resident
    if nb_mod == 1:
        mod_spec = pl.BlockSpec((1, 6, D_MODEL), lambda i: (0, 0, 0))
    else:
        mod_spec = pl.BlockSpec((1, 6, D_MODEL), lambda i: (i // tiles_per_seq, 0, 0))
    in_specs = [row(D_MODEL), mod_spec, full(g_pre), full(w_in), full(g_bq), full(w_uq), full(g_bkv), full(w_ukv)]
    args = [x2, mod6, g_pre, w_in, g_bq, w_uq, g_bkv, w_ukv]
    if latent:
        tab_spec = pl.BlockSpec((tm, LANES), lambda i: (i % tiles_per_seq, 0))
        in_specs += [tab_spec] * 6
        args += list(rope_tabs)
    widths = [(D_MODEL, BF16), (512, BF16), (512, BF16), (512, BF16), (B_HEADS * LANES, BF16),
              (B_HEADS * LANES, BF16), (B_HEADS * B_V, BF16), (512, BF16), (LANES, BF16), (LANES, BF16),
              (MIX_W, F32)]
    if not latent:
        widths += [(512, F32), (512, F32), (B_KVLORA, F32), (B_ROPE, F32), (LANES, F32), (LANES, F32)]
    out_specs = [row(w) for w, _ in widths]
    out_shape = [jax.ShapeDtypeStruct((m, w), dt) for w, dt in widths]
    return pl.pallas_call(
        functools.partial(_premix_kernel, latent),
        grid=(m // tm,),
        in_specs=in_specs, out_specs=out_specs, out_shape=out_shape,
        compiler_params=_cparams(("arbitrary",)),
        name="premix_latent" if latent else "premix_context",
    )(*args)


def _kvprep_kernel(ckv_ref, kr_ref, wukv_ref, kb_ref, vb_ref):
    kvb = _dot(ckv_ref[...].astype(BF16), wukv_ref[...])
    kr_pad = kr_ref[...]
    for i in range(B_HEADS):
        sl = slice(i * LANES, (i + 1) * LANES)
        kb_ref[:, sl] = (kvb[:, sl] + kr_pad).astype(BF16)
    vb_ref[...] = kvb[:, B_HEADS * LANES:].astype(BF16)


def _kvprep(ckv2, kr_pad2, w_ukv):
    m = ckv2.shape[0]
    tm = _row_tile(m)
    return pl.pallas_call(
        _kvprep_kernel,
        grid=(m // tm,),
        in_specs=[pl.BlockSpec((tm, B_KVLORA), lambda i: (i, 0)),
                  pl.BlockSpec((tm, LANES), lambda i: (i, 0)),
                  _resident(w_ukv)],
        out_specs=[pl.BlockSpec((tm, B_HEADS * LANES), lambda i: (i, 0)),
                   pl.BlockSpec((tm, B_HEADS * B_V), lambda i: (i, 0))],
        out_shape=[jax.ShapeDtypeStruct((m, B_HEADS * LANES), BF16),
                   jax.ShapeDtypeStruct((m, B_HEADS * B_V), BF16)],
        compiler_params=_cparams(("arbitrary",)),
        name="kvprep_b",
    )(ckv2, kr_pad2, w_ukv)


def _scores(mp, ch):
    s = _dot_nt(mp[0], ch[0](mp[1]))
    return s if ch[2] is None else s + ch[2](mp[1])


def _attend(maps, chunks, first=None, tail=None, tail_store=None):
    outs = []
    if first is None:
        cur = [_scores(maps[0], ch) for ch in chunks]
        get = lambda ci: cur[ci]
    else:
        get = first
    for mi, (_, tag, sink) in enumerate(maps):
        m = functools.reduce(jnp.maximum, [jnp.max(get(ci), axis=-1, keepdims=True) for ci in range(len(chunks))])
        l = None
        if sink is not None:
            m = jnp.maximum(m, sink)
            l = jnp.exp2(sink - m)
        nxt, acc = [], None
        for ci, ch in enumerate(chunks):
            if mi + 1 < len(maps):
                nxt.append(_scores(maps[mi + 1], ch))
            elif tail is not None:
                tail_store(ci, _scores(tail, ch))
            p = jnp.exp2(get(ci) - m)
            ls = jnp.sum(p, axis=-1, keepdims=True)
            pv = _dot(p.astype(BF16), ch[1](tag))
            l = ls if l is None else l + ls
            acc = pv if acc is None else acc + pv
        outs.append(acc / l)
        get = lambda ci, vals=nxt: vals[ci]
    return outs


def _attend_tiles(nt, tile_maps, chunks, s_scr, finish):
    def store(ci, s):
        s_scr[ci][...] = s

    first = tile_maps(0)[0]
    for ci, ch in enumerate(chunks):
        store(ci, _scores(first, ch))

    def body(i, carry):
        tail = tile_maps(jnp.minimum(i + 1, nt - 1))[0] if nt > 1 else None
        outs = _attend(tile_maps(i), chunks, first=lambda ci: s_scr[ci][...], tail=tail, tail_store=store)
        finish(i, outs)
        return carry

    lax.fori_loop(0, nt, body, 0)


def _chunk_sizes(kv):
    sizes = []
    for a in kv[::2]:
        t = a.shape[1]
        assert t % KEY_CHUNK == 0 or t < KEY_CHUNK
        sizes += [min(KEY_CHUNK, t)] * (t // min(KEY_CHUNK, t))
    return sizes


def _ref_chunks(kv_refs):
    chunks = []
    for i in range(0, len(kv_refs), 2):
        k_ref, v_ref = kv_refs[i], kv_refs[i + 1]
        t = k_ref.shape[1]
        size = min(KEY_CHUNK, t)
        for c in range(0, t, size):
            chunks.append((lambda tag, r=k_ref, c=c, n=size: r[0, c:c + n, tag[0]],
                           lambda tag, r=v_ref, c=c, n=size: r[0, c:c + n, tag[1]], None))
    return chunks


def _attn_a_kernel(lam_init, nkv, tq, q_ref, *refs):
    kv_refs = refs[:2 * nkv]
    lq1_ref, lk1_ref, lq2_ref, lk2_ref, gsub_ref, o_ref = refs[2 * nkv:2 * nkv + 6]
    s_scr = refs[2 * nkv + 6:]
    lam = (jnp.exp(jnp.sum(lq1_ref[...] * lk1_ref[...], axis=-1, keepdims=True))
           - jnp.exp(jnp.sum(lq2_ref[...] * lk2_ref[...], axis=-1, keepdims=True)) + lam_init)
    lane = lax.broadcasted_iota(jnp.int32, (tq, LANES), 1)
    tile_rows = lambda i: pl.ds(pl.multiple_of(i * tq, tq), tq)

    def tile_maps(i):
        maps = []
        for h in range(A_HPS):
            sl = slice(h * LANES, (h + 1) * LANES)
            q = q_ref[0, tile_rows(i), sl]
            zero = jnp.zeros_like(q)
            maps.append((jnp.where(lane < A_DIM, q, zero), (sl, sl), None))
            maps.append((jnp.where(lane >= A_DIM, q, zero), (sl, sl), None))
        return maps

    def finish(i, outs):
        for h in range(A_HPS):
            o = outs[2 * h] - lam * outs[2 * h + 1]
            o_ref[0, tile_rows(i), h * LANES:(h + 1) * LANES] = (
                _rms(o, gsub_ref[...]) * (1.0 - lam_init)).astype(BF16)

    _attend_tiles(q_ref.shape[1] // tq, tile_maps, _ref_chunks(kv_refs), s_scr, finish)


def _attn_a(lam_init, q, kv, lq1, lk1, lq2, lk2, g_sub):
    bsz, n, _ = q.shape
    tq = min(Q_TILE, n)
    w = A_HPS * LANES
    small = lambda a: pl.BlockSpec(a.shape, lambda b, h: (0, 0))
    return pl.pallas_call(
        functools.partial(_attn_a_kernel, lam_init, len(kv) // 2, tq),
        grid=(bsz, A_HEADS // A_HPS),
        in_specs=[pl.BlockSpec((1, n, w), lambda b, h: (b, 0, h))]
        + [pl.BlockSpec((1, a.shape[1], w), lambda b, h: (b, 0, h)) for a in kv]
        + [small(lq1), small(lk1), small(lq2), small(lk2), small(g_sub)],
        out_specs=pl.BlockSpec((1, n, w), lambda b, h: (b, 0, h)),
        out_shape=jax.ShapeDtypeStruct((bsz, n, A_HEADS * LANES), BF16),
        scratch_shapes=[pltpu.VMEM((tq, size), F32) for size in _chunk_sizes(kv)],
        compiler_params=_cparams(("arbitrary", "arbitrary")),
        name="attn_a",
    )(q, *kv, lq1, lk1, lq2, lk2, g_sub)


def _attn_b_kernel(nkv, tq, q_ref, *refs):
    kv_refs, o_ref, s_scr = refs[:2 * nkv], refs[2 * nkv], refs[2 * nkv + 1:]
    lane = lax.broadcasted_iota(jnp.int32, (tq, LANES), 1)
    tile_rows = lambda i: pl.ds(pl.multiple_of(i * tq, tq), tq)

    def tile_maps(i):
        maps = []
        for h in range(B_HPS):
            sl = slice(h * LANES, (h + 1) * LANES)
            vsl = slice((h // 2) * LANES, (h // 2 + 1) * LANES)
            maps.append((q_ref[0, tile_rows(i), sl], (sl, vsl), None))
        return maps

    def finish(i, outs):
        for pair in range(B_HPS // 2):
            o_ref[0, tile_rows(i), pair * LANES:(pair + 1) * LANES] = (
                jnp.where(lane < B_V, outs[2 * pair], outs[2 * pair + 1]).astype(BF16))

    _attend_tiles(q_ref.shape[1] // tq, tile_maps, _ref_chunks(kv_refs), s_scr, finish)


def _attn_b(q, kv):
    bsz, n, _ = q.shape
    tq = min(Q_TILE, n)
    kv_specs = []
    for i, a in enumerate(kv):
        w = B_HPS * (LANES if i % 2 == 0 else B_V)
        kv_specs.append(pl.BlockSpec((1, a.shape[1], w), lambda b, h: (b, 0, h)))
    return pl.pallas_call(
        functools.partial(_attn_b_kernel, len(kv) // 2, tq),
        grid=(bsz, B_HEADS // B_HPS),
        in_specs=[pl.BlockSpec((1, n, B_HPS * LANES), lambda b, h: (b, 0, h))] + kv_specs,
        out_specs=pl.BlockSpec((1, n, B_HPS * B_V), lambda b, h: (b, 0, h)),
        out_shape=jax.ShapeDtypeStruct((bsz, n, B_HEADS * B_V), BF16),
        scratch_shapes=[pltpu.VMEM((tq, size), F32) for size in _chunk_sizes(kv)],
        compiler_params=_cparams(("arbitrary", "arbitrary")),
        name="attn_b",
    )(q, *kv)


def _attn_c_kernel(latent, nblk, *refs):
    if latent:
        q_ref, kx_ref, vx_ref, sink_ref, kp_ref, kc_ref, kn_ref, vp_ref, vc_ref, vn_ref, bias_ref, o_ref = refs
    else:
        q_ref, kx_ref, vx_ref, sink_ref, o_ref = refs
    q = q_ref[0]
    lane = lax.broadcasted_iota(jnp.int32, (Q_BLOCK, LANES), 1)
    parts = []
    for r in range(C_REP):
        qr = q[:, r * LANES:(r + 1) * LANES]
        for g in range(C_KV_HEADS):
            in_half = (lane >= g * C_DIM) & (lane < (g + 1) * C_DIM)
            parts.append(jnp.where(in_half, qr, jnp.zeros_like(qr)))
    chunks = [(lambda tag: kx_ref[0], lambda tag: vx_ref[0], None)]
    if latent:
        kw = jnp.concatenate([kp_ref[0], kc_ref[0], kn_ref[0]], axis=0)
        vw = jnp.concatenate([vp_ref[0], vc_ref[0], vn_ref[0]], axis=0)
        chunks.append((lambda tag: kw, lambda tag: vw, lambda tag: bias_ref[0]))
    maps = []
    for r in range(C_REP):
        rows = slice(2 * r * Q_BLOCK, (2 * r + 2) * Q_BLOCK)
        maps.append((jnp.concatenate(parts[2 * r:2 * r + 2], axis=0), None, sink_ref[rows, :] * LOG2E))
    outs = _attend(maps, chunks)
    for r, o in enumerate(outs):
        o_ref[0, :, r * LANES:(r + 1) * LANES] = jnp.where(lane < C_DIM, o[:Q_BLOCK], o[Q_BLOCK:]).astype(BF16)


def _attn_c(q, kx, vx, sink_rows, k_lat=None, v_lat=None):
    bsz, n, _ = q.shape
    tc = kx.shape[1]
    nblk = n // Q_BLOCK
    latent = k_lat is not None
    in_specs = [pl.BlockSpec((1, Q_BLOCK, 512), lambda b, j: (b, j, 0)),
                pl.BlockSpec((1, tc, LANES), lambda b, j: (b, 0, 0)),
                pl.BlockSpec((1, tc, LANES), lambda b, j: (b, 0, 0)),
                pl.BlockSpec(sink_rows.shape, lambda b, j: (0, 0))]
    args = [q, kx, vx, sink_rows]
    if latent:
        prev = pl.BlockSpec((1, Q_BLOCK, LANES), lambda b, j: (b, jnp.maximum(j - 1, 0), 0))
        cur = pl.BlockSpec((1, Q_BLOCK, LANES), lambda b, j: (b, j, 0))
        nxt = pl.BlockSpec((1, Q_BLOCK, LANES), lambda b, j: (b, jnp.minimum(j + 1, nblk - 1), 0))
        row = (jnp.arange(C_KV_HEADS * Q_BLOCK) % Q_BLOCK)[:, None]
        col = jnp.arange(3 * Q_BLOCK)[None, :]
        in_prev, in_next = col < Q_BLOCK, col >= 2 * Q_BLOCK
        band = jnp.where(in_prev, col >= row, jnp.where(in_next, col - 2 * Q_BLOCK <= row, True))
        variants = [band & ~in_prev, band, band & ~in_next, band & ~in_prev & ~in_next]
        bias = jnp.stack([jnp.where(ok, 0.0, NEG_INF).astype(F32) for ok in variants], axis=0)
        last = nblk - 1

        def bias_map(b, j):
            first_i = (j == 0).astype(jnp.int32)
            last_i = (j == last).astype(jnp.int32)
            return (1 - first_i + last_i + 2 * first_i * last_i, 0, 0)

        in_specs += [prev, cur, nxt, prev, cur, nxt, pl.BlockSpec((1,) + bias.shape[1:], bias_map)]
        args += [k_lat, k_lat, k_lat, v_lat, v_lat, v_lat, bias]
    return pl.pallas_call(
        functools.partial(_attn_c_kernel, latent, nblk),
        grid=(bsz, nblk),
        in_specs=in_specs,
        out_specs=pl.BlockSpec((1, Q_BLOCK, 512), lambda b, j: (b, j, 0)),
        out_shape=jax.ShapeDtypeStruct((bsz, n, 512), BF16),
        compiler_params=_cparams(("arbitrary", "arbitrary")),
        name="attn_c_latent" if latent else "attn_c_context",
    )(*args)


def _s5_prepare_kernel(are_ref, aim_ref, ldt_ref, btre_ref, btim_ref, cre_ref, cim_ref,
                       arec_ref, aimc_ref, ctre_ref, ctim_ref,
                       td_ref, wd_ref, vd_ref, a16re_ref, a16im_ref):
    p, n = D_GROUP, D_STATE
    gl = pl.program_id(0) % S5_SG

    def one_hot(shape, target):
        r = lax.broadcasted_iota(jnp.int32, shape, 0)
        c = lax.broadcasted_iota(jnp.int32, shape, 1)
        return jnp.where(c == target(r), 1.0, 0.0).astype(BF16)

    place_p = one_hot((p, LANES), lambda r: gl * p + r)
    place_n = one_hot((n, S5_SLB * LANES), lambda r: gl * n + r)
    pbits = p.bit_length() - 1
    place_tp = one_hot((S5_CHUNK * p, S5_FLAT), lambda r: (r >> pbits) * LANES + gl * p + (r & (p - 1)))

    refs = (are_ref, aim_ref, ldt_ref, btre_ref, btim_ref, cre_ref, cim_ref, a16re_ref, a16im_ref)
    kt_fwd, w_fwd = _s5_prepare_direction(0, *refs)
    kt_bwd, w_bwd = _s5_prepare_direction(1, *refs)
    pieces = [kt_bwd[(S5_CHUNK - 1 - i) * p:(S5_CHUNK - i) * p] for i in range(S5_CHUNK - 1)]
    pieces.append(kt_fwd[0:p] + kt_bwd[0:p])
    pieces.append(kt_fwd[p:])
    lagk = _dot(jnp.concatenate(pieces, axis=0).astype(BF16), place_p)
    for t_in in range(S5_CHUNK):
        row = [lagk[(t_out - t_in + S5_CHUNK - 1) * p:(t_out - t_in + S5_CHUNK) * p] for t_out in range(S5_CHUNK)]
        td_ref[0, t_in, 0] = jnp.concatenate(row, axis=1).astype(BF16)

    w_sets = [w_fwd[0][::-1], w_fwd[1][::-1], w_bwd[0], w_bwd[1]]
    for x, rows in enumerate(w_sets):
        wexp = _dot(jnp.concatenate(rows, axis=0).astype(BF16), place_n)
        for t in range(S5_CHUNK):
            wd_ref[0, t, 0, :, x * S5_SLB * LANES:(x + 1) * S5_SLB * LANES] = wexp[t * p:(t + 1) * p].astype(BF16)

    tt = lax.broadcasted_iota(jnp.int32, (n, S5_CHUNK * p), 1) >> pbits
    for d in range(N_DIRS):
        power = (tt + 1 if d == 0 else S5_CHUNK - tt).astype(F32)
        dt = jnp.exp(ldt_ref[d, 0])
        mag = jnp.exp(power * (arec_ref[d, 0] * dt))
        ang = power * (aimc_ref[d, 0] * dt)
        pr, pi = mag * jnp.cos(ang), mag * jnp.sin(ang)
        ctre, ctim = ctre_ref[d, 0], ctim_ref[d, 0]
        vd_ref[0, 2 * d, 0] = _dot((ctre * pr - ctim * pi).astype(BF16), place_tp).astype(BF16)
        vd_ref[0, 2 * d + 1, 0] = _dot((-(ctre * pi + ctim * pr)).astype(BF16), place_tp).astype(BF16)


def _s5_prepare_direction(d, are_ref, aim_ref, ldt_ref, btre_ref, btim_ref, cre_ref, cim_ref, a16re_ref, a16im_ref):
    hi = lax.Precision.HIGHEST
    are = are_ref[d, 0]
    aim = aim_ref[d, 0]
    dt = jnp.exp(ldt_ref[d, 0])
    nj = S5_CHUNK + 1
    jj = lax.broadcasted_iota(jnp.int32, (nj, D_STATE), 0).astype(F32)
    mag = jnp.exp(jj * (are * dt))
    ang = jj * (aim * dt)
    pre = mag * jnp.cos(ang)
    pim = mag * jnp.sin(ang)
    xr = pre[1:2] - 1.0
    xi = pim[1:2]
    den = are * are + aim * aim
    fr = (xr * are + xi * aim) / den
    fi = (xi * are - xr * aim) / den
    btre = btre_ref[d, 0]
    btim = btim_ref[d, 0]
    bbre = fr * btre - fi * btim
    bbim = fr * btim + fi * btre
    cre = cre_ref[d, 0]
    cim = cim_ref[d, 0]
    wj_re, wj_im = [], []
    for j in range(S5_CHUNK):
        pr = pre[j:j + 1]
        pi = pim[j:j + 1]
        wj_re.append(pr * bbre - pi * bbim)
        wj_im.append(pr * bbim + pi * bbre)
    dn = (((1,), (1,)), ((), ()))
    a16re_ref[d, 0] = pre[S5_CHUNK:S5_CHUNK + 1]
    a16im_ref[d, 0] = pim[S5_CHUNK:S5_CHUNK + 1]
    kt = (lax.dot_general(jnp.concatenate(wj_re, axis=0), cre, dn, precision=hi, preferred_element_type=F32)
          - lax.dot_general(jnp.concatenate(wj_im, axis=0), cim, dn, precision=hi, preferred_element_type=F32))
    return kt, (wj_re, wj_im)


def _s5_prepare(a_re, a_im, log_dt, b_re, b_im, c_re, c_im):
    g, n, p = D_GROUPS, D_STATE, D_GROUP
    nsg = g // S5_SG
    v4 = lambda a: a.reshape(N_DIRS, g, 1, a.shape[-1])
    col = lambda a: a.reshape(N_DIRS, g, n, 1)
    spec = lambda *s: pl.BlockSpec((N_DIRS, 1) + s, lambda i: (0, i) + (0,) * len(s))
    grp = lambda *s: pl.BlockSpec((1, s[0], 1) + s[1:], lambda i: (i // S5_SG, 0, i % S5_SG, 0, 0))
    bt_re = jnp.swapaxes(b_re, -1, -2)
    bt_im = jnp.swapaxes(b_im, -1, -2)
    ct_re = jnp.tile(jnp.swapaxes(c_re, -1, -2), (1, 1, 1, S5_CHUNK))
    ct_im = jnp.tile(jnp.swapaxes(c_im, -1, -2), (1, 1, 1, S5_CHUNK))
    outs = pl.pallas_call(
        _s5_prepare_kernel,
        grid=(g,),
        in_specs=[spec(1, n), spec(1, n), spec(1, 1), spec(p, n), spec(p, n), spec(p, n), spec(p, n),
                  spec(n, 1), spec(n, 1), spec(n, S5_CHUNK * p), spec(n, S5_CHUNK * p)],
        out_specs=[grp(S5_CHUNK, p, S5_FLAT), grp(S5_CHUNK, p, 4 * S5_SLB * LANES), grp(4, n, S5_FLAT),
                   spec(1, n), spec(1, n)],
        out_shape=[jax.ShapeDtypeStruct((nsg, S5_CHUNK, S5_SG, p, S5_FLAT), BF16),
                   jax.ShapeDtypeStruct((nsg, S5_CHUNK, S5_SG, p, 4 * S5_SLB * LANES), BF16),
                   jax.ShapeDtypeStruct((nsg, 4, S5_SG, n, S5_FLAT), BF16)]
        + [jax.ShapeDtypeStruct((N_DIRS, g, 1, n), F32)] * 2,
        compiler_params=_cparams(("arbitrary",)),
        name="s5_prepare",
    )(v4(a_re), v4(a_im), log_dt.reshape(N_DIRS, g, 1, 1), bt_re, bt_im, c_re, c_im,
      col(a_re), col(a_im), ct_re, ct_im)
    td, wd, vd, a16re, a16im = outs
    t_sg = td.reshape(nsg, S5_FLAT, S5_FLAT)
    w_sg = wd.reshape(nsg, S5_FLAT, 4 * S5_SLB * LANES)
    v_sg = vd.reshape(nsg, 4 * S5_SG * n, S5_FLAT)
    a16 = [a16re[0].reshape(1, g * n), a16im[0].reshape(1, g * n),
           a16re[1].reshape(1, g * n), a16im[1].reshape(1, g * n)]
    return t_sg, w_sg, v_sg, a16


def _s5_gather_chunks(du_ref, u_scr, bsz, ck):
    for b in range(bsz):
        for t in range(S5_CHUNK):
            u_scr[b * ck:(b + 1) * ck, t * LANES:(t + 1) * LANES] = (
                du_ref[b, pl.ds(t, ck, stride=S5_CHUNK), :].astype(BF16))


def _s5_state_in_kernel(bsz, bpad, ck, du_ref, w_ref, o0_ref, o1_ref, o2_ref, o3_ref, u_scr):
    _s5_gather_chunks(du_ref, u_scr, bsz, ck)
    s = _dot(u_scr[...], w_ref[0])
    for x, o_ref in enumerate((o0_ref, o1_ref, o2_ref, o3_ref)):
        for j in range(S5_SLB):
            col = (x * S5_SLB + j) * LANES
            for b in range(bpad):
                if b < bsz:
                    o_ref[j, pl.ds(b, ck, stride=bpad), :] = s[b * ck:(b + 1) * ck, col:col + LANES]
                else:
                    o_ref[j, pl.ds(b, ck, stride=bpad), :] = jnp.zeros((ck, LANES), F32)


def _s5_state_in(du, w_sg, bpad, ck):
    bsz, n, _ = du.shape
    nchunk = n // S5_CHUNK
    nsg = w_sg.shape[0]
    return pl.pallas_call(
        functools.partial(_s5_state_in_kernel, bsz, bpad, ck),
        grid=(nsg, nchunk // ck),
        in_specs=[pl.BlockSpec((bsz, ck * S5_CHUNK, LANES), lambda s, r: (0, r, s)),
                  pl.BlockSpec((1, S5_FLAT, 4 * S5_SLB * LANES), lambda s, r: (s, 0, 0),
                               pipeline_mode=pl.Buffered(1))],
        out_specs=[pl.BlockSpec((S5_SLB, ck * bpad, LANES), lambda s, r: (s, r, 0))] * 4,
        out_shape=[jax.ShapeDtypeStruct((nsg * S5_SLB, nchunk * bpad, LANES), F32)] * 4,
        scratch_shapes=[pltpu.VMEM((bsz * ck, S5_FLAT), BF16)],
        compiler_params=_cparams(("arbitrary", "arbitrary")),
        name="s5_state_in",
    )(du, w_sg)


def _s5_scan_kernel(nchunk, bpad, s0re_ref, s0im_ref, s1re_ref, s1im_ref, a0re_ref, a0im_ref, a1re_ref, a1im_ref,
                    h0re_ref, h0im_ref, h1re_ref, h1im_ref,
                    p0re_ref, p0im_ref, p1re_ref, p1im_ref, f0re_ref, f0im_ref, f1re_ref, f1im_ref):
    a0re, a0im, a1re, a1im = a0re_ref[...], a0im_ref[...], a1re_ref[...], a1im_ref[...]

    def body(i, carry):
        r0, i0, r1, i1 = carry
        k = pl.ds(pl.multiple_of(i * bpad, bpad), bpad)
        kb = pl.ds(pl.multiple_of((nchunk - 1 - i) * bpad, bpad), bpad)
        p0re_ref[:, k, :] = r0
        p0im_ref[:, k, :] = i0
        p1re_ref[:, kb, :] = r1
        p1im_ref[:, kb, :] = i1
        n_r0 = a0re * r0 - a0im * i0 + s0re_ref[:, k, :]
        n_i0 = a0re * i0 + a0im * r0 + s0im_ref[:, k, :]
        n_r1 = a1re * r1 - a1im * i1 + s1re_ref[:, kb, :]
        n_i1 = a1re * i1 + a1im * r1 + s1im_ref[:, kb, :]
        return n_r0, n_i0, n_r1, n_i1

    r0, i0, r1, i1 = lax.fori_loop(0, nchunk, body,
                                   (h0re_ref[...], h0im_ref[...], h1re_ref[...], h1im_ref[...]))
    f0re_ref[...] = r0
    f0im_ref[...] = i0
    f1re_ref[...] = r1
    f1im_ref[...] = i1


def _s5_scan(s_in, a16, h0, nchunk, bpad):
    nlb = D_GROUPS * D_STATE // LANES
    lb = 2
    seq_spec = pl.BlockSpec((lb, nchunk * bpad, LANES), lambda i: (i, 0, 0))
    a_spec = pl.BlockSpec((lb, 1, LANES), lambda i: (i, 0, 0))
    h_spec = pl.BlockSpec((lb, bpad, LANES), lambda i: (i, 0, 0))
    outs = pl.pallas_call(
        functools.partial(_s5_scan_kernel, nchunk, bpad),
        grid=(nlb // lb,),
        in_specs=[seq_spec] * 4 + [a_spec] * 4 + [h_spec] * 4,
        out_specs=[seq_spec] * 4 + [h_spec] * 4,
        out_shape=[jax.ShapeDtypeStruct((nlb, nchunk * bpad, LANES), F32)] * 4
        + [jax.ShapeDtypeStruct((nlb, bpad, LANES), F32)] * 4,
        compiler_params=_cparams(("arbitrary",)),
        name="s5_scan",
    )(*s_in, *a16, *h0)
    return outs[:4], outs[4:]


def _s5_output_kernel(bsz, bpad, ck, du_ref, t_ref, v_ref, p0_ref, p1_ref, p2_ref, p3_ref, y_ref, u_scr, p_scr):
    _s5_gather_chunks(du_ref, u_scr, bsz, ck)
    for x, p_ref in enumerate((p0_ref, p1_ref, p2_ref, p3_ref)):
        for j in range(S5_SLB):
            col = (x * S5_SLB + j) * LANES
            for b in range(bsz):
                p_scr[b * ck:(b + 1) * ck, col:col + LANES] = (
                    p_ref[j, pl.ds(b, ck, stride=bpad), :].astype(BF16))
    y = _dot(u_scr[...], t_ref[0]) + _dot(p_scr[...], v_ref[0])
    for b in range(bsz):
        for t in range(S5_CHUNK):
            y_ref[b, pl.ds(t, ck, stride=S5_CHUNK), :] = y[b * ck:(b + 1) * ck, t * LANES:(t + 1) * LANES]


def _s5_output(du, t_sg, v_sg, p_states, bpad, ck):
    bsz, n, _ = du.shape
    nchunk = n // S5_CHUNK
    nsg = t_sg.shape[0]
    tok = pl.BlockSpec((bsz, ck * S5_CHUNK, LANES), lambda s, r: (0, r, s))
    mat = pl.BlockSpec((1, S5_FLAT, S5_FLAT), lambda s, r: (s, 0, 0), pipeline_mode=pl.Buffered(1))
    return pl.pallas_call(
        functools.partial(_s5_output_kernel, bsz, bpad, ck),
        grid=(nsg, nchunk // ck),
        in_specs=[tok, mat, mat] + [pl.BlockSpec((S5_SLB, ck * bpad, LANES), lambda s, r: (s, r, 0))] * 4,
        out_specs=tok,
        out_shape=jax.ShapeDtypeStruct((bsz, n, MIX_W), F32),
        scratch_shapes=[pltpu.VMEM((bsz * ck, S5_FLAT), BF16), pltpu.VMEM((bsz * ck, S5_FLAT), BF16)],
        compiler_params=_cparams(("arbitrary", "arbitrary")),
        name="s5_output",
    )(du, t_sg, v_sg, *p_states)


def _s5_mixer(du, mats, h0):
    t_sg, w_sg, v_sg, a16 = mats
    bsz, n, _ = du.shape
    nchunk = n // S5_CHUNK
    bpad = -(-bsz // 8) * 8
    ck = min(nchunk, max(S5_ROWS // bsz, 1))
    assert nchunk % ck == 0 and (ck * S5_CHUNK) % 8 == 0
    to_blocks = lambda h: h.reshape(h.shape[0], -1, LANES).transpose(1, 0, 2)
    h0 = [to_blocks(jnp.pad(h, ((0, bpad - bsz), (0, 0)))) for h in h0]
    a16 = [to_blocks(a) for a in a16]
    s_in = _s5_state_in(du, w_sg, bpad, ck)
    p_states, finals = _s5_scan(s_in, a16, h0, nchunk, bpad)
    y = _s5_output(du, t_sg, v_sg, p_states, bpad, ck)
    return y, [f.transpose(1, 0, 2).reshape(bpad, -1)[:bsz] for f in finals]


def _merge_kernel(x_ref, hb_ref, oa_ref, ob_ref, oc_ref, y_ref, u_ref, mod_ref, dskip_ref, wglu_ref,
                  wgate_ref, wbr_ref, wo_ref, gpost_ref, xo_ref):
    yv = y_ref[...] + dskip_ref[...] * u_ref[...]
    yg = jax.nn.gelu(yv)
    od = yg * jax.nn.sigmoid(_dot(yg.astype(BF16), wglu_ref[...]))
    hb = hb_ref[...]
    branches = (oa_ref[...], ob_ref[...], oc_ref[...], od.astype(BF16))
    merged = None
    for k, o in enumerate(branches):
        gate = jax.nn.sigmoid(_dot(hb, wgate_ref[:, k * D_MODEL:(k + 1) * D_MODEL]))
        term = gate * _dot(o, wbr_ref[k])
        merged = term if merged is None else merged + term
    z = _dot(merged.astype(BF16), wo_ref[...])
    gate_m = mod_ref[0, 2:3, :]
    xo_ref[...] = x_ref[...] + gate_m * _rms(z, gpost_ref[...])


def _merge(x2, hb, oa, ob, oc, y, du, mod6, d_skip, w_glu, w_gate, w_branch, w_o, g_post, seq):
    m = x2.shape[0]
    tm = _row_tile(m)
    tiles_per_seq = max(seq // tm, 1)
    assert mod6.shape[0] == 1 or seq % tm == 0
    row = lambda w: pl.BlockSpec((tm, w), lambda i: (i, 0))
    full = _resident
    if mod6.shape[0] == 1:
        mod_spec = pl.BlockSpec((1, 6, D_MODEL), lambda i: (0, 0, 0))
    else:
        mod_spec = pl.BlockSpec((1, 6, D_MODEL), lambda i: (i // tiles_per_seq, 0, 0))
    return pl.pallas_call(
        _merge_kernel,
        grid=(m // tm,),
        in_specs=[row(D_MODEL), row(D_MODEL), row(MIX_W), row(MIX_W), row(MIX_W), row(MIX_W), row(MIX_W),
                  mod_spec, full(d_skip), full(w_glu), full(w_gate), full(w_branch), full(w_o), full(g_post)],
        out_specs=row(D_MODEL),
        out_shape=jax.ShapeDtypeStruct((m, D_MODEL), F32),
        compiler_params=_cparams(("arbitrary",)),
        name="merge",
    )(x2, hb, oa, ob, oc, y, du, mod6, d_skip, w_glu, w_gate, w_branch, w_o, g_post)


def _ffn_kernel(x_ref, mod_ref, gpre_ref, w1_ref, w3_ref, w2_ref, gpost_ref, xo_ref):
    x = x_ref[...]
    shift = mod_ref[0, 3:4, :]
    scale = mod_ref[0, 4:5, :]
    gate = mod_ref[0, 5:6, :]
    h2 = (_rms(x, gpre_ref[...]) * (1.0 + scale) + shift).astype(BF16)
    a = _dot(h2, w1_ref[...])
    b = _dot(h2, w3_ref[...])
    f = _dot((jax.nn.silu(a) * b).astype(BF16), w2_ref[...])
    xo_ref[...] = x + gate * _rms(f, gpost_ref[...])


def _ffn(x2, mod6, g_pre, w1, w3, w2, g_post, seq):
    m = x2.shape[0]
    tm = _row_tile(m)
    tiles_per_seq = max(seq // tm, 1)
    assert mod6.shape[0] == 1 or seq % tm == 0
    row = lambda w: pl.BlockSpec((tm, w), lambda i: (i, 0))
    full = _resident
    if mod6.shape[0] == 1:
        mod_spec = pl.BlockSpec((1, 6, D_MODEL), lambda i: (0, 0, 0))
    else:
        mod_spec = pl.BlockSpec((1, 6, D_MODEL), lambda i: (i // tiles_per_seq, 0, 0))
    return pl.pallas_call(
        _ffn_kernel,
        grid=(m // tm,),
        in_specs=[row(D_MODEL), mod_spec, full(g_pre), full(w1), full(w3), full(w2), full(g_post)],
        out_specs=row(D_MODEL),
        out_shape=jax.ShapeDtypeStruct((m, D_MODEL), F32),
        compiler_params=_cparams(("arbitrary",)),
        name="ffn",
    )(x2, mod6, g_pre, w1, w3, w2, g_post)


def _rope_tables(n, rot_dim, lane_off):
    rows = n // GRID_W
    pos_row = jnp.repeat(jnp.arange(rows, dtype=F32), GRID_W)
    pos_col = jnp.tile(jnp.arange(GRID_W, dtype=F32), rows)
    n_freq = rot_dim // 4
    inv_freq = ROPE_BASE ** (-jnp.arange(n_freq, dtype=F32) / n_freq)
    ang = jnp.concatenate([pos_row[:, None] * inv_freq, pos_col[:, None] * inv_freq], axis=-1)
    cos, sin = jnp.cos(ang), jnp.sin(ang)
    zero = jnp.zeros_like(sin)
    c = jnp.concatenate([cos, cos], axis=-1)
    sa = jnp.concatenate([-sin, zero], axis=-1)
    sb = jnp.concatenate([zero, sin], axis=-1)
    if rot_dim == LANES // 2 and lane_off == 0:
        return tuple(jnp.tile(t, (1, 2)) for t in (c, sa, sb))
    pad = lambda t, fill: jnp.concatenate(
        [jnp.full((n, lane_off), fill, F32), t, jnp.full((n, LANES - lane_off - rot_dim), fill, F32)], axis=-1)
    return pad(c, 1.0), pad(sa, 0.0), pad(sb, 0.0)


def _layer_weights(l, w_in, w_gate, w_b_uq, w_b_ukv, sink_c, w_glu, w_branch, w_o, w_ff1, w_ff3, w_ff2):
    wi = w_in[l]
    aq, ak, av = wi[:, 0:512], wi[:, 512:1024], wi[:, 1024:1536]
    bcq, bckv, bkr = wi[:, 1536:1792], wi[:, 1792:1920], wi[:, 1920:1952]
    cq, ck, cv, du = wi[:, 1952:2464], wi[:, 2464:2592], wi[:, 2592:2720], wi[:, 2720:3232]
    cq_perm = cq.reshape(D_MODEL, C_KV_HEADS, C_REP, C_DIM).transpose(0, 2, 1, 3).reshape(D_MODEL, 512)
    zeros = lambda w: jnp.zeros((D_MODEL, w), F32)
    bkr_pad = jnp.concatenate([zeros(B_KR_LANE), bkr, zeros(LANES - B_KR_LANE - B_ROPE)], axis=1)
    w_in_r = jnp.concatenate([aq, ak, av, cq_perm, ck, cv, du, bcq, bckv, bkr_pad], axis=1).astype(BF16)

    uq = w_b_uq[l].reshape(B_QLORA, B_HEADS, B_NOPE + B_ROPE)
    uq = jnp.concatenate([uq, jnp.zeros((B_QLORA, B_HEADS, LANES - B_NOPE - B_ROPE), F32)], axis=-1)
    w_uq_r = uq.reshape(B_QLORA, B_HEADS * LANES).astype(BF16)
    ukv = w_b_ukv[l].reshape(B_KVLORA, B_HEADS, B_NOPE + B_V)
    kn = jnp.concatenate([ukv[..., :B_NOPE], jnp.zeros((B_KVLORA, B_HEADS, LANES - B_NOPE), F32)], axis=-1)
    w_ukv_r = jnp.concatenate([kn.reshape(B_KVLORA, B_HEADS * LANES),
                               ukv[..., B_NOPE:].reshape(B_KVLORA, B_HEADS * B_V)], axis=1).astype(BF16)

    sink_perm = sink_c[l].reshape(C_KV_HEADS, C_REP).T.reshape(C_HEADS)
    sink_rows = jnp.repeat(sink_perm, Q_BLOCK)[:, None]

    wbr = w_branch[l]
    wbr_c = wbr[2].reshape(C_KV_HEADS, C_REP, C_DIM, D_MODEL).transpose(1, 0, 2, 3).reshape(MIX_W, D_MODEL)
    w_branch_r = jnp.stack([wbr[0], wbr[1], wbr_c, wbr[3]], axis=0).astype(BF16)
    return dict(w_in=w_in_r, w_uq=w_uq_r, w_ukv=w_ukv_r, sink_rows=sink_rows,
                w_gate=w_gate[l].astype(BF16), w_glu=w_glu[l].astype(BF16), w_branch=w_branch_r,
                w_o=w_o[l].astype(BF16), w_ff1=w_ff1[l].astype(BF16), w_ff3=w_ff3[l].astype(BF16),
                w_ff2=w_ff2[l].astype(BF16))


def _trunk_layer(l, x, mod6, lw, sp, s5_mats, rope_tabs, ctx):
    latent = ctx is not None
    bsz, n, _ = x.shape
    m = bsz * n
    x2 = x.reshape(m, D_MODEL)
    row1 = lambda v: v.reshape(1, -1)

    outs = _premix(latent, x2, mod6, row1(sp['g_pre_mix']), lw['w_in'], row1(sp['g_b_q']), lw['w_uq'],
                   row1(sp['g_b_kv']), lw['w_ukv'], rope_tabs, n)
    hb, qa, ka, va, qb, kb, vb, qc, kc, vc, du = outs[:11]
    r3 = lambda t: t.reshape(bsz, n, t.shape[-1])
    qa, ka, va, qb, kb, vb, qc, kc, vc, du = map(r3, (qa, ka, va, qb, kb, vb, qc, kc, vc, du))

    lam_init = 0.8 - 0.6 * math.exp(-0.3 * l)
    lam_args = (row1(sp['lam_q1']), row1(sp['lam_k1']), row1(sp['lam_q2']), row1(sp['lam_k2']), row1(sp['g_a_sub']))
    if latent:
        past = ctx['a_k'].shape[1]
        ka_ctx = ctx['a_k'].reshape(bsz, past, 512).astype(BF16)
        va_ctx = ctx['a_v'].reshape(bsz, past, 512).astype(BF16)
        kr_pad = jnp.pad(ctx['b_kr'].reshape(bsz * past, B_ROPE),
                         ((0, 0), (B_KR_LANE, LANES - B_KR_LANE - B_ROPE)))
        kb_ctx, vb_ctx = _kvprep(ctx['b_ckv'].reshape(bsz * past, B_KVLORA), kr_pad, lw['w_ukv'])
        o_a = _attn_a(lam_init, qa, [ka, va, ka_ctx, va_ctx], *lam_args)
        o_b = _attn_b(qb, [kb, vb, kb_ctx.reshape(bsz, past, -1), vb_ctx.reshape(bsz, past, -1)])
        o_c = _attn_c(qc, ctx['c_k'].reshape(bsz, past, LANES).astype(BF16),
                      ctx['c_v'].reshape(bsz, past, LANES).astype(BF16), lw['sink_rows'], kc, vc)
        gn = D_GROUPS * D_STATE
        h0 = [ctx['d_re'][:, 0].reshape(bsz, gn), ctx['d_im'][:, 0].reshape(bsz, gn),
              ctx['d_re'][:, 1].reshape(bsz, gn), ctx['d_im'][:, 1].reshape(bsz, gn)]
    else:
        o_a = _attn_a(lam_init, qa, [ka, va], *lam_args)
        o_b = _attn_b(qb, [kb, vb])
        o_c = _attn_c(qc, kc, vc, lw['sink_rows'])
        h0 = [jnp.zeros((bsz, D_GROUPS * D_STATE), F32)] * 4
    y_s5, finals = _s5_mixer(du, s5_mats, h0)

    f2 = lambda t: t.reshape(m, t.shape[-1])
    x2 = _merge(x2, hb, f2(o_a), f2(o_b), f2(o_c), f2(y_s5), f2(du), mod6, row1(sp['ssm_d']), lw['w_glu'],
                lw['w_gate'], lw['w_branch'], lw['w_o'], row1(sp['g_post_mix']), n)
    x2 = _ffn(x2, mod6, row1(sp['g_pre_ffn']), lw['w_ff1'], lw['w_ff3'], lw['w_ff2'], row1(sp['g_post_ffn']), n)
    x = x2.reshape(bsz, n, D_MODEL)
    if latent:
        return x, None
    akf, avf, ckvf, krf, ckf, cvf = outs[11:]
    st = lambda t: t.reshape(bsz, D_GROUPS, D_STATE)
    new_ctx = {'a_k': akf.reshape(bsz, n, A_HEADS, 2 * A_DIM), 'a_v': avf.reshape(bsz, n, A_HEADS, 2 * A_DIM),
               'b_ckv': ckvf.reshape(bsz, n, B_KVLORA), 'b_kr': krf.reshape(bsz, n, B_ROPE),
               'c_k': ckf.reshape(bsz, n, C_KV_HEADS, C_DIM), 'c_v': cvf.reshape(bsz, n, C_KV_HEADS, C_DIM),
               'd_re': jnp.stack([st(finals[0]), st(finals[2])], axis=1),
               'd_im': jnp.stack([st(finals[1]), st(finals[3])], axis=1)}
    return x, new_ctx


def kernel(x_prompt, x_sample, cache_a_k, cache_a_v, cache_b_ckv, cache_b_kr, cache_c_k, cache_c_v, state_d_re, state_d_im, c, c_ctx, w_mod, b_mod, g_pre_mix, g_post_mix, g_pre_ffn, g_post_ffn, w_in, w_gate, lam_q1, lam_k1, lam_q2, lam_k2, g_a_sub, g_b_q, g_b_kv, w_b_uq, w_b_ukv, sink_c, ssm_a_re, ssm_a_im, ssm_log_dt, ssm_b_re, ssm_b_im, ssm_c_re, ssm_c_im, ssm_d, w_glu, w_branch, w_o, w_ff1, w_ff3, w_ff2):
    dec_b, dec_n, _ = x_sample.shape
    assert dec_b + 1 <= 8

    cond = jnp.concatenate([c_ctx[None, :], c, jnp.zeros((8 - 1 - dec_b, D_MODEL), F32)], axis=0)
    mod = _modulation(cond, w_mod, b_mod)

    small = dict(g_pre_mix=g_pre_mix, g_post_mix=g_post_mix, g_pre_ffn=g_pre_ffn, g_post_ffn=g_post_ffn,
                 lam_q1=lam_q1, lam_k1=lam_k1, lam_q2=lam_q2, lam_k2=lam_k2, g_a_sub=g_a_sub, g_b_q=g_b_q,
                 g_b_kv=g_b_kv, ssm_d=ssm_d)
    rope_tabs = _rope_tables(dec_n, A_DIM, 0) + _rope_tables(dec_n, B_ROPE, B_KR_LANE)

    layers = []
    for l in range(DEPTH):
        lw = _layer_weights(l, w_in, w_gate, w_b_uq, w_b_ukv, sink_c, w_glu, w_branch, w_o, w_ff1, w_ff3, w_ff2)
        sp = {k: v[l] for k, v in small.items()}
        s5_mats = _s5_prepare(ssm_a_re[l], ssm_a_im[l], ssm_log_dt[l], ssm_b_re[l], ssm_b_im[l],
                              ssm_c_re[l], ssm_c_im[l])
        mod_l = mod[l].reshape(8, 6, D_MODEL)
        layers.append((lw, sp, s5_mats, mod_l))

    y_prompt = x_prompt
    ctx_out = []
    for l, (lw, sp, s5_mats, mod_l) in enumerate(layers):
        y_prompt, new_ctx = _trunk_layer(l, y_prompt, mod_l[0:1], lw, sp, s5_mats, None, None)
        ctx_out.append(new_ctx)

    y_sample = x_sample
    for l, (lw, sp, s5_mats, mod_l) in enumerate(layers):
        cached = {'a_k': cache_a_k[:, l], 'a_v': cache_a_v[:, l], 'b_ckv': cache_b_ckv[:, l],
                  'b_kr': cache_b_kr[:, l], 'c_k': cache_c_k[:, l], 'c_v': cache_c_v[:, l],
                  'd_re': state_d_re[:, l], 'd_im': state_d_im[:, l]}
        y_sample, _ = _trunk_layer(l, y_sample, mod_l[1:1 + dec_b], lw, sp, s5_mats, rope_tabs, cached)

    stack = lambda name: jnp.stack([cx[name] for cx in ctx_out], axis=1)
    return (y_prompt, y_sample, stack('a_k'), stack('a_v'), stack('b_ckv'), stack('b_kr'),
            stack('c_k'), stack('c_v'), stack('d_re'), stack('d_im'))
```

```python
import functools
import math

import jax
import jax.numpy as jnp
from jax import lax
from jax.experimental import pallas as pl
from jax.experimental.pallas import tpu as pltpu

F32 = jnp.float32
BF16 = jnp.bfloat16

D_MODEL = 1024
DEPTH = 2
GRID_W = 64
Q_BLOCK = 128
ROPE_BASE = 10000.0
RMS_EPS = 1e-6
NEG_INF = -1e30
LOG2E = math.log2(math.e)
MIX_W = D_MODEL // 2
N_BRANCH = 4
A_HEADS = 4
A_DIM = 64
B_HEADS = 8
B_NOPE = 64
B_ROPE = 32
B_V = 64
B_QLORA = 256
B_KVLORA = 128
C_HEADS = 8
C_KV_HEADS = 2
C_REP = C_HEADS // C_KV_HEADS
C_DIM = 64
D_GROUP = 16
D_GROUPS = MIX_W // D_GROUP
D_STATE = 64
N_DIRS = 2
FF_HIDDEN = ((8 * D_MODEL // 3 + 255) // 256) * 256

LANES = 128
S5_CHUNK = 16
S5_SG = LANES // D_GROUP
S5_FLAT = S5_CHUNK * LANES
S5_SLB = S5_SG * D_STATE // LANES
S5_ROWS = 256
VMEM_LIMIT = 56 * 1024 * 1024
ROW_TILE = 512
Q_TILE = 256
KEY_CHUNK = 512
A_HPS = 2
B_HPS = 4

COL_AQ, COL_AK, COL_AV = 0, 512, 1024
COL_CQ, COL_CK, COL_CV = 1536, 2048, 2176
COL_DU = 2304
COL_BCQ, COL_BCKV, COL_BKR = 2816, 3072, 3200
IN_COLS = 3328
B_KR_LANE = B_NOPE


def _cparams(sem):
    return pltpu.CompilerParams(dimension_semantics=sem, vmem_limit_bytes=VMEM_LIMIT)


def _resident(a):
    return pl.BlockSpec(a.shape, lambda i: (0,) * a.ndim, pipeline_mode=pl.Buffered(1))


def _row_tile(m):
    return min(ROW_TILE, m)


def _dot(a, b):
    return jnp.dot(a, b, preferred_element_type=F32)


def _dot_nt(a, b):
    return lax.dot_general(a, b, (((1,), (1,)), ((), ())), preferred_element_type=F32)


def _rms(x, g):
    return x * lax.rsqrt(jnp.mean(x * x, axis=-1, keepdims=True) + RMS_EPS) * g


def _rope(x, c, sa, sb, half):
    w = x.shape[-1]
    return x * c + pltpu.roll(x, w - half, 1) * sa + pltpu.roll(x, half, 1) * sb


def _mod_kernel(c_ref, w_ref, b_ref, o_ref):
    c = c_ref[...]
    o_ref[0] = _dot(jax.nn.silu(c).astype(BF16), w_ref[0].astype(BF16)) + b_ref[0]


def _modulation(cond, w_mod, b_mod):
    nblk = 6
    return pl.pallas_call(
        _mod_kernel,
        grid=(DEPTH, nblk),
        in_specs=[pl.BlockSpec((8, D_MODEL), lambda l, j: (0, 0)),
                  pl.BlockSpec((1, D_MODEL, D_MODEL), lambda l, j: (l, 0, j)),
                  pl.BlockSpec((1, 1, D_MODEL), lambda l, j: (l, 0, j))],
        out_specs=pl.BlockSpec((1, 8, D_MODEL), lambda l, j: (l, 0, j)),
        out_shape=jax.ShapeDtypeStruct((DEPTH, 8, 6 * D_MODEL), F32),
        compiler_params=_cparams(("arbitrary", "arbitrary")),
        name="modulation",
    )(cond, w_mod, b_mod.reshape(DEPTH, 1, 6 * D_MODEL))


def _premix_kernel(latent, *refs):
    if latent:
        (x_ref, mod_ref, g_ref, win_ref, gbq_ref, wuq_ref, gbkv_ref, wukv_ref,
         ca_ref, saa_ref, sba_ref, cb_ref, sab_ref, sbb_ref,
         hb_ref, qa_ref, ka_ref, va_ref, qb_ref, kb_ref, vb_ref, qc_ref, kc_ref, vc_ref, du_ref) = refs
    else:
        (x_ref, mod_ref, g_ref, win_ref, gbq_ref, wuq_ref, gbkv_ref, wukv_ref,
         hb_ref, qa_ref, ka_ref, va_ref, qb_ref, kb_ref, vb_ref, qc_ref, kc_ref, vc_ref, du_ref,
         akf_ref, avf_ref, ckvf_ref, krf_ref, ckf_ref, cvf_ref) = refs

    x = x_ref[...]
    shift = mod_ref[0, 0:1, :]
    scale = mod_ref[0, 1:2, :]
    h = _rms(x, g_ref[...]) * (1.0 + scale) + shift
    hb = h.astype(BF16)
    hb_ref[...] = hb
    proj = _dot(hb, win_ref[...])

    if latent:
        ca, saa, sba = ca_ref[...], saa_ref[...], sba_ref[...]
        cb, sab, sbb = cb_ref[...], sab_ref[...], sbb_ref[...]
        rope_a = lambda t: _rope(t, ca, saa, sba, A_DIM // 2)
        rope_b = lambda t: _rope(t, cb, sab, sbb, B_ROPE // 2)
    else:
        rope_a = rope_b = lambda t: t

    def blk(col, i):
        return proj[:, col + i * LANES: col + (i + 1) * LANES]

    a_scale = A_DIM ** -0.5 * LOG2E
    for i in range(A_HEADS):
        sl = slice(i * LANES, (i + 1) * LANES)
        qa_ref[:, sl] = (rope_a(blk(COL_AQ, i)) * a_scale).astype(BF16)
        ka_ref[:, sl] = rope_a(blk(COL_AK, i)).astype(BF16)
    va_ref[...] = proj[:, COL_AV:COL_AV + 512].astype(BF16)

    c_scale = C_DIM ** -0.5 * LOG2E
    for i in range(C_REP):
        sl = slice(i * LANES, (i + 1) * LANES)
        qc_ref[:, sl] = (rope_a(blk(COL_CQ, i)) * c_scale).astype(BF16)
    kc_ref[...] = rope_a(blk(COL_CK, 0)).astype(BF16)
    vc_ref[...] = blk(COL_CV, 0).astype(BF16)

    du_ref[...] = proj[:, COL_DU:COL_DU + MIX_W]

    b_scale = (B_NOPE + B_ROPE) ** -0.5 * LOG2E
    cqn = _rms(proj[:, COL_BCQ:COL_BCQ + B_QLORA], gbq_ref[...])
    qb = _dot(cqn.astype(BF16), wuq_ref[...])
    ckv = _rms(proj[:, COL_BCKV:COL_BCKV + B_KVLORA], gbkv_ref[...])
    kvb = _dot(ckv.astype(BF16), wukv_ref[...])
    kr_pad = rope_b(blk(COL_BKR, 0))
    for i in range(B_HEADS):
        sl = slice(i * LANES, (i + 1) * LANES)
        qb_ref[:, sl] = (rope_b(qb[:, sl]) * b_scale).astype(BF16)
        kb_ref[:, sl] = (kvb[:, sl] + kr_pad).astype(BF16)
    vb_ref[...] = kvb[:, B_HEADS * LANES:].astype(BF16)

    if not latent:
        akf_ref[...] = proj[:, COL_AK:COL_AK + 512]
        avf_ref[...] = proj[:, COL_AV:COL_AV + 512]
        ckvf_ref[...] = ckv
        krf_ref[...] = kr_pad[:, B_KR_LANE:B_KR_LANE + B_ROPE]
        ckf_ref[...] = blk(COL_CK, 0)
        cvf_ref[...] = blk(COL_CV, 0)


def _premix(latent, x2, mod6, g_pre, w_in, g_bq, w_uq, g_bkv, w_ukv, rope_tabs, seq):
    m = x2.shape[0]
    tm = _row_tile(m)
    tiles_per_seq = max(seq // tm, 1)
    nb_mod = mod6.shape[0]
    assert nb_mod == 1 or seq % tm == 0
    row = lambda w: pl.BlockSpec((tm, w), lambda i: (i, 0))
    full = _resident
    if nb_mod == 1:
        mod_spec = pl.BlockSpec((1, 6, D_MODEL), lambda i: (0, 0, 0))
    else:
        mod_spec = pl.BlockSpec((1, 6, D_MODEL), lambda i: (i // tiles_per_seq, 0, 0))
    in_specs = [row(D_MODEL), mod_spec, full(g_pre), full(w_in), full(g_bq), full(w_uq), full(g_bkv), full(w_ukv)]
    args = [x2, mod6, g_pre, w_in, g_bq, w_uq, g_bkv, w_ukv]
    if latent:
        tab_spec = pl.BlockSpec((tm, LANES), lambda i: (i % tiles_per_seq, 0))
        in_specs += [tab_spec] * 6
        args += list(rope_tabs)
    widths = [(D_MODEL, BF16), (512, BF16), (512, BF16), (512, BF16), (B_HEADS * LANES, BF16),
              (B_HEADS * LANES, BF16), (B_HEADS * B_V, BF16), (512, BF16), (LANES, BF16), (LANES, BF16),
              (MIX_W, F32)]
    if not latent:
        widths += [(512, F32), (512, F32), (B_KVLORA, F32), (B_ROPE, F32), (LANES, F32), (LANES, F32)]
    out_specs = [row(w) for w, _ in widths]
    out_shape = [jax.ShapeDtypeStruct((m, w), dt) for w, dt in widths]
    return pl.pallas_call(
        functools.partial(_premix_kernel, latent),
        grid=(m // tm,),
        in_specs=in_specs, out_specs=out_specs, out_shape=out_shape,
        compiler_params=_cparams(("arbitrary",)),
        name="premix_latent" if latent else "premix_context",
    )(*args)


def _kvprep_kernel(ckv_ref, kr_ref, wukv_ref, kb_ref, vb_ref):
    kvb = _dot(ckv_ref[...].astype(BF16), wukv_ref[...])
    kr_pad = kr_ref[...]
    for i in range(B_HEADS):
        sl = slice(i * LANES, (i + 1) * LANES)
        kb_ref[:, sl] = (kvb[:, sl] + kr_pad).astype(BF16)
    vb_ref[...] = kvb[:, B_HEADS * LANES:].astype(BF16)


def _kvprep(ckv2, kr_pad2, w_ukv):
    m = ckv2.shape[0]
    tm = _row_tile(m)
    return pl.pallas_call(
        _kvprep_kernel,
        grid=(m // tm,),
        in_specs=[pl.BlockSpec((tm, B_KVLORA), lambda i: (i, 0)),
                  pl.BlockSpec((tm, LANES), lambda i: (i, 0)),
                  _resident(w_ukv)],
        out_specs=[pl.BlockSpec((tm, B_HEADS * LANES), lambda i: (i, 0)),
                   pl.BlockSpec((tm, B_HEADS * B_V), lambda i: (i, 0))],
        out_shape=[jax.ShapeDtypeStruct((m, B_HEADS * LANES), BF16),
                   jax.ShapeDtypeStruct((m, B_HEADS * B_V), BF16)],
        compiler_params=_cparams(("arbitrary",)),
        name="kvprep_b",
    )(ckv2, kr_pad2, w_ukv)


def _scores(mp, ch):
    s = _dot_nt(mp[0], ch[0](mp[1]))
    return s if ch[2] is None else s + ch[2](mp[1])


def _lane_fold(x, op):
    parts = [x[:, c:c + LANES] for c in range(0, x.shape[-1], LANES)]
    return functools.reduce(op, parts)


def _attend_tiles(nt, tile_maps, chunks, s_scr, finish):
    nc, lead = len(chunks), len(s_scr)
    item_scores = lambda maps, u: _scores(maps[u // nc], chunks[u % nc])

    first = tile_maps(0)
    for u in range(lead):
        s_scr[u][...] = item_scores(first, u)

    def body(i, carry):
        maps = tile_maps(i)
        tail = tile_maps(jnp.minimum(i + 1, nt - 1)) if nt > 1 else None
        total = len(maps) * nc
        assert lead <= total
        vals = {}
        get = lambda u: s_scr[u][...] if u < lead else vals[u]
        outs = []
        for mi, (_, tag, sink) in enumerate(maps):
            m = jnp.max(functools.reduce(jnp.maximum, [_lane_fold(get(mi * nc + ci), jnp.maximum) for ci in range(nc)]),
                        axis=-1, keepdims=True)
            if sink is not None:
                m = jnp.maximum(m, sink)
            l, acc = None, None
            for ci, ch in enumerate(chunks):
                t = mi * nc + ci
                u = t + lead
                if u < total:
                    vals[u] = item_scores(maps, u)
                elif tail is not None:
                    s_scr[u - total][...] = item_scores(tail, u - total)
                p = jnp.exp2(get(t) - m)
                ls = _lane_fold(p, jnp.add)
                pv = _dot(p.astype(BF16), ch[1](tag))
                l = ls if l is None else l + ls
                acc = pv if acc is None else acc + pv
            denom = jnp.sum(l, axis=-1, keepdims=True)
            if sink is not None:
                denom = denom + jnp.exp2(sink - m)
            outs.append(acc / denom)
        finish(i, outs)
        return carry

    lax.fori_loop(0, nt, body, 0)


def _score_buffers(kv, tq):
    sizes = []
    for a in kv[::2]:
        t = a.shape[1]
        assert t % KEY_CHUNK == 0 or t < KEY_CHUNK
        sizes += [min(KEY_CHUNK, t)] * (t // min(KEY_CHUNK, t))
    lead = len(sizes)
    return [pltpu.VMEM((tq, sizes[u % len(sizes)]), F32) for u in range(lead)]


def _ref_chunks(kv_refs):
    chunks = []
    for i in range(0, len(kv_refs), 2):
        k_ref, v_ref = kv_refs[i], kv_refs[i + 1]
        t = k_ref.shape[1]
        size = min(KEY_CHUNK, t)
        for c in range(0, t, size):
            chunks.append((lambda tag, r=k_ref, c=c, n=size: r[0, c:c + n, tag[0]],
                           lambda tag, r=v_ref, c=c, n=size: r[0, c:c + n, tag[1]], None))
    return chunks


def _attn_a_kernel(lam_init, nkv, tq, q_ref, *refs):
    kv_refs = refs[:2 * nkv]
    lq1_ref, lk1_ref, lq2_ref, lk2_ref, gsub_ref, o_ref = refs[2 * nkv:2 * nkv + 6]
    s_scr = refs[2 * nkv + 6:]
    lam = (jnp.exp(jnp.sum(lq1_ref[...] * lk1_ref[...], axis=-1, keepdims=True))
           - jnp.exp(jnp.sum(lq2_ref[...] * lk2_ref[...], axis=-1, keepdims=True)) + lam_init)
    lane = lax.broadcasted_iota(jnp.int32, (tq, LANES), 1)
    tile_rows = lambda i: pl.ds(pl.multiple_of(i * tq, tq), tq)

    def tile_maps(i):
        maps = []
        for h in range(A_HPS):
            sl = slice(h * LANES, (h + 1) * LANES)
            q = q_ref[0, tile_rows(i), sl]
            zero = jnp.zeros_like(q)
            maps.append((jnp.where(lane < A_DIM, q, zero), (sl, sl), None))
            maps.append((jnp.where(lane >= A_DIM, q, zero), (sl, sl), None))
        return maps

    def finish(i, outs):
        for h in range(A_HPS):
            o = outs[2 * h] - lam * outs[2 * h + 1]
            o_ref[0, tile_rows(i), h * LANES:(h + 1) * LANES] = (
                _rms(o, gsub_ref[...]) * (1.0 - lam_init)).astype(BF16)

    _attend_tiles(q_ref.shape[1] // tq, tile_maps, _ref_chunks(kv_refs), s_scr, finish)


def _attn_a(lam_init, q, kv, lq1, lk1, lq2, lk2, g_sub):
    bsz, n, _ = q.shape
    tq = min(Q_TILE, n)
    w = A_HPS * LANES
    small = lambda a: pl.BlockSpec(a.shape, lambda b, h: (0, 0))
    return pl.pallas_call(
        functools.partial(_attn_a_kernel, lam_init, len(kv) // 2, tq),
        grid=(bsz, A_HEADS // A_HPS),
        in_specs=[pl.BlockSpec((1, n, w), lambda b, h: (b, 0, h))]
        + [pl.BlockSpec((1, a.shape[1], w), lambda b, h: (b, 0, h)) for a in kv]
        + [small(lq1), small(lk1), small(lq2), small(lk2), small(g_sub)],
        out_specs=pl.BlockSpec((1, n, w), lambda b, h: (b, 0, h)),
        out_shape=jax.ShapeDtypeStruct((bsz, n, A_HEADS * LANES), BF16),
        scratch_shapes=_score_buffers(kv, tq),
        compiler_params=_cparams(("arbitrary", "arbitrary")),
        name="attn_a",
    )(q, *kv, lq1, lk1, lq2, lk2, g_sub)


def _attn_b_kernel(nkv, tq, q_ref, *refs):
    kv_refs, o_ref, s_scr = refs[:2 * nkv], refs[2 * nkv], refs[2 * nkv + 1:]
    lane = lax.broadcasted_iota(jnp.int32, (tq, LANES), 1)
    tile_rows = lambda i: pl.ds(pl.multiple_of(i * tq, tq), tq)

    def tile_maps(i):
        maps = []
        for h in range(B_HPS):
            sl = slice(h * LANES, (h + 1) * LANES)
            vsl = slice((h // 2) * LANES, (h // 2 + 1) * LANES)
            maps.append((q_ref[0, tile_rows(i), sl], (sl, vsl), None))
        return maps

    def finish(i, outs):
        for pair in range(B_HPS // 2):
            o_ref[0, tile_rows(i), pair * LANES:(pair + 1) * LANES] = (
                jnp.where(lane < B_V, outs[2 * pair], outs[2 * pair + 1]).astype(BF16))

    _attend_tiles(q_ref.shape[1] // tq, tile_maps, _ref_chunks(kv_refs), s_scr, finish)


def _attn_b(q, kv):
    bsz, n, _ = q.shape
    tq = min(Q_TILE, n)
    kv_specs = []
    for i, a in enumerate(kv):
        w = B_HPS * (LANES if i % 2 == 0 else B_V)
        kv_specs.append(pl.BlockSpec((1, a.shape[1], w), lambda b, h: (b, 0, h)))
    return pl.pallas_call(
        functools.partial(_attn_b_kernel, len(kv) // 2, tq),
        grid=(bsz, B_HEADS // B_HPS),
        in_specs=[pl.BlockSpec((1, n, B_HPS * LANES), lambda b, h: (b, 0, h))] + kv_specs,
        out_specs=pl.BlockSpec((1, n, B_HPS * B_V), lambda b, h: (b, 0, h)),
        out_shape=jax.ShapeDtypeStruct((bsz, n, B_HEADS * B_V), BF16),
        scratch_shapes=_score_buffers(kv, tq),
        compiler_params=_cparams(("arbitrary", "arbitrary")),
        name="attn_b",
    )(q, *kv)


def _attn_c_kernel(latent, nblk, *refs):
    if latent:
        q_ref, kx_ref, vx_ref, sink_ref, kw_ref, vw_ref, bias_ref, o_ref = refs[:8]
    else:
        q_ref, kx_ref, vx_ref, sink_ref, o_ref = refs[:5]
    s_scr = refs[8 if latent else 5:]
    lane = lax.broadcasted_iota(jnp.int32, (Q_BLOCK, LANES), 1)
    block_rows = lambda j: pl.ds(pl.multiple_of(j * Q_BLOCK, Q_BLOCK), Q_BLOCK)

    chunks = [(lambda j: kx_ref[0], lambda j: vx_ref[0], None)]
    if latent:
        win_rows = lambda j: pl.ds(pl.multiple_of(j * Q_BLOCK, Q_BLOCK), 3 * Q_BLOCK)

        def bias_of(j):
            first_i = (jnp.asarray(j) == 0).astype(jnp.int32)
            last_i = (jnp.asarray(j) == nblk - 1).astype(jnp.int32)
            return bias_ref[1 - first_i + last_i + 2 * first_i * last_i]

        chunks.append((lambda j: kw_ref[0, win_rows(j), :], lambda j: vw_ref[0, win_rows(j), :], bias_of))

    def tile_maps(j):
        maps = []
        for r in range(C_REP):
            qr = q_ref[0, block_rows(j), r * LANES:(r + 1) * LANES]
            halves = [jnp.where((lane >= g * C_DIM) & (lane < (g + 1) * C_DIM), qr, jnp.zeros_like(qr))
                      for g in range(C_KV_HEADS)]
            rows = slice(C_KV_HEADS * r * Q_BLOCK, C_KV_HEADS * (r + 1) * Q_BLOCK)
            maps.append((jnp.concatenate(halves, axis=0), j, sink_ref[rows, :] * LOG2E))
        return maps

    def finish(j, outs):
        for r, o in enumerate(outs):
            o_ref[0, block_rows(j), r * LANES:(r + 1) * LANES] = (
                jnp.where(lane < C_DIM, o[:Q_BLOCK], o[Q_BLOCK:]).astype(BF16))

    _attend_tiles(nblk, tile_maps, chunks, s_scr, finish)


def _attn_c(q, kx, vx, sink_rows, k_lat=None, v_lat=None):
    bsz, n, _ = q.shape
    tc = kx.shape[1]
    nblk = n // Q_BLOCK
    latent = k_lat is not None
    rows = C_KV_HEADS * Q_BLOCK
    whole = lambda a: pl.BlockSpec((1,) + a.shape[1:], lambda b: (b,) + (0,) * (a.ndim - 1))
    const = lambda a: pl.BlockSpec(a.shape, lambda b: (0,) * a.ndim)
    in_specs = [whole(q), whole(kx), whole(vx), const(sink_rows)]
    args = [q, kx, vx, sink_rows]
    scratch = [pltpu.VMEM((rows, tc), F32)]
    if latent:
        pad = ((0, 0), (Q_BLOCK, Q_BLOCK), (0, 0))
        kw, vw = jnp.pad(k_lat, pad), jnp.pad(v_lat, pad)
        row = (jnp.arange(rows) % Q_BLOCK)[:, None]
        col = jnp.arange(3 * Q_BLOCK)[None, :]
        in_prev, in_next = col < Q_BLOCK, col >= 2 * Q_BLOCK
        band = jnp.where(in_prev, col >= row, jnp.where(in_next, col - 2 * Q_BLOCK <= row, True))
        variants = [band & ~in_prev, band, band & ~in_next, band & ~in_prev & ~in_next]
        bias = jnp.stack([jnp.where(ok, 0.0, NEG_INF).astype(F32) for ok in variants], axis=0)
        in_specs += [whole(kw), whole(vw), const(bias)]
        args += [kw, vw, bias]
        scratch.append(pltpu.VMEM((rows, 3 * Q_BLOCK), F32))
    return pl.pallas_call(
        functools.partial(_attn_c_kernel, latent, nblk),
        grid=(bsz,),
        in_specs=in_specs,
        out_specs=whole(q),
        out_shape=jax.ShapeDtypeStruct((bsz, n, 512), BF16),
        scratch_shapes=scratch,
        compiler_params=_cparams(("arbitrary",)),
        name="attn_c_latent" if latent else "attn_c_context",
    )(*args)


def _s5_prepare_kernel(are_ref, aim_ref, ldt_ref, btre_ref, btim_ref, cre_ref, cim_ref,
                       arec_ref, aimc_ref, ctre_ref, ctim_ref,
                       td_ref, wd_ref, vd_ref, a16re_ref, a16im_ref):
    p, n = D_GROUP, D_STATE
    gl = pl.program_id(0) % S5_SG

    def one_hot(shape, target):
        r = lax.broadcasted_iota(jnp.int32, shape, 0)
        c = lax.broadcasted_iota(jnp.int32, shape, 1)
        return jnp.where(c == target(r), 1.0, 0.0).astype(BF16)

    place_p = one_hot((p, LANES), lambda r: gl * p + r)
    place_n = one_hot((n, S5_SLB * LANES), lambda r: gl * n + r)
    pbits = p.bit_length() - 1
    place_tp = one_hot((S5_CHUNK * p, S5_FLAT), lambda r: (r >> pbits) * LANES + gl * p + (r & (p - 1)))

    refs = (are_ref, aim_ref, ldt_ref, btre_ref, btim_ref, cre_ref, cim_ref, a16re_ref, a16im_ref)
    kt_fwd, w_fwd = _s5_prepare_direction(0, *refs)
    kt_bwd, w_bwd = _s5_prepare_direction(1, *refs)
    pieces = [kt_bwd[(S5_CHUNK - 1 - i) * p:(S5_CHUNK - i) * p] for i in range(S5_CHUNK - 1)]
    pieces.append(kt_fwd[0:p] + kt_bwd[0:p])
    pieces.append(kt_fwd[p:])
    lagk = _dot(jnp.concatenate(pieces, axis=0).astype(BF16), place_p)
    for t_in in range(S5_CHUNK):
        row = [lagk[(t_out - t_in + S5_CHUNK - 1) * p:(t_out - t_in + S5_CHUNK) * p] for t_out in range(S5_CHUNK)]
        td_ref[0, t_in, 0] = jnp.concatenate(row, axis=1).astype(BF16)

    w_sets = [w_fwd[0][::-1], w_fwd[1][::-1], w_bwd[0], w_bwd[1]]
    for x, rows in enumerate(w_sets):
        wexp = _dot(jnp.concatenate(rows, axis=0).astype(BF16), place_n)
        for t in range(S5_CHUNK):
            wd_ref[0, t, 0, :, x * S5_SLB * LANES:(x + 1) * S5_SLB * LANES] = wexp[t * p:(t + 1) * p].astype(BF16)

    tt = lax.broadcasted_iota(jnp.int32, (n, S5_CHUNK * p), 1) >> pbits
    for d in range(N_DIRS):
        power = (tt + 1 if d == 0 else S5_CHUNK - tt).astype(F32)
        dt = jnp.exp(ldt_ref[d, 0])
        mag = jnp.exp(power * (arec_ref[d, 0] * dt))
        ang = power * (aimc_ref[d, 0] * dt)
        pr, pi = mag * jnp.cos(ang), mag * jnp.sin(ang)
        ctre, ctim = ctre_ref[d, 0], ctim_ref[d, 0]
        vd_ref[0, 2 * d, 0] = _dot((ctre * pr - ctim * pi).astype(BF16), place_tp).astype(BF16)
        vd_ref[0, 2 * d + 1, 0] = _dot((-(ctre * pi + ctim * pr)).astype(BF16), place_tp).astype(BF16)


def _s5_prepare_direction(d, are_ref, aim_ref, ldt_ref, btre_ref, btim_ref, cre_ref, cim_ref, a16re_ref, a16im_ref):
    hi = lax.Precision.HIGHEST
    are = are_ref[d, 0]
    aim = aim_ref[d, 0]
    dt = jnp.exp(ldt_ref[d, 0])
    nj = S5_CHUNK + 1
    jj = lax.broadcasted_iota(jnp.int32, (nj, D_STATE), 0).astype(F32)
    mag = jnp.exp(jj * (are * dt))
    ang = jj * (aim * dt)
    pre = mag * jnp.cos(ang)
    pim = mag * jnp.sin(ang)
    xr = pre[1:2] - 1.0
    xi = pim[1:2]
    den = are * are + aim * aim
    fr = (xr * are + xi * aim) / den
    fi = (xi * are - xr * aim) / den
    btre = btre_ref[d, 0]
    btim = btim_ref[d, 0]
    bbre = fr * btre - fi * btim
    bbim = fr * btim + fi * btre
    cre = cre_ref[d, 0]
    cim = cim_ref[d, 0]
    wj_re, wj_im = [], []
    for j in range(S5_CHUNK):
        pr = pre[j:j + 1]
        pi = pim[j:j + 1]
        wj_re.append(pr * bbre - pi * bbim)
        wj_im.append(pr * bbim + pi * bbre)
    dn = (((1,), (1,)), ((), ()))
    a16re_ref[d, 0] = pre[S5_CHUNK:S5_CHUNK + 1]
    a16im_ref[d, 0] = pim[S5_CHUNK:S5_CHUNK + 1]
    kt = (lax.dot_general(jnp.concatenate(wj_re, axis=0), cre, dn, precision=hi, preferred_element_type=F32)
          - lax.dot_general(jnp.concatenate(wj_im, axis=0), cim, dn, precision=hi, preferred_element_type=F32))
    return kt, (wj_re, wj_im)


def _s5_prepare(a_re, a_im, log_dt, b_re, b_im, c_re, c_im):
    g, n, p = D_GROUPS, D_STATE, D_GROUP
    nsg = g // S5_SG
    v4 = lambda a: a.reshape(N_DIRS, g, 1, a.shape[-1])
    col = lambda a: a.reshape(N_DIRS, g, n, 1)
    spec = lambda *s: pl.BlockSpec((N_DIRS, 1) + s, lambda i: (0, i) + (0,) * len(s))
    grp = lambda *s: pl.BlockSpec((1, s[0], 1) + s[1:], lambda i: (i // S5_SG, 0, i % S5_SG, 0, 0))
    bt_re = jnp.swapaxes(b_re, -1, -2)
    bt_im = jnp.swapaxes(b_im, -1, -2)
    ct_re = jnp.tile(jnp.swapaxes(c_re, -1, -2), (1, 1, 1, S5_CHUNK))
    ct_im = jnp.tile(jnp.swapaxes(c_im, -1, -2), (1, 1, 1, S5_CHUNK))
    outs = pl.pallas_call(
        _s5_prepare_kernel,
        grid=(g,),
        in_specs=[spec(1, n), spec(1, n), spec(1, 1), spec(p, n), spec(p, n), spec(p, n), spec(p, n),
                  spec(n, 1), spec(n, 1), spec(n, S5_CHUNK * p), spec(n, S5_CHUNK * p)],
        out_specs=[grp(S5_CHUNK, p, S5_FLAT), grp(S5_CHUNK, p, 4 * S5_SLB * LANES), grp(4, n, S5_FLAT),
                   spec(1, n), spec(1, n)],
        out_shape=[jax.ShapeDtypeStruct((nsg, S5_CHUNK, S5_SG, p, S5_FLAT), BF16),
                   jax.ShapeDtypeStruct((nsg, S5_CHUNK, S5_SG, p, 4 * S5_SLB * LANES), BF16),
                   jax.ShapeDtypeStruct((nsg, 4, S5_SG, n, S5_FLAT), BF16)]
        + [jax.ShapeDtypeStruct((N_DIRS, g, 1, n), F32)] * 2,
        compiler_params=_cparams(("arbitrary",)),
        name="s5_prepare",
    )(v4(a_re), v4(a_im), log_dt.reshape(N_DIRS, g, 1, 1), bt_re, bt_im, c_re, c_im,
      col(a_re), col(a_im), ct_re, ct_im)
    td, wd, vd, a16re, a16im = outs
    t_sg = td.reshape(nsg, S5_FLAT, S5_FLAT)
    w_sg = wd.reshape(nsg, S5_FLAT, 4 * S5_SLB * LANES)
    v_sg = vd.reshape(nsg, 4 * S5_SG * n, S5_FLAT)
    a16 = [a16re[0].reshape(1, g * n), a16im[0].reshape(1, g * n),
           a16re[1].reshape(1, g * n), a16im[1].reshape(1, g * n)]
    return t_sg, w_sg, v_sg, a16


def _s5_gather_chunks(du_ref, u_scr, bsz, ck):
    for b in range(bsz):
        for t in range(S5_CHUNK):
            u_scr[b * ck:(b + 1) * ck, t * LANES:(t + 1) * LANES] = (
                du_ref[b, pl.ds(t, ck, stride=S5_CHUNK), :].astype(BF16))


def _s5_state_in_kernel(bsz, bpad, ck, du_ref, w_ref, o0_ref, o1_ref, o2_ref, o3_ref, u_scr):
    _s5_gather_chunks(du_ref, u_scr, bsz, ck)
    s = _dot(u_scr[...], w_ref[0])
    for x, o_ref in enumerate((o0_ref, o1_ref, o2_ref, o3_ref)):
        for j in range(S5_SLB):
            col = (x * S5_SLB + j) * LANES
            for b in range(bpad):
                if b < bsz:
                    o_ref[j, pl.ds(b, ck, stride=bpad), :] = s[b * ck:(b + 1) * ck, col:col + LANES]
                else:
                    o_ref[j, pl.ds(b, ck, stride=bpad), :] = jnp.zeros((ck, LANES), F32)


def _s5_state_in(du, w_sg, bpad, ck):
    bsz, n, _ = du.shape
    nchunk = n // S5_CHUNK
    nsg = w_sg.shape[0]
    return pl.pallas_call(
        functools.partial(_s5_state_in_kernel, bsz, bpad, ck),
        grid=(nsg, nchunk // ck),
        in_specs=[pl.BlockSpec((bsz, ck * S5_CHUNK, LANES), lambda s, r: (0, r, s)),
                  pl.BlockSpec((1, S5_FLAT, 4 * S5_SLB * LANES), lambda s, r: (s, 0, 0),
                               pipeline_mode=pl.Buffered(1))],
        out_specs=[pl.BlockSpec((S5_SLB, ck * bpad, LANES), lambda s, r: (s, r, 0))] * 4,
        out_shape=[jax.ShapeDtypeStruct((nsg * S5_SLB, nchunk * bpad, LANES), F32)] * 4,
        scratch_shapes=[pltpu.VMEM((bsz * ck, S5_FLAT), BF16)],
        compiler_params=_cparams(("arbitrary", "arbitrary")),
        name="s5_state_in",
    )(du, w_sg)


def _s5_scan_kernel(nchunk, bpad, s0re_ref, s0im_ref, s1re_ref, s1im_ref, a0re_ref, a0im_ref, a1re_ref, a1im_ref,
                    h0re_ref, h0im_ref, h1re_ref, h1im_ref,
                    p0re_ref, p0im_ref, p1re_ref, p1im_ref, f0re_ref, f0im_ref, f1re_ref, f1im_ref):
    a0re, a0im, a1re, a1im = a0re_ref[...], a0im_ref[...], a1re_ref[...], a1im_ref[...]

    def body(i, carry):
        r0, i0, r1, i1 = carry
        k = pl.ds(pl.multiple_of(i * bpad, bpad), bpad)
        kb = pl.ds(pl.multiple_of((nchunk - 1 - i) * bpad, bpad), bpad)
        p0re_ref[:, k, :] = r0
        p0im_ref[:, k, :] = i0
        p1re_ref[:, kb, :] = r1
        p1im_ref[:, kb, :] = i1
        n_r0 = a0re * r0 - a0im * i0 + s0re_ref[:, k, :]
        n_i0 = a0re * i0 + a0im * r0 + s0im_ref[:, k, :]
        n_r1 = a1re * r1 - a1im * i1 + s1re_ref[:, kb, :]
        n_i1 = a1re * i1 + a1im * r1 + s1im_ref[:, kb, :]
        return n_r0, n_i0, n_r1, n_i1

    r0, i0, r1, i1 = lax.fori_loop(0, nchunk, body,
                                   (h0re_ref[...], h0im_ref[...], h1re_ref[...], h1im_ref[...]))
    f0re_ref[...] = r0
    f0im_ref[...] = i0
    f1re_ref[...] = r1
    f1im_ref[...] = i1


def _s5_scan(s_in, a16, h0, nchunk, bpad):
    nlb = D_GROUPS * D_STATE // LANES
    lb = 2
    seq_spec = pl.BlockSpec((lb, nchunk * bpad, LANES), lambda i: (i, 0, 0))
    a_spec = pl.BlockSpec((lb, 1, LANES), lambda i: (i, 0, 0))
    h_spec = pl.BlockSpec((lb, bpad, LANES), lambda i: (i, 0, 0))
    outs = pl.pallas_call(
        functools.partial(_s5_scan_kernel, nchunk, bpad),
        grid=(nlb // lb,),
        in_specs=[seq_spec] * 4 + [a_spec] * 4 + [h_spec] * 4,
        out_specs=[seq_spec] * 4 + [h_spec] * 4,
        out_shape=[jax.ShapeDtypeStruct((nlb, nchunk * bpad, LANES), F32)] * 4
        + [jax.ShapeDtypeStruct((nlb, bpad, LANES), F32)] * 4,
        compiler_params=_cparams(("arbitrary",)),
        name="s5_scan",
    )(*s_in, *a16, *h0)
    return outs[:4], outs[4:]


def _s5_output_kernel(bsz, bpad, ck, du_ref, t_ref, v_ref, p0_ref, p1_ref, p2_ref, p3_ref, y_ref, u_scr, p_scr):
    _s5_gather_chunks(du_ref, u_scr, bsz, ck)
    for x, p_ref in enumerate((p0_ref, p1_ref, p2_ref, p3_ref)):
        for j in range(S5_SLB):
            col = (x * S5_SLB + j) * LANES
            for b in range(bsz):
                p_scr[b * ck:(b + 1) * ck, col:col + LANES] = (
                    p_ref[j, pl.ds(b, ck, stride=bpad), :].astype(BF16))
    y = _dot(u_scr[...], t_ref[0]) + _dot(p_scr[...], v_ref[0])
    for b in range(bsz):
        for t in range(S5_CHUNK):
            y_ref[b, pl.ds(t, ck, stride=S5_CHUNK), :] = y[b * ck:(b + 1) * ck, t * LANES:(t + 1) * LANES]


def _s5_output(du, t_sg, v_sg, p_states, bpad, ck):
    bsz, n, _ = du.shape
    nchunk = n // S5_CHUNK
    nsg = t_sg.shape[0]
    tok = pl.BlockSpec((bsz, ck * S5_CHUNK, LANES), lambda s, r: (0, r, s))
    mat = pl.BlockSpec((1, S5_FLAT, S5_FLAT), lambda s, r: (s, 0, 0), pipeline_mode=pl.Buffered(1))
    return pl.pallas_call(
        functools.partial(_s5_output_kernel, bsz, bpad, ck),
        grid=(nsg, nchunk // ck),
        in_specs=[tok, mat, mat] + [pl.BlockSpec((S5_SLB, ck * bpad, LANES), lambda s, r: (s, r, 0))] * 4,
        out_specs=tok,
        out_shape=jax.ShapeDtypeStruct((bsz, n, MIX_W), F32),
        scratch_shapes=[pltpu.VMEM((bsz * ck, S5_FLAT), BF16), pltpu.VMEM((bsz * ck, S5_FLAT), BF16)],
        compiler_params=_cparams(("arbitrary", "arbitrary")),
        name="s5_output",
    )(du, t_sg, v_sg, *p_states)


def _s5_mixer(du, mats, h0):
    t_sg, w_sg, v_sg, a16 = mats
    bsz, n, _ = du.shape
    nchunk = n // S5_CHUNK
    bpad = -(-bsz // 8) * 8
    ck = min(nchunk, max(S5_ROWS // bsz, 1))
    assert nchunk % ck == 0 and (ck * S5_CHUNK) % 8 == 0
    to_blocks = lambda h: h.reshape(h.shape[0], -1, LANES).transpose(1, 0, 2)
    h0 = [to_blocks(jnp.pad(h, ((0, bpad - bsz), (0, 0)))) for h in h0]
    a16 = [to_blocks(a) for a in a16]
    s_in = _s5_state_in(du, w_sg, bpad, ck)
    p_states, finals = _s5_scan(s_in, a16, h0, nchunk, bpad)
    y = _s5_output(du, t_sg, v_sg, p_states, bpad, ck)
    return y, [f.transpose(1, 0, 2).reshape(bpad, -1)[:bsz] for f in finals]


def _merge_kernel(x_ref, hb_ref, oa_ref, ob_ref, oc_ref, y_ref, u_ref, mod_ref, dskip_ref, wglu_ref,
                  wgate_ref, wbr_ref, wo_ref, gpost_ref, xo_ref):
    yv = y_ref[...] + dskip_ref[...] * u_ref[...]
    yg = jax.nn.gelu(yv)
    od = yg * jax.nn.sigmoid(_dot(yg.astype(BF16), wglu_ref[...]))
    hb = hb_ref[...]
    branches = (oa_ref[...], ob_ref[...], oc_ref[...], od.astype(BF16))
    merged = None
    for k, o in enumerate(branches):
        gate = jax.nn.sigmoid(_dot(hb, wgate_ref[:, k * D_MODEL:(k + 1) * D_MODEL]))
        term = gate * _dot(o, wbr_ref[k])
        merged = term if merged is None else merged + term
    z = _dot(merged.astype(BF16), wo_ref[...])
    gate_m = mod_ref[0, 2:3, :]
    xo_ref[...] = x_ref[...] + gate_m * _rms(z, gpost_ref[...])


def _merge(x2, hb, oa, ob, oc, y, du, mod6, d_skip, w_glu, w_gate, w_branch, w_o, g_post, seq):
    m = x2.shape[0]
    tm = _row_tile(m)
    tiles_per_seq = max(seq // tm, 1)
    assert mod6.shape[0] == 1 or seq % tm == 0
    row = lambda w: pl.BlockSpec((tm, w), lambda i: (i, 0))
    full = _resident
    if mod6.shape[0] == 1:
        mod_spec = pl.BlockSpec((1, 6, D_MODEL), lambda i: (0, 0, 0))
    else:
        mod_spec = pl.BlockSpec((1, 6, D_MODEL), lambda i: (i // tiles_per_seq, 0, 0))
    return pl.pallas_call(
        _merge_kernel,
        grid=(m // tm,),
        in_specs=[row(D_MODEL), row(D_MODEL), row(MIX_W), row(MIX_W), row(MIX_W), row(MIX_W), row(MIX_W),
                  mod_spec, full(d_skip), full(w_glu), full(w_gate), full(w_branch), full(w_o), full(g_post)],
        out_specs=row(D_MODEL),
        out_shape=jax.ShapeDtypeStruct((m, D_MODEL), F32),
        compiler_params=_cparams(("arbitrary",)),
        name="merge",
    )(x2, hb, oa, ob, oc, y, du, mod6, d_skip, w_glu, w_gate, w_branch, w_o, g_post)


def _ffn_kernel(x_ref, mod_ref, gpre_ref, w1_ref, w3_ref, w2_ref, gpost_ref, xo_ref):
    x = x_ref[...]
    shift = mod_ref[0, 3:4, :]
    scale = mod_ref[0, 4:5, :]
    gate = mod_ref[0, 5:6, :]
    h2 = (_rms(x, gpre_ref[...]) * (1.0 + scale) + shift).astype(BF16)
    a = _dot(h2, w1_ref[...])
    b = _dot(h2, w3_ref[...])
    f = _dot((jax.nn.silu(a) * b).astype(BF16), w2_ref[...])
    xo_ref[...] = x + gate * _rms(f, gpost_ref[...])


def _ffn(x2, mod6, g_pre, w1, w3, w2, g_post, seq):
    m = x2.shape[0]
    tm = _row_tile(m)
    tiles_per_seq = max(seq // tm, 1)
    assert mod6.shape[0] == 1 or seq % tm == 0
    row = lambda w: pl.BlockSpec((tm, w), lambda i: (i, 0))
    full = _resident
    if mod6.shape[0] == 1:
        mod_spec = pl.BlockSpec((1, 6, D_MODEL), lambda i: (0, 0, 0))
    else:
        mod_spec = pl.BlockSpec((1, 6, D_MODEL), lambda i: (i // tiles_per_seq, 0, 0))
    return pl.pallas_call(
        _ffn_kernel,
        grid=(m // tm,),
        in_specs=[row(D_MODEL), mod_spec, full(g_pre), full(w1), full(w3), full(w2), full(g_post)],
        out_specs=row(D_MODEL),
        out_shape=jax.ShapeDtypeStruct((m, D_MODEL), F32),
        compiler_params=_cparams(("arbitrary",)),
        name="ffn",
    )(x2, mod6, g_pre, w1, w3, w2, g_post)


def _rope_tables(n, rot_dim, lane_off):
    rows = n // GRID_W
    pos_row = jnp.repeat(jnp.arange(rows, dtype=F32), GRID_W)
    pos_col = jnp.tile(jnp.arange(GRID_W, dtype=F32), rows)
    n_freq = rot_dim // 4
    inv_freq = ROPE_BASE ** (-jnp.arange(n_freq, dtype=F32) / n_freq)
    ang = jnp.concatenate([pos_row[:, None] * inv_freq, pos_col[:, None] * inv_freq], axis=-1)
    cos, sin = jnp.cos(ang), jnp.sin(ang)
    zero = jnp.zeros_like(sin)
    c = jnp.concatenate([cos, cos], axis=-1)
    sa = jnp.concatenate([-sin, zero], axis=-1)
    sb = jnp.concatenate([zero, sin], axis=-1)
    if rot_dim == LANES // 2 and lane_off == 0:
        return tuple(jnp.tile(t, (1, 2)) for t in (c, sa, sb))
    pad = lambda t, fill: jnp.concatenate(
        [jnp.full((n, lane_off), fill, F32), t, jnp.full((n, LANES - lane_off - rot_dim), fill, F32)], axis=-1)
    return pad(c, 1.0), pad(sa, 0.0), pad(sb, 0.0)


def _layer_weights(l, w_in, w_gate, w_b_uq, w_b_ukv, sink_c, w_glu, w_branch, w_o, w_ff1, w_ff3, w_ff2):
    wi = w_in[l]
    aq, ak, av = wi[:, 0:512], wi[:, 512:1024], wi[:, 1024:1536]
    bcq, bckv, bkr = wi[:, 1536:1792], wi[:, 1792:1920], wi[:, 1920:1952]
    cq, ck, cv, du = wi[:, 1952:2464], wi[:, 2464:2592], wi[:, 2592:2720], wi[:, 2720:3232]
    cq_perm = cq.reshape(D_MODEL, C_KV_HEADS, C_REP, C_DIM).transpose(0, 2, 1, 3).reshape(D_MODEL, 512)
    zeros = lambda w: jnp.zeros((D_MODEL, w), F32)
    bkr_pad = jnp.concatenate([zeros(B_KR_LANE), bkr, zeros(LANES - B_KR_LANE - B_ROPE)], axis=1)
    w_in_r = jnp.concatenate([aq, ak, av, cq_perm, ck, cv, du, bcq, bckv, bkr_pad], axis=1).astype(BF16)

    uq = w_b_uq[l].reshape(B_QLORA, B_HEADS, B_NOPE + B_ROPE)
    uq = jnp.concatenate([uq, jnp.zeros((B_QLORA, B_HEADS, LANES - B_NOPE - B_ROPE), F32)], axis=-1)
    w_uq_r = uq.reshape(B_QLORA, B_HEADS * LANES).astype(BF16)
    ukv = w_b_ukv[l].reshape(B_KVLORA, B_HEADS, B_NOPE + B_V)
    kn = jnp.concatenate([ukv[..., :B_NOPE], jnp.zeros((B_KVLORA, B_HEADS, LANES - B_NOPE), F32)], axis=-1)
    w_ukv_r = jnp.concatenate([kn.reshape(B_KVLORA, B_HEADS * LANES),
                               ukv[..., B_NOPE:].reshape(B_KVLORA, B_HEADS * B_V)], axis=1).astype(BF16)

    sink_perm = sink_c[l].reshape(C_KV_HEADS, C_REP).T.reshape(C_HEADS)
    sink_rows = jnp.repeat(sink_perm, Q_BLOCK)[:, None]

    wbr = w_branch[l]
    wbr_c = wbr[2].reshape(C_KV_HEADS, C_REP, C_DIM, D_MODEL).transpose(1, 0, 2, 3).reshape(MIX_W, D_MODEL)
    w_branch_r = jnp.stack([wbr[0], wbr[1], wbr_c, wbr[3]], axis=0).astype(BF16)
    return dict(w_in=w_in_r, w_uq=w_uq_r, w_ukv=w_ukv_r, sink_rows=sink_rows,
                w_gate=w_gate[l].astype(BF16), w_glu=w_glu[l].astype(BF16), w_branch=w_branch_r,
                w_o=w_o[l].astype(BF16), w_ff1=w_ff1[l].astype(BF16), w_ff3=w_ff3[l].astype(BF16),
                w_ff2=w_ff2[l].astype(BF16))


def _trunk_layer(l, x, mod6, lw, sp, s5_mats, rope_tabs, ctx):
    latent = ctx is not None
    bsz, n, _ = x.shape
    m = bsz * n
    x2 = x.reshape(m, D_MODEL)
    row1 = lambda v: v.reshape(1, -1)

    outs = _premix(latent, x2, mod6, row1(sp['g_pre_mix']), lw['w_in'], row1(sp['g_b_q']), lw['w_uq'],
                   row1(sp['g_b_kv']), lw['w_ukv'], rope_tabs, n)
    hb, qa, ka, va, qb, kb, vb, qc, kc, vc, du = outs[:11]
    r3 = lambda t: t.reshape(bsz, n, t.shape[-1])
    qa, ka, va, qb, kb, vb, qc, kc, vc, du = map(r3, (qa, ka, va, qb, kb, vb, qc, kc, vc, du))

    lam_init = 0.8 - 0.6 * math.exp(-0.3 * l)
    lam_args = (row1(sp['lam_q1']), row1(sp['lam_k1']), row1(sp['lam_q2']), row1(sp['lam_k2']), row1(sp['g_a_sub']))
    if latent:
        past = ctx['a_k'].shape[1]
        ka_ctx = ctx['a_k'].reshape(bsz, past, 512).astype(BF16)
        va_ctx = ctx['a_v'].reshape(bsz, past, 512).astype(BF16)
        kr_pad = jnp.pad(ctx['b_kr'].reshape(bsz * past, B_ROPE),
                         ((0, 0), (B_KR_LANE, LANES - B_KR_LANE - B_ROPE)))
        kb_ctx, vb_ctx = _kvprep(ctx['b_ckv'].reshape(bsz * past, B_KVLORA), kr_pad, lw['w_ukv'])
        o_a = _attn_a(lam_init, qa, [ka, va, ka_ctx, va_ctx], *lam_args)
        o_b = _attn_b(qb, [kb, vb, kb_ctx.reshape(bsz, past, -1), vb_ctx.reshape(bsz, past, -1)])
        o_c = _attn_c(qc, ctx['c_k'].reshape(bsz, past, LANES).astype(BF16),
                      ctx['c_v'].reshape(bsz, past, LANES).astype(BF16), lw['sink_rows'], kc, vc)
        gn = D_GROUPS * D_STATE
        h0 = [ctx['d_re'][:, 0].reshape(bsz, gn), ctx['d_im'][:, 0].reshape(bsz, gn),
              ctx['d_re'][:, 1].reshape(bsz, gn), ctx['d_im'][:, 1].reshape(bsz, gn)]
    else:
        o_a = _attn_a(lam_init, qa, [ka, va], *lam_args)
        o_b = _attn_b(qb, [kb, vb])
        o_c = _attn_c(qc, kc, vc, lw['sink_rows'])
        h0 = [jnp.zeros((bsz, D_GROUPS * D_STATE), F32)] * 4
    y_s5, finals = _s5_mixer(du, s5_mats, h0)

    f2 = lambda t: t.reshape(m, t.shape[-1])
    x2 = _merge(x2, hb, f2(o_a), f2(o_b), f2(o_c), f2(y_s5), f2(du), mod6, row1(sp['ssm_d']), lw['w_glu'],
                lw['w_gate'], lw['w_branch'], lw['w_o'], row1(sp['g_post_mix']), n)
    x2 = _ffn(x2, mod6, row1(sp['g_pre_ffn']), lw['w_ff1'], lw['w_ff3'], lw['w_ff2'], row1(sp['g_post_ffn']), n)
    x = x2.reshape(bsz, n, D_MODEL)
    if latent:
        return x, None
    akf, avf, ckvf, krf, ckf, cvf = outs[11:]
    st = lambda t: t.reshape(bsz, D_GROUPS, D_STATE)
    new_ctx = {'a_k': akf.reshape(bsz, n, A_HEADS, 2 * A_DIM), 'a_v': avf.reshape(bsz, n, A_HEADS, 2 * A_DIM),
               'b_ckv': ckvf.reshape(bsz, n, B_KVLORA), 'b_kr': krf.reshape(bsz, n, B_ROPE),
               'c_k': ckf.reshape(bsz, n, C_KV_HEADS, C_DIM), 'c_v': cvf.reshape(bsz, n, C_KV_HEADS, C_DIM),
               'd_re': jnp.stack([st(finals[0]), st(finals[2])], axis=1),
               'd_im': jnp.stack([st(finals[1]), st(finals[3])], axis=1)}
    return x, new_ctx


def kernel(x_prompt, x_sample, cache_a_k, cache_a_v, cache_b_ckv, cache_b_kr, cache_c_k, cache_c_v, state_d_re, state_d_im, c, c_ctx, w_mod, b_mod, g_pre_mix, g_post_mix, g_pre_ffn, g_post_ffn, w_in, w_gate, lam_q1, lam_k1, lam_q2, lam_k2, g_a_sub, g_b_q, g_b_kv, w_b_uq, w_b_ukv, sink_c, ssm_a_re, ssm_a_im, ssm_log_dt, ssm_b_re, ssm_b_im, ssm_c_re, ssm_c_im, ssm_d, w_glu, w_branch, w_o, w_ff1, w_ff3, w_ff2):
    dec_b, dec_n, _ = x_sample.shape
    assert dec_b + 1 <= 8

    cond = jnp.concatenate([c_ctx[None, :], c, jnp.zeros((8 - 1 - dec_b, D_MODEL), F32)], axis=0)
    mod = _modulation(cond, w_mod, b_mod)

    small = dict(g_pre_mix=g_pre_mix, g_post_mix=g_post_mix, g_pre_ffn=g_pre_ffn, g_post_ffn=g_post_ffn,
                 lam_q1=lam_q1, lam_k1=lam_k1, lam_q2=lam_q2, lam_k2=lam_k2, g_a_sub=g_a_sub, g_b_q=g_b_q,
                 g_b_kv=g_b_kv, ssm_d=ssm_d)
    rope_tabs = _rope_tables(dec_n, A_DIM, 0) + _rope_tables(dec_n, B_ROPE, B_KR_LANE)

    layers = []
    for l in range(DEPTH):
        lw = _layer_weights(l, w_in, w_gate, w_b_uq, w_b_ukv, sink_c, w_glu, w_branch, w_o, w_ff1, w_ff3, w_ff2)
        sp = {k: v[l] for k, v in small.items()}
        s5_mats = _s5_prepare(ssm_a_re[l], ssm_a_im[l], ssm_log_dt[l], ssm_b_re[l], ssm_b_im[l],
                              ssm_c_re[l], ssm_c_im[l])
        mod_l = mod[l].reshape(8, 6, D_MODEL)
        layers.append((lw, sp, s5_mats, mod_l))

    y_prompt = x_prompt
    ctx_out = []
    for l, (lw, sp, s5_mats, mod_l) in enumerate(layers):
        y_prompt, new_ctx = _trunk_layer(l, y_prompt, mod_l[0:1], lw, sp, s5_mats, None, None)
        ctx_out.append(new_ctx)

    y_sample = x_sample
    for l, (lw, sp, s5_mats, mod_l) in enumerate(layers):
        cached = {'a_k': cache_a_k[:, l], 'a_v': cache_a_v[:, l], 'b_ckv': cache_b_ckv[:, l],
                  'b_kr': cache_b_kr[:, l], 'c_k': cache_c_k[:, l], 'c_v': cache_c_v[:, l],
                  'd_re': state_d_re[:, l], 'd_im': state_d_im[:, l]}
        y_sample, _ = _trunk_layer(l, y_sample, mod_l[1:1 + dec_b], lw, sp, s5_mats, rope_tabs, cached)

    stack = lambda name: jnp.stack([cx[name] for cx in ctx_out], axis=1)
    return (y_prompt, y_sample, stack('a_k'), stack('a_v'), stack('b_ckv'), stack('b_kr'),
            stack('c_k'), stack('c_v'), stack('d_re'), stack('d_im'))
```

```python
import functools
import math

import jax
import jax.numpy as jnp
import numpy as np
from jax import lax
from jax.experimental import pallas as pl
from jax.experimental.pallas import tpu as pltpu

F32 = jnp.float32
BF16 = jnp.bfloat16

D_MODEL = 1024
DEPTH = 2
GRID_W = 64
Q_BLOCK = 128
ROPE_BASE = 10000.0
RMS_EPS = 1e-6
NEG_INF = -1e30
LOG2E = math.log2(math.e)
MIX_W = D_MODEL // 2
N_BRANCH = 4
A_HEADS = 4
A_DIM = 64
B_HEADS = 8
B_NOPE = 64
B_ROPE = 32
B_V = 64
B_QLORA = 256
B_KVLORA = 128
C_HEADS = 8
C_KV_HEADS = 2
C_REP = C_HEADS // C_KV_HEADS
C_DIM = 64
D_GROUP = 16
D_GROUPS = MIX_W // D_GROUP
D_STATE = 64
N_DIRS = 2
FF_HIDDEN = ((8 * D_MODEL // 3 + 255) // 256) * 256

LANES = 128
S5_CHUNK = 16
S5_SG = LANES // D_GROUP
S5_FLAT = S5_CHUNK * LANES
S5_SLB = S5_SG * D_STATE // LANES
S5_ROWS = 256
VMEM_LIMIT = 56 * 1024 * 1024
ROW_TILE = 512
Q_TILE = 256
KEY_CHUNK = 512
A_HPS = 2
B_HPS = 4

COL_AQ, COL_AK, COL_AV = 0, 512, 1024
COL_CQ, COL_CK, COL_CV = 1536, 2048, 2176
COL_DU = 2304
COL_BCQ, COL_BCKV, COL_BKR = 2816, 3072, 3200
IN_COLS = 3328
B_KR_LANE = B_NOPE


def _cparams(sem):
    return pltpu.CompilerParams(dimension_semantics=sem, vmem_limit_bytes=VMEM_LIMIT)


def _resident(a):
    return pl.BlockSpec(a.shape, lambda i: (0,) * a.ndim, pipeline_mode=pl.Buffered(1))


def _row_tile(m):
    return min(ROW_TILE, m)


def _dot(a, b):
    return jnp.dot(a, b, preferred_element_type=F32)


def _dot_nt(a, b):
    return lax.dot_general(a, b, (((1,), (1,)), ((), ())), preferred_element_type=F32)


def _rms(x, g):
    return x * lax.rsqrt(jnp.mean(x * x, axis=-1, keepdims=True) + RMS_EPS) * g


def _rope(x, c, sa, sb, half):
    w = x.shape[-1]
    return x * c + pltpu.roll(x, w - half, 1) * sa + pltpu.roll(x, half, 1) * sb


def _mod_kernel(c_ref, w_ref, b_ref, o_ref):
    c = c_ref[...]
    o_ref[0] = _dot(jax.nn.silu(c).astype(BF16), w_ref[0].astype(BF16)) + b_ref[0]


def _modulation(cond, w_mod, b_mod):
    nblk = 6
    return pl.pallas_call(
        _mod_kernel,
        grid=(DEPTH, nblk),
        in_specs=[pl.BlockSpec((8, D_MODEL), lambda l, j: (0, 0)),
                  pl.BlockSpec((1, D_MODEL, D_MODEL), lambda l, j: (l, 0, j)),
                  pl.BlockSpec((1, 1, D_MODEL), lambda l, j: (l, 0, j))],
        out_specs=pl.BlockSpec((1, 8, D_MODEL), lambda l, j: (l, 0, j)),
        out_shape=jax.ShapeDtypeStruct((DEPTH, 8, 6 * D_MODEL), F32),
        compiler_params=_cparams(("arbitrary", "arbitrary")),
        name="modulation",
    )(cond, w_mod, b_mod.reshape(DEPTH, 1, 6 * D_MODEL))


def _premix_kernel(latent, *refs):
    if latent:
        (x_ref, mod_ref, g_ref, win_ref, gbq_ref, wuq_ref, gbkv_ref, wukv_ref,
         ca_ref, saa_ref, sba_ref, cb_ref, sab_ref, sbb_ref,
         hb_ref, qa_ref, ka_ref, va_ref, qb_ref, kb_ref, vb_ref, qc_ref, kc_ref, vc_ref, du_ref) = refs
    else:
        (x_ref, mod_ref, g_ref, win_ref, gbq_ref, wuq_ref, gbkv_ref, wukv_ref,
         hb_ref, qa_ref, ka_ref, va_ref, qb_ref, kb_ref, vb_ref, qc_ref, kc_ref, vc_ref, du_ref,
         akf_ref, avf_ref, ckvf_ref, krf_ref, ckf_ref, cvf_ref) = refs

    x = x_ref[...]
    shift = mod_ref[0, 0:1, :]
    scale = mod_ref[0, 1:2, :]
    h = _rms(x, g_ref[...]) * (1.0 + scale) + shift
    hb = h.astype(BF16)
    hb_ref[...] = hb
    proj = _dot(hb, win_ref[...])

    if latent:
        ca, saa, sba = ca_ref[...], saa_ref[...], sba_ref[...]
        cb, sab, sbb = cb_ref[...], sab_ref[...], sbb_ref[...]
        rope_a = lambda t: _rope(t, ca, saa, sba, A_DIM // 2)
        rope_b = lambda t: _rope(t, cb, sab, sbb, B_ROPE // 2)
    else:
        rope_a = rope_b = lambda t: t

    def blk(col, i):
        return proj[:, col + i * LANES: col + (i + 1) * LANES]

    a_scale = A_DIM ** -0.5 * LOG2E
    for i in range(A_HEADS):
        sl = slice(i * LANES, (i + 1) * LANES)
        qa_ref[:, sl] = (rope_a(blk(COL_AQ, i)) * a_scale).astype(BF16)
        ka_ref[:, sl] = rope_a(blk(COL_AK, i)).astype(BF16)
    va_ref[...] = proj[:, COL_AV:COL_AV + 512].astype(BF16)

    c_scale = C_DIM ** -0.5 * LOG2E
    for i in range(C_REP):
        sl = slice(i * LANES, (i + 1) * LANES)
        qc_ref[:, sl] = (rope_a(blk(COL_CQ, i)) * c_scale).astype(BF16)
    kc_ref[...] = rope_a(blk(COL_CK, 0)).astype(BF16)
    vc_ref[...] = blk(COL_CV, 0).astype(BF16)

    du_ref[...] = proj[:, COL_DU:COL_DU + MIX_W]

    b_scale = (B_NOPE + B_ROPE) ** -0.5 * LOG2E
    cqn = _rms(proj[:, COL_BCQ:COL_BCQ + B_QLORA], gbq_ref[...])
    qb = _dot(cqn.astype(BF16), wuq_ref[...])
    ckv = _rms(proj[:, COL_BCKV:COL_BCKV + B_KVLORA], gbkv_ref[...])
    kvb = _dot(ckv.astype(BF16), wukv_ref[...])
    kr_pad = rope_b(blk(COL_BKR, 0))
    for i in range(B_HEADS):
        sl = slice(i * LANES, (i + 1) * LANES)
        qb_ref[:, sl] = (rope_b(qb[:, sl]) * b_scale).astype(BF16)
        kb_ref[:, sl] = (kvb[:, sl] + kr_pad).astype(BF16)
    vb_ref[...] = kvb[:, B_HEADS * LANES:].astype(BF16)

    if not latent:
        akf_ref[...] = proj[:, COL_AK:COL_AK + 512]
        avf_ref[...] = proj[:, COL_AV:COL_AV + 512]
        ckvf_ref[...] = ckv
        krf_ref[...] = kr_pad[:, B_KR_LANE:B_KR_LANE + B_ROPE]
        ckf_ref[...] = blk(COL_CK, 0)
        cvf_ref[...] = blk(COL_CV, 0)


def _premix(latent, x2, mod6, g_pre, w_in, g_bq, w_uq, g_bkv, w_ukv, rope_tabs, seq):
    m = x2.shape[0]
    tm = _row_tile(m)
    tiles_per_seq = max(seq // tm, 1)
    nb_mod = mod6.shape[0]
    assert nb_mod == 1 or seq % tm == 0
    row = lambda w: pl.BlockSpec((tm, w), lambda i: (i, 0))
    full = _resident
    if nb_mod == 1:
        mod_spec = pl.BlockSpec((1, 6, D_MODEL), lambda i: (0, 0, 0))
    else:
        mod_spec = pl.BlockSpec((1, 6, D_MODEL), lambda i: (i // tiles_per_seq, 0, 0))
    in_specs = [row(D_MODEL), mod_spec, full(g_pre), full(w_in), full(g_bq), full(w_uq), full(g_bkv), full(w_ukv)]
    args = [x2, mod6, g_pre, w_in, g_bq, w_uq, g_bkv, w_ukv]
    if latent:
        tab_spec = pl.BlockSpec((tm, LANES), lambda i: (i % tiles_per_seq, 0))
        in_specs += [tab_spec] * 6
        args += list(rope_tabs)
    widths = [(D_MODEL, BF16), (512, BF16), (512, BF16), (512, BF16), (B_HEADS * LANES, BF16),
              (B_HEADS * LANES, BF16), (B_HEADS * B_V, BF16), (512, BF16), (LANES, BF16), (LANES, BF16),
              (MIX_W, F32)]
    if not latent:
        widths += [(512, F32), (512, F32), (B_KVLORA, F32), (B_ROPE, F32), (LANES, F32), (LANES, F32)]
    out_specs = [row(w) for w, _ in widths]
    out_shape = [jax.ShapeDtypeStruct((m, w), dt) for w, dt in widths]
    return pl.pallas_call(
        functools.partial(_premix_kernel, latent),
        grid=(m // tm,),
        in_specs=in_specs, out_specs=out_specs, out_shape=out_shape,
        compiler_params=_cparams(("arbitrary",)),
        name="premix_latent" if latent else "premix_context",
    )(*args)


def _kvprep_kernel(ckv_ref, kr_ref, wukv_ref, kb_ref, vb_ref):
    kvb = _dot(ckv_ref[...].astype(BF16), wukv_ref[...])
    kr_pad = kr_ref[...]
    for i in range(B_HEADS):
        sl = slice(i * LANES, (i + 1) * LANES)
        kb_ref[:, sl] = (kvb[:, sl] + kr_pad).astype(BF16)
    vb_ref[...] = kvb[:, B_HEADS * LANES:].astype(BF16)


def _kvprep(ckv2, kr_pad2, w_ukv):
    m = ckv2.shape[0]
    tm = _row_tile(m)
    return pl.pallas_call(
        _kvprep_kernel,
        grid=(m // tm,),
        in_specs=[pl.BlockSpec((tm, B_KVLORA), lambda i: (i, 0)),
                  pl.BlockSpec((tm, LANES), lambda i: (i, 0)),
                  _resident(w_ukv)],
        out_specs=[pl.BlockSpec((tm, B_HEADS * LANES), lambda i: (i, 0)),
                   pl.BlockSpec((tm, B_HEADS * B_V), lambda i: (i, 0))],
        out_shape=[jax.ShapeDtypeStruct((m, B_HEADS * LANES), BF16),
                   jax.ShapeDtypeStruct((m, B_HEADS * B_V), BF16)],
        compiler_params=_cparams(("arbitrary",)),
        name="kvprep_b",
    )(ckv2, kr_pad2, w_ukv)


def _scores(mp, ch):
    s = _dot_nt(mp[0], ch[0](mp[1]))
    return s if ch[2] is None else s + ch[2](mp[1])


def _lane_fold(x, op):
    parts = [x[:, c:c + LANES] for c in range(0, x.shape[-1], LANES)]
    return functools.reduce(op, parts)


def _attend_tiles(nt, tile_maps, chunks, s_scr, finish, pair_coef=None):
    nc = len(chunks)
    scr, m_scr = s_scr[:nc], s_scr[nc]
    assert len(s_scr) == nc + 1

    def score_chunk(mp, ci, part, store):
        s = _scores(mp, chunks[ci])
        if store:
            scr[ci][...] = s
        fold = _lane_fold(s, jnp.maximum)
        return s, (fold if part is None else jnp.maximum(part, fold))

    part = None
    for ci in range(nc):
        _, part = score_chunk(tile_maps(0)[0], ci, part, True)
    m_scr[...] = part

    def body(i, carry):
        maps = tile_maps(i)
        tail = tile_maps(jnp.minimum(i + 1, nt - 1))[0] if nt > 1 else None
        cur, cur_part = None, m_scr[...]
        outs = []
        for mi, (_, tag, sink) in enumerate(maps):
            m = jnp.max(cur_part, axis=-1, keepdims=True)
            if sink is not None:
                m = jnp.maximum(m, sink)
            nxt_map = maps[mi + 1] if mi + 1 < len(maps) else tail
            nxt, nxt_part, l, acc, probs = [], None, None, None, []
            for ci, ch in enumerate(chunks):
                if nxt_map is not None:
                    s, nxt_part = score_chunk(nxt_map, ci, nxt_part, mi + 1 == len(maps))
                    nxt.append(s)
                p = jnp.exp2((scr[ci][...] if cur is None else cur[ci]) - m)
                ls = _lane_fold(p, jnp.add)
                l = ls if l is None else l + ls
                if pair_coef is None:
                    pv = _dot(p.astype(BF16), ch[1](tag))
                    acc = pv if acc is None else acc + pv
                else:
                    probs.append(p.astype(BF16))
            denom = jnp.sum(l, axis=-1, keepdims=True)
            if sink is not None:
                denom = denom + jnp.exp2(sink - m)
            if pair_coef is None:
                outs.append(acc / denom)
            elif mi % 2 == 0:
                first_probs, first_denom = probs, denom
            else:
                wa, wb = (1.0 / first_denom).astype(BF16), (pair_coef / denom).astype(BF16)
                for ci, ch in enumerate(chunks):
                    pv = _dot(first_probs[ci] * wa - probs[ci] * wb, ch[1](tag))
                    acc = pv if acc is None else acc + pv
                outs.append(acc)
            if mi + 1 < len(maps):
                cur, cur_part = nxt, nxt_part
            elif tail is not None:
                m_scr[...] = nxt_part
        finish(i, outs)
        return carry

    lax.fori_loop(0, nt, body, 0)


def _score_buffers(kv, tq):
    sizes = []
    for a in kv[::2]:
        t = a.shape[1]
        assert t % KEY_CHUNK == 0 or t < KEY_CHUNK
        sizes += [min(KEY_CHUNK, t)] * (t // min(KEY_CHUNK, t))
    return [pltpu.VMEM((tq, size), F32) for size in sizes] + [pltpu.VMEM((tq, LANES), F32)]


def _ref_chunks(kv_refs):
    chunks = []
    for i in range(0, len(kv_refs), 2):
        k_ref, v_ref = kv_refs[i], kv_refs[i + 1]
        t = k_ref.shape[1]
        size = min(KEY_CHUNK, t)
        for c in range(0, t, size):
            chunks.append((lambda tag, r=k_ref, c=c, n=size: r[0, c:c + n, tag[0]],
                           lambda tag, r=v_ref, c=c, n=size: r[0, c:c + n, tag[1]], None))
    return chunks


def _attn_a_kernel(lam_init, nkv, tq, q_ref, *refs):
    kv_refs = refs[:2 * nkv]
    lq1_ref, lk1_ref, lq2_ref, lk2_ref, gsub_ref, o_ref = refs[2 * nkv:2 * nkv + 6]
    s_scr = refs[2 * nkv + 6:]
    lam = (jnp.exp(jnp.sum(lq1_ref[...] * lk1_ref[...], axis=-1, keepdims=True))
           - jnp.exp(jnp.sum(lq2_ref[...] * lk2_ref[...], axis=-1, keepdims=True)) + lam_init)
    lane = lax.broadcasted_iota(jnp.int32, (tq, LANES), 1)
    tile_rows = lambda i: pl.ds(pl.multiple_of(i * tq, tq), tq)

    def tile_maps(i):
        maps = []
        for h in range(A_HPS):
            sl = slice(h * LANES, (h + 1) * LANES)
            q = q_ref[0, tile_rows(i), sl]
            zero = jnp.zeros_like(q)
            maps.append((jnp.where(lane < A_DIM, q, zero), (sl, sl), None))
            maps.append((jnp.where(lane >= A_DIM, q, zero), (sl, sl), None))
        return maps

    def finish(i, outs):
        for h, o in enumerate(outs):
            o_ref[0, tile_rows(i), h * LANES:(h + 1) * LANES] = (
                _rms(o, gsub_ref[...]) * (1.0 - lam_init)).astype(BF16)

    _attend_tiles(q_ref.shape[1] // tq, tile_maps, _ref_chunks(kv_refs), s_scr, finish, pair_coef=lam)


def _attn_a(lam_init, q, kv, lq1, lk1, lq2, lk2, g_sub):
    bsz, n, _ = q.shape
    tq = min(Q_TILE, n)
    w = A_HPS * LANES
    small = lambda a: pl.BlockSpec(a.shape, lambda b, h: (0, 0))
    return pl.pallas_call(
        functools.partial(_attn_a_kernel, lam_init, len(kv) // 2, tq),
        grid=(bsz, A_HEADS // A_HPS),
        in_specs=[pl.BlockSpec((1, n, w), lambda b, h: (b, 0, h))]
        + [pl.BlockSpec((1, a.shape[1], w), lambda b, h: (b, 0, h)) for a in kv]
        + [small(lq1), small(lk1), small(lq2), small(lk2), small(g_sub)],
        out_specs=pl.BlockSpec((1, n, w), lambda b, h: (b, 0, h)),
        out_shape=jax.ShapeDtypeStruct((bsz, n, A_HEADS * LANES), BF16),
        scratch_shapes=_score_buffers(kv, tq),
        compiler_params=_cparams(("arbitrary", "arbitrary")),
        name="attn_a",
    )(q, *kv, lq1, lk1, lq2, lk2, g_sub)


def _attn_b_kernel(nkv, tq, q_ref, *refs):
    kv_refs, o_ref, s_scr = refs[:2 * nkv], refs[2 * nkv], refs[2 * nkv + 1:]
    lane = lax.broadcasted_iota(jnp.int32, (tq, LANES), 1)
    tile_rows = lambda i: pl.ds(pl.multiple_of(i * tq, tq), tq)

    def tile_maps(i):
        maps = []
        for h in range(B_HPS):
            sl = slice(h * LANES, (h + 1) * LANES)
            vsl = slice((h // 2) * LANES, (h // 2 + 1) * LANES)
            maps.append((q_ref[0, tile_rows(i), sl], (sl, vsl), None))
        return maps

    def finish(i, outs):
        for pair in range(B_HPS // 2):
            o_ref[0, tile_rows(i), pair * LANES:(pair + 1) * LANES] = (
                jnp.where(lane < B_V, outs[2 * pair], outs[2 * pair + 1]).astype(BF16))

    _attend_tiles(q_ref.shape[1] // tq, tile_maps, _ref_chunks(kv_refs), s_scr, finish)


def _attn_b(q, kv):
    bsz, n, _ = q.shape
    tq = min(Q_TILE, n)
    kv_specs = []
    for i, a in enumerate(kv):
        w = B_HPS * (LANES if i % 2 == 0 else B_V)
        kv_specs.append(pl.BlockSpec((1, a.shape[1], w), lambda b, h: (b, 0, h)))
    return pl.pallas_call(
        functools.partial(_attn_b_kernel, len(kv) // 2, tq),
        grid=(bsz, B_HEADS // B_HPS),
        in_specs=[pl.BlockSpec((1, n, B_HPS * LANES), lambda b, h: (b, 0, h))] + kv_specs,
        out_specs=pl.BlockSpec((1, n, B_HPS * B_V), lambda b, h: (b, 0, h)),
        out_shape=jax.ShapeDtypeStruct((bsz, n, B_HEADS * B_V), BF16),
        scratch_shapes=_score_buffers(kv, tq),
        compiler_params=_cparams(("arbitrary", "arbitrary")),
        name="attn_b",
    )(q, *kv)


def _attn_c_kernel(latent, nblk, *refs):
    if latent:
        q_ref, kx_ref, vx_ref, sink_ref, kw_ref, vw_ref, bias_ref, o_ref = refs[:8]
    else:
        q_ref, kx_ref, vx_ref, sink_ref, o_ref = refs[:5]
    s_scr = refs[8 if latent else 5:]
    lane = lax.broadcasted_iota(jnp.int32, (Q_BLOCK, LANES), 1)
    block_rows = lambda j: pl.ds(pl.multiple_of(j * Q_BLOCK, Q_BLOCK), Q_BLOCK)

    chunks = [(lambda j: kx_ref[0], lambda j: vx_ref[0], None)]
    if latent:
        win_rows = lambda j: pl.ds(pl.multiple_of(j * Q_BLOCK, Q_BLOCK), 3 * Q_BLOCK)

        def bias_of(j):
            first_i = (jnp.asarray(j) == 0).astype(jnp.int32)
            last_i = (jnp.asarray(j) == nblk - 1).astype(jnp.int32)
            return bias_ref[1 - first_i + last_i + 2 * first_i * last_i]

        chunks.append((lambda j: kw_ref[0, win_rows(j), :], lambda j: vw_ref[0, win_rows(j), :], bias_of))

    def tile_maps(j):
        maps = []
        for r in range(C_REP):
            qr = q_ref[0, block_rows(j), r * LANES:(r + 1) * LANES]
            halves = [jnp.where((lane >= g * C_DIM) & (lane < (g + 1) * C_DIM), qr, jnp.zeros_like(qr))
                      for g in range(C_KV_HEADS)]
            rows = slice(C_KV_HEADS * r * Q_BLOCK, C_KV_HEADS * (r + 1) * Q_BLOCK)
            maps.append((jnp.concatenate(halves, axis=0), j, sink_ref[rows, :] * LOG2E))
        return maps

    def finish(j, outs):
        for r, o in enumerate(outs):
            o_ref[0, block_rows(j), r * LANES:(r + 1) * LANES] = (
                jnp.where(lane < C_DIM, o[:Q_BLOCK], o[Q_BLOCK:]).astype(BF16))

    _attend_tiles(nblk, tile_maps, chunks, s_scr, finish)


def _attn_c(q, kx, vx, sink_rows, k_lat=None, v_lat=None):
    bsz, n, _ = q.shape
    tc = kx.shape[1]
    nblk = n // Q_BLOCK
    latent = k_lat is not None
    rows = C_KV_HEADS * Q_BLOCK
    whole = lambda a: pl.BlockSpec((1,) + a.shape[1:], lambda b: (b,) + (0,) * (a.ndim - 1))
    const = lambda a: pl.BlockSpec(a.shape, lambda b: (0,) * a.ndim)
    in_specs = [whole(q), whole(kx), whole(vx), const(sink_rows)]
    args = [q, kx, vx, sink_rows]
    scratch = [pltpu.VMEM((rows, tc), F32)]
    if latent:
        pad = ((0, 0), (Q_BLOCK, Q_BLOCK), (0, 0))
        kw, vw = jnp.pad(k_lat, pad), jnp.pad(v_lat, pad)
        row = (np.arange(rows) % Q_BLOCK)[:, None]
        col = np.arange(3 * Q_BLOCK)[None, :]
        in_prev, in_next = col < Q_BLOCK, col >= 2 * Q_BLOCK
        band = np.where(in_prev, col >= row, np.where(in_next, col - 2 * Q_BLOCK <= row, True))
        variants = [band & ~in_prev, band, band & ~in_next, band & ~in_prev & ~in_next]
        bias = np.stack([np.where(ok, 0.0, NEG_INF).astype(np.float32) for ok in variants], axis=0)
        in_specs += [whole(kw), whole(vw), const(bias)]
        args += [kw, vw, bias]
        scratch.append(pltpu.VMEM((rows, 3 * Q_BLOCK), F32))
    scratch.append(pltpu.VMEM((rows, LANES), F32))
    return pl.pallas_call(
        functools.partial(_attn_c_kernel, latent, nblk),
        grid=(bsz,),
        in_specs=in_specs,
        out_specs=whole(q),
        out_shape=jax.ShapeDtypeStruct((bsz, n, 512), BF16),
        scratch_shapes=scratch,
        compiler_params=_cparams(("arbitrary",)),
        name="attn_c_latent" if latent else "attn_c_context",
    )(*args)


def _s5_prepare_kernel(are_ref, aim_ref, ldt_ref, btre_ref, btim_ref, cre_ref, cim_ref,
                       arec_ref, aimc_ref, ctre_ref, ctim_ref,
                       td_ref, wd_ref, vd_ref, a16re_ref, a16im_ref):
    p, n = D_GROUP, D_STATE
    gl = pl.program_id(0) % S5_SG

    def one_hot(shape, target):
        r = lax.broadcasted_iota(jnp.int32, shape, 0)
        c = lax.broadcasted_iota(jnp.int32, shape, 1)
        return jnp.where(c == target(r), 1.0, 0.0).astype(BF16)

    place_p = one_hot((p, LANES), lambda r: gl * p + r)
    place_n = one_hot((n, S5_SLB * LANES), lambda r: gl * n + r)
    pbits = p.bit_length() - 1
    place_tp = one_hot((S5_CHUNK * p, S5_FLAT), lambda r: (r >> pbits) * LANES + gl * p + (r & (p - 1)))

    refs = (are_ref, aim_ref, ldt_ref, btre_ref, btim_ref, cre_ref, cim_ref, a16re_ref, a16im_ref)
    kt_fwd, w_fwd = _s5_prepare_direction(0, *refs)
    kt_bwd, w_bwd = _s5_prepare_direction(1, *refs)
    pieces = [kt_bwd[(S5_CHUNK - 1 - i) * p:(S5_CHUNK - i) * p] for i in range(S5_CHUNK - 1)]
    pieces.append(kt_fwd[0:p] + kt_bwd[0:p])
    pieces.append(kt_fwd[p:])
    lagk = _dot(jnp.concatenate(pieces, axis=0).astype(BF16), place_p)
    for t_in in range(S5_CHUNK):
        row = [lagk[(t_out - t_in + S5_CHUNK - 1) * p:(t_out - t_in + S5_CHUNK) * p] for t_out in range(S5_CHUNK)]
        td_ref[0, t_in, 0] = jnp.concatenate(row, axis=1).astype(BF16)

    w_sets = [w_fwd[0][::-1], w_fwd[1][::-1], w_bwd[0], w_bwd[1]]
    for x, rows in enumerate(w_sets):
        wexp = _dot(jnp.concatenate(rows, axis=0).astype(BF16), place_n)
        for t in range(S5_CHUNK):
            wd_ref[0, t, 0, :, x * S5_SLB * LANES:(x + 1) * S5_SLB * LANES] = wexp[t * p:(t + 1) * p].astype(BF16)

    tt = lax.broadcasted_iota(jnp.int32, (n, S5_CHUNK * p), 1) >> pbits
    for d in range(N_DIRS):
        power = (tt + 1 if d == 0 else S5_CHUNK - tt).astype(F32)
        dt = jnp.exp(ldt_ref[d, 0])
        mag = jnp.exp(power * (arec_ref[d, 0] * dt))
        ang = power * (aimc_ref[d, 0] * dt)
        pr, pi = mag * jnp.cos(ang), mag * jnp.sin(ang)
        ctre, ctim = ctre_ref[d, 0], ctim_ref[d, 0]
        vd_ref[0, 2 * d, 0] = _dot((ctre * pr - ctim * pi).astype(BF16), place_tp).astype(BF16)
        vd_ref[0, 2 * d + 1, 0] = _dot((-(ctre * pi + ctim * pr)).astype(BF16), place_tp).astype(BF16)


def _s5_prepare_direction(d, are_ref, aim_ref, ldt_ref, btre_ref, btim_ref, cre_ref, cim_ref, a16re_ref, a16im_ref):
    hi = lax.Precision.HIGHEST
    are = are_ref[d, 0]
    aim = aim_ref[d, 0]
    dt = jnp.exp(ldt_ref[d, 0])
    nj = S5_CHUNK + 1
    jj = lax.broadcasted_iota(jnp.int32, (nj, D_STATE), 0).astype(F32)
    mag = jnp.exp(jj * (are * dt))
    ang = jj * (aim * dt)
    pre = mag * jnp.cos(ang)
    pim = mag * jnp.sin(ang)
    xr = pre[1:2] - 1.0
    xi = pim[1:2]
    den = are * are + aim * aim
    fr = (xr * are + xi * aim) / den
    fi = (xi * are - xr * aim) / den
    btre = btre_ref[d, 0]
    btim = btim_ref[d, 0]
    bbre = fr * btre - fi * btim
    bbim = fr * btim + fi * btre
    cre = cre_ref[d, 0]
    cim = cim_ref[d, 0]
    wj_re, wj_im = [], []
    for j in range(S5_CHUNK):
        pr = pre[j:j + 1]
        pi = pim[j:j + 1]
        wj_re.append(pr * bbre - pi * bbim)
        wj_im.append(pr * bbim + pi * bbre)
    dn = (((1,), (1,)), ((), ()))
    a16re_ref[d, 0] = pre[S5_CHUNK:S5_CHUNK + 1]
    a16im_ref[d, 0] = pim[S5_CHUNK:S5_CHUNK + 1]
    kt = (lax.dot_general(jnp.concatenate(wj_re, axis=0), cre, dn, precision=hi, preferred_element_type=F32)
          - lax.dot_general(jnp.concatenate(wj_im, axis=0), cim, dn, precision=hi, preferred_element_type=F32))
    return kt, (wj_re, wj_im)


def _s5_prepare(a_re, a_im, log_dt, b_re, b_im, c_re, c_im):
    g, n, p = D_GROUPS, D_STATE, D_GROUP
    nsg = g // S5_SG
    v4 = lambda a: a.reshape(N_DIRS, g, 1, a.shape[-1])
    col = lambda a: a.reshape(N_DIRS, g, n, 1)
    spec = lambda *s: pl.BlockSpec((N_DIRS, 1) + s, lambda i: (0, i) + (0,) * len(s))
    grp = lambda *s: pl.BlockSpec((1, s[0], 1) + s[1:], lambda i: (i // S5_SG, 0, i % S5_SG, 0, 0))
    bt_re = jnp.swapaxes(b_re, -1, -2)
    bt_im = jnp.swapaxes(b_im, -1, -2)
    ct_re = jnp.tile(jnp.swapaxes(c_re, -1, -2), (1, 1, 1, S5_CHUNK))
    ct_im = jnp.tile(jnp.swapaxes(c_im, -1, -2), (1, 1, 1, S5_CHUNK))
    outs = pl.pallas_call(
        _s5_prepare_kernel,
        grid=(g,),
        in_specs=[spec(1, n), spec(1, n), spec(1, 1), spec(p, n), spec(p, n), spec(p, n), spec(p, n),
                  spec(n, 1), spec(n, 1), spec(n, S5_CHUNK * p), spec(n, S5_CHUNK * p)],
        out_specs=[grp(S5_CHUNK, p, S5_FLAT), grp(S5_CHUNK, p, 4 * S5_SLB * LANES), grp(4, n, S5_FLAT),
                   spec(1, n), spec(1, n)],
        out_shape=[jax.ShapeDtypeStruct((nsg, S5_CHUNK, S5_SG, p, S5_FLAT), BF16),
                   jax.ShapeDtypeStruct((nsg, S5_CHUNK, S5_SG, p, 4 * S5_SLB * LANES), BF16),
                   jax.ShapeDtypeStruct((nsg, 4, S5_SG, n, S5_FLAT), BF16)]
        + [jax.ShapeDtypeStruct((N_DIRS, g, 1, n), F32)] * 2,
        compiler_params=_cparams(("arbitrary",)),
        name="s5_prepare",
    )(v4(a_re), v4(a_im), log_dt.reshape(N_DIRS, g, 1, 1), bt_re, bt_im, c_re, c_im,
      col(a_re), col(a_im), ct_re, ct_im)
    td, wd, vd, a16re, a16im = outs
    t_sg = td.reshape(nsg, S5_FLAT, S5_FLAT)
    w_sg = wd.reshape(nsg, S5_FLAT, 4 * S5_SLB * LANES)
    v_sg = vd.reshape(nsg, 4 * S5_SG * n, S5_FLAT)
    a16 = [a16re[0].reshape(1, g * n), a16im[0].reshape(1, g * n),
           a16re[1].reshape(1, g * n), a16im[1].reshape(1, g * n)]
    return t_sg, w_sg, v_sg, a16


def _s5_gather_chunks(du_ref, u_scr, bsz, ck):
    for b in range(bsz):
        for t in range(S5_CHUNK):
            u_scr[b * ck:(b + 1) * ck, t * LANES:(t + 1) * LANES] = (
                du_ref[b, pl.ds(t, ck, stride=S5_CHUNK), :].astype(BF16))


def _s5_state_in_kernel(bsz, bpad, ck, du_ref, w_ref, o0_ref, o1_ref, o2_ref, o3_ref, u_scr):
    _s5_gather_chunks(du_ref, u_scr, bsz, ck)
    s = _dot(u_scr[...], w_ref[0])
    for x, o_ref in enumerate((o0_ref, o1_ref, o2_ref, o3_ref)):
        for j in range(S5_SLB):
            col = (x * S5_SLB + j) * LANES
            for b in range(bpad):
                if b < bsz:
                    o_ref[j, pl.ds(b, ck, stride=bpad), :] = s[b * ck:(b + 1) * ck, col:col + LANES]
                else:
                    o_ref[j, pl.ds(b, ck, stride=bpad), :] = jnp.zeros((ck, LANES), F32)


def _s5_state_in(du, w_sg, bpad, ck):
    bsz, n, _ = du.shape
    nchunk = n // S5_CHUNK
    nsg = w_sg.shape[0]
    return pl.pallas_call(
        functools.partial(_s5_state_in_kernel, bsz, bpad, ck),
        grid=(nsg, nchunk // ck),
        in_specs=[pl.BlockSpec((bsz, ck * S5_CHUNK, LANES), lambda s, r: (0, r, s)),
                  pl.BlockSpec((1, S5_FLAT, 4 * S5_SLB * LANES), lambda s, r: (s, 0, 0),
                               pipeline_mode=pl.Buffered(1))],
        out_specs=[pl.BlockSpec((S5_SLB, ck * bpad, LANES), lambda s, r: (s, r, 0))] * 4,
        out_shape=[jax.ShapeDtypeStruct((nsg * S5_SLB, nchunk * bpad, LANES), F32)] * 4,
        scratch_shapes=[pltpu.VMEM((bsz * ck, S5_FLAT), BF16)],
        compiler_params=_cparams(("arbitrary", "arbitrary")),
        name="s5_state_in",
    )(du, w_sg)


def _s5_scan_kernel(nchunk, bpad, s0re_ref, s0im_ref, s1re_ref, s1im_ref, a0re_ref, a0im_ref, a1re_ref, a1im_ref,
                    h0re_ref, h0im_ref, h1re_ref, h1im_ref,
                    p0re_ref, p0im_ref, p1re_ref, p1im_ref, f0re_ref, f0im_ref, f1re_ref, f1im_ref):
    a0re, a0im, a1re, a1im = a0re_ref[...], a0im_ref[...], a1re_ref[...], a1im_ref[...]

    def body(i, carry):
        r0, i0, r1, i1 = carry
        k = pl.ds(pl.multiple_of(i * bpad, bpad), bpad)
        kb = pl.ds(pl.multiple_of((nchunk - 1 - i) * bpad, bpad), bpad)
        p0re_ref[:, k, :] = r0
        p0im_ref[:, k, :] = i0
        p1re_ref[:, kb, :] = r1
        p1im_ref[:, kb, :] = i1
        n_r0 = a0re * r0 - a0im * i0 + s0re_ref[:, k, :]
        n_i0 = a0re * i0 + a0im * r0 + s0im_ref[:, k, :]
        n_r1 = a1re * r1 - a1im * i1 + s1re_ref[:, kb, :]
        n_i1 = a1re * i1 + a1im * r1 + s1im_ref[:, kb, :]
        return n_r0, n_i0, n_r1, n_i1

    r0, i0, r1, i1 = lax.fori_loop(0, nchunk, body,
                                   (h0re_ref[...], h0im_ref[...], h1re_ref[...], h1im_ref[...]))
    f0re_ref[...] = r0
    f0im_ref[...] = i0
    f1re_ref[...] = r1
    f1im_ref[...] = i1


def _s5_scan(s_in, a16, h0, nchunk, bpad):
    nlb = D_GROUPS * D_STATE // LANES
    lb = 2
    seq_spec = pl.BlockSpec((lb, nchunk * bpad, LANES), lambda i: (i, 0, 0))
    a_spec = pl.BlockSpec((lb, 1, LANES), lambda i: (i, 0, 0))
    h_spec = pl.BlockSpec((lb, bpad, LANES), lambda i: (i, 0, 0))
    outs = pl.pallas_call(
        functools.partial(_s5_scan_kernel, nchunk, bpad),
        grid=(nlb // lb,),
        in_specs=[seq_spec] * 4 + [a_spec] * 4 + [h_spec] * 4,
        out_specs=[seq_spec] * 4 + [h_spec] * 4,
        out_shape=[jax.ShapeDtypeStruct((nlb, nchunk * bpad, LANES), F32)] * 4
        + [jax.ShapeDtypeStruct((nlb, bpad, LANES), F32)] * 4,
        compiler_params=_cparams(("arbitrary",)),
        name="s5_scan",
    )(*s_in, *a16, *h0)
    return outs[:4], outs[4:]


def _s5_output_kernel(bsz, bpad, ck, du_ref, t_ref, v_ref, p0_ref, p1_ref, p2_ref, p3_ref, y_ref, u_scr, p_scr):
    _s5_gather_chunks(du_ref, u_scr, bsz, ck)
    for x, p_ref in enumerate((p0_ref, p1_ref, p2_ref, p3_ref)):
        for j in range(S5_SLB):
            col = (x * S5_SLB + j) * LANES
            for b in range(bsz):
                p_scr[b * ck:(b + 1) * ck, col:col + LANES] = (
                    p_ref[j, pl.ds(b, ck, stride=bpad), :].astype(BF16))
    y = _dot(u_scr[...], t_ref[0]) + _dot(p_scr[...], v_ref[0])
    for b in range(bsz):
        for t in range(S5_CHUNK):
            y_ref[b, pl.ds(t, ck, stride=S5_CHUNK), :] = y[b * ck:(b + 1) * ck, t * LANES:(t + 1) * LANES]


def _s5_output(du, t_sg, v_sg, p_states, bpad, ck):
    bsz, n, _ = du.shape
    nchunk = n // S5_CHUNK
    nsg = t_sg.shape[0]
    tok = pl.BlockSpec((bsz, ck * S5_CHUNK, LANES), lambda s, r: (0, r, s))
    mat = pl.BlockSpec((1, S5_FLAT, S5_FLAT), lambda s, r: (s, 0, 0), pipeline_mode=pl.Buffered(1))
    return pl.pallas_call(
        functools.partial(_s5_output_kernel, bsz, bpad, ck),
        grid=(nsg, nchunk // ck),
        in_specs=[tok, mat, mat] + [pl.BlockSpec((S5_SLB, ck * bpad, LANES), lambda s, r: (s, r, 0))] * 4,
        out_specs=tok,
        out_shape=jax.ShapeDtypeStruct((bsz, n, MIX_W), F32),
        scratch_shapes=[pltpu.VMEM((bsz * ck, S5_FLAT), BF16), pltpu.VMEM((bsz * ck, S5_FLAT), BF16)],
        compiler_params=_cparams(("arbitrary", "arbitrary")),
        name="s5_output",
    )(du, t_sg, v_sg, *p_states)


def _s5_mixer(du, mats, h0):
    t_sg, w_sg, v_sg, a16 = mats
    bsz, n, _ = du.shape
    nchunk = n // S5_CHUNK
    bpad = -(-bsz // 8) * 8
    ck = min(nchunk, max(S5_ROWS // bsz, 1))
    assert nchunk % ck == 0 and (ck * S5_CHUNK) % 8 == 0
    to_blocks = lambda h: h.reshape(h.shape[0], -1, LANES).transpose(1, 0, 2)
    h0 = [to_blocks(jnp.pad(h, ((0, bpad - bsz), (0, 0)))) for h in h0]
    a16 = [to_blocks(a) for a in a16]
    s_in = _s5_state_in(du, w_sg, bpad, ck)
    p_states, finals = _s5_scan(s_in, a16, h0, nchunk, bpad)
    y = _s5_output(du, t_sg, v_sg, p_states, bpad, ck)
    return y, [f.transpose(1, 0, 2).reshape(bpad, -1)[:bsz] for f in finals]


def _merge_kernel(x_ref, hb_ref, oa_ref, ob_ref, oc_ref, y_ref, u_ref, mod_ref, dskip_ref, wglu_ref,
                  wgate_ref, wbr_ref, wo_ref, gpost_ref, xo_ref):
    yv = y_ref[...] + dskip_ref[...] * u_ref[...]
    yg = jax.nn.gelu(yv)
    od = yg * jax.nn.sigmoid(_dot(yg.astype(BF16), wglu_ref[...]))
    hb = hb_ref[...]
    branches = (oa_ref[...], ob_ref[...], oc_ref[...], od.astype(BF16))
    merged = None
    for k, o in enumerate(branches):
        gate = jax.nn.sigmoid(_dot(hb, wgate_ref[:, k * D_MODEL:(k + 1) * D_MODEL]))
        term = gate * _dot(o, wbr_ref[k])
        merged = term if merged is None else merged + term
    z = _dot(merged.astype(BF16), wo_ref[...])
    gate_m = mod_ref[0, 2:3, :]
    xo_ref[...] = x_ref[...] + gate_m * _rms(z, gpost_ref[...])


def _merge(x2, hb, oa, ob, oc, y, du, mod6, d_skip, w_glu, w_gate, w_branch, w_o, g_post, seq):
    m = x2.shape[0]
    tm = _row_tile(m)
    tiles_per_seq = max(seq // tm, 1)
    assert mod6.shape[0] == 1 or seq % tm == 0
    row = lambda w: pl.BlockSpec((tm, w), lambda i: (i, 0))
    full = _resident
    if mod6.shape[0] == 1:
        mod_spec = pl.BlockSpec((1, 6, D_MODEL), lambda i: (0, 0, 0))
    else:
        mod_spec = pl.BlockSpec((1, 6, D_MODEL), lambda i: (i // tiles_per_seq, 0, 0))
    return pl.pallas_call(
        _merge_kernel,
        grid=(m // tm,),
        in_specs=[row(D_MODEL), row(D_MODEL), row(MIX_W), row(MIX_W), row(MIX_W), row(MIX_W), row(MIX_W),
                  mod_spec, full(d_skip), full(w_glu), full(w_gate), full(w_branch), full(w_o), full(g_post)],
        out_specs=row(D_MODEL),
        out_shape=jax.ShapeDtypeStruct((m, D_MODEL), F32),
        compiler_params=_cparams(("arbitrary",)),
        name="merge",
    )(x2, hb, oa, ob, oc, y, du, mod6, d_skip, w_glu, w_gate, w_branch, w_o, g_post)


def _ffn_kernel(x_ref, mod_ref, gpre_ref, w1_ref, w3_ref, w2_ref, gpost_ref, xo_ref):
    x = x_ref[...]
    shift = mod_ref[0, 3:4, :]
    scale = mod_ref[0, 4:5, :]
    gate = mod_ref[0, 5:6, :]
    h2 = (_rms(x, gpre_ref[...]) * (1.0 + scale) + shift).astype(BF16)
    a = _dot(h2, w1_ref[...])
    b = _dot(h2, w3_ref[...])
    f = _dot((jax.nn.silu(a) * b).astype(BF16), w2_ref[...])
    xo_ref[...] = x + gate * _rms(f, gpost_ref[...])


def _ffn(x2, mod6, g_pre, w1, w3, w2, g_post, seq):
    m = x2.shape[0]
    tm = _row_tile(m)
    tiles_per_seq = max(seq // tm, 1)
    assert mod6.shape[0] == 1 or seq % tm == 0
    row = lambda w: pl.BlockSpec((tm, w), lambda i: (i, 0))
    full = _resident
    if mod6.shape[0] == 1:
        mod_spec = pl.BlockSpec((1, 6, D_MODEL), lambda i: (0, 0, 0))
    else:
        mod_spec = pl.BlockSpec((1, 6, D_MODEL), lambda i: (i // tiles_per_seq, 0, 0))
    return pl.pallas_call(
        _ffn_kernel,
        grid=(m // tm,),
        in_specs=[row(D_MODEL), mod_spec, full(g_pre), full(w1), full(w3), full(w2), full(g_post)],
        out_specs=row(D_MODEL),
        out_shape=jax.ShapeDtypeStruct((m, D_MODEL), F32),
        compiler_params=_cparams(("arbitrary",)),
        name="ffn",
    )(x2, mod6, g_pre, w1, w3, w2, g_post)


def _rope_tables(n, rot_dim, lane_off):
    f32 = np.float32
    rows = n // GRID_W
    pos_row = np.repeat(np.arange(rows, dtype=f32), GRID_W)
    pos_col = np.tile(np.arange(GRID_W, dtype=f32), rows)
    n_freq = rot_dim // 4
    inv_freq = (ROPE_BASE ** (-np.arange(n_freq, dtype=f32) / f32(n_freq))).astype(f32)
    ang = np.concatenate([pos_row[:, None] * inv_freq, pos_col[:, None] * inv_freq], axis=-1)
    cos, sin = np.cos(ang).astype(f32), np.sin(ang).astype(f32)
    zero = np.zeros_like(sin)
    c = np.concatenate([cos, cos], axis=-1)
    sa = np.concatenate([-sin, zero], axis=-1)
    sb = np.concatenate([zero, sin], axis=-1)
    if rot_dim == LANES // 2 and lane_off == 0:
        return tuple(np.tile(t, (1, 2)) for t in (c, sa, sb))
    pad = lambda t, fill: np.concatenate(
        [np.full((n, lane_off), fill, f32), t, np.full((n, LANES - lane_off - rot_dim), fill, f32)], axis=-1)
    return pad(c, 1.0), pad(sa, 0.0), pad(sb, 0.0)


def _layer_weights(l, w_in, w_gate, w_b_uq, w_b_ukv, sink_c, w_glu, w_branch, w_o, w_ff1, w_ff3, w_ff2):
    wi = w_in[l]
    aq, ak, av = wi[:, 0:512], wi[:, 512:1024], wi[:, 1024:1536]
    bcq, bckv, bkr = wi[:, 1536:1792], wi[:, 1792:1920], wi[:, 1920:1952]
    cq, ck, cv, du = wi[:, 1952:2464], wi[:, 2464:2592], wi[:, 2592:2720], wi[:, 2720:3232]
    cq_perm = cq.reshape(D_MODEL, C_KV_HEADS, C_REP, C_DIM).transpose(0, 2, 1, 3).reshape(D_MODEL, 512)
    zeros = lambda w: jnp.zeros((D_MODEL, w), F32)
    bkr_pad = jnp.concatenate([zeros(B_KR_LANE), bkr, zeros(LANES - B_KR_LANE - B_ROPE)], axis=1)
    w_in_r = jnp.concatenate([aq, ak, av, cq_perm, ck, cv, du, bcq, bckv, bkr_pad], axis=1).astype(BF16)

    uq = w_b_uq[l].reshape(B_QLORA, B_HEADS, B_NOPE + B_ROPE)
    uq = jnp.concatenate([uq, jnp.zeros((B_QLORA, B_HEADS, LANES - B_NOPE - B_ROPE), F32)], axis=-1)
    w_uq_r = uq.reshape(B_QLORA, B_HEADS * LANES).astype(BF16)
    ukv = w_b_ukv[l].reshape(B_KVLORA, B_HEADS, B_NOPE + B_V)
    kn = jnp.concatenate([ukv[..., :B_NOPE], jnp.zeros((B_KVLORA, B_HEADS, LANES - B_NOPE), F32)], axis=-1)
    w_ukv_r = jnp.concatenate([kn.reshape(B_KVLORA, B_HEADS * LANES),
                               ukv[..., B_NOPE:].reshape(B_KVLORA, B_HEADS * B_V)], axis=1).astype(BF16)

    sink_perm = sink_c[l].reshape(C_KV_HEADS, C_REP).T.reshape(C_HEADS)
    sink_rows = jnp.repeat(sink_perm, Q_BLOCK)[:, None]

    wbr = w_branch[l]
    wbr_c = wbr[2].reshape(C_KV_HEADS, C_REP, C_DIM, D_MODEL).transpose(1, 0, 2, 3).reshape(MIX_W, D_MODEL)
    w_branch_r = jnp.stack([wbr[0], wbr[1], wbr_c, wbr[3]], axis=0).astype(BF16)
    return dict(w_in=w_in_r, w_uq=w_uq_r, w_ukv=w_ukv_r, sink_rows=sink_rows,
                w_gate=w_gate[l].astype(BF16), w_glu=w_glu[l].astype(BF16), w_branch=w_branch_r,
                w_o=w_o[l].astype(BF16), w_ff1=w_ff1[l].astype(BF16), w_ff3=w_ff3[l].astype(BF16),
                w_ff2=w_ff2[l].astype(BF16))


def _trunk_layer(l, x, mod6, lw, sp, s5_mats, rope_tabs, ctx):
    latent = ctx is not None
    bsz, n, _ = x.shape
    m = bsz * n
    x2 = x.reshape(m, D_MODEL)
    row1 = lambda v: v.reshape(1, -1)

    outs = _premix(latent, x2, mod6, row1(sp['g_pre_mix']), lw['w_in'], row1(sp['g_b_q']), lw['w_uq'],
                   row1(sp['g_b_kv']), lw['w_ukv'], rope_tabs, n)
    hb, qa, ka, va, qb, kb, vb, qc, kc, vc, du = outs[:11]
    r3 = lambda t: t.reshape(bsz, n, t.shape[-1])
    qa, ka, va, qb, kb, vb, qc, kc, vc, du = map(r3, (qa, ka, va, qb, kb, vb, qc, kc, vc, du))

    lam_init = 0.8 - 0.6 * math.exp(-0.3 * l)
    lam_args = (row1(sp['lam_q1']), row1(sp['lam_k1']), row1(sp['lam_q2']), row1(sp['lam_k2']), row1(sp['g_a_sub']))
    if latent:
        past = ctx['a_k'].shape[1]
        ka_ctx = ctx['a_k'].reshape(bsz, past, 512).astype(BF16)
        va_ctx = ctx['a_v'].reshape(bsz, past, 512).astype(BF16)
        kr_pad = jnp.pad(ctx['b_kr'].reshape(bsz * past, B_ROPE),
                         ((0, 0), (B_KR_LANE, LANES - B_KR_LANE - B_ROPE)))
        kb_ctx, vb_ctx = _kvprep(ctx['b_ckv'].reshape(bsz * past, B_KVLORA), kr_pad, lw['w_ukv'])
        o_a = _attn_a(lam_init, qa, [ka, va, ka_ctx, va_ctx], *lam_args)
        o_b = _attn_b(qb, [kb, vb, kb_ctx.reshape(bsz, past, -1), vb_ctx.reshape(bsz, past, -1)])
        o_c = _attn_c(qc, ctx['c_k'].reshape(bsz, past, LANES).astype(BF16),
                      ctx['c_v'].reshape(bsz, past, LANES).astype(BF16), lw['sink_rows'], kc, vc)
        gn = D_GROUPS * D_STATE
        h0 = [ctx['d_re'][:, 0].reshape(bsz, gn), ctx['d_im'][:, 0].reshape(bsz, gn),
              ctx['d_re'][:, 1].reshape(bsz, gn), ctx['d_im'][:, 1].reshape(bsz, gn)]
    else:
        o_a = _attn_a(lam_init, qa, [ka, va], *lam_args)
        o_b = _attn_b(qb, [kb, vb])
        o_c = _attn_c(qc, kc, vc, lw['sink_rows'])
        h0 = [jnp.zeros((bsz, D_GROUPS * D_STATE), F32)] * 4
    y_s5, finals = _s5_mixer(du, s5_mats, h0)

    f2 = lambda t: t.reshape(m, t.shape[-1])
    x2 = _merge(x2, hb, f2(o_a), f2(o_b), f2(o_c), f2(y_s5), f2(du), mod6, row1(sp['ssm_d']), lw['w_glu'],
                lw['w_gate'], lw['w_branch'], lw['w_o'], row1(sp['g_post_mix']), n)
    x2 = _ffn(x2, mod6, row1(sp['g_pre_ffn']), lw['w_ff1'], lw['w_ff3'], lw['w_ff2'], row1(sp['g_post_ffn']), n)
    x = x2.reshape(bsz, n, D_MODEL)
    if latent:
        return x, None
    akf, avf, ckvf, krf, ckf, cvf = outs[11:]
    st = lambda t: t.reshape(bsz, D_GROUPS, D_STATE)
    new_ctx = {'a_k': akf.reshape(bsz, n, A_HEADS, 2 * A_DIM), 'a_v': avf.reshape(bsz, n, A_HEADS, 2 * A_DIM),
               'b_ckv': ckvf.reshape(bsz, n, B_KVLORA), 'b_kr': krf.reshape(bsz, n, B_ROPE),
               'c_k': ckf.reshape(bsz, n, C_KV_HEADS, C_DIM), 'c_v': cvf.reshape(bsz, n, C_KV_HEADS, C_DIM),
               'd_re': jnp.stack([st(finals[0]), st(finals[2])], axis=1),
               'd_im': jnp.stack([st(finals[1]), st(finals[3])], axis=1)}
    return x, new_ctx


def kernel(x_prompt, x_sample, cache_a_k, cache_a_v, cache_b_ckv, cache_b_kr, cache_c_k, cache_c_v, state_d_re, state_d_im, c, c_ctx, w_mod, b_mod, g_pre_mix, g_post_mix, g_pre_ffn, g_post_ffn, w_in, w_gate, lam_q1, lam_k1, lam_q2, lam_k2, g_a_sub, g_b_q, g_b_kv, w_b_uq, w_b_ukv, sink_c, ssm_a_re, ssm_a_im, ssm_log_dt, ssm_b_re, ssm_b_im, ssm_c_re, ssm_c_im, ssm_d, w_glu, w_branch, w_o, w_ff1, w_ff3, w_ff2):
    dec_b, dec_n, _ = x_sample.shape
    assert dec_b + 1 <= 8

    cond = jnp.concatenate([c_ctx[None, :], c, jnp.zeros((8 - 1 - dec_b, D_MODEL), F32)], axis=0)
    mod = _modulation(cond, w_mod, b_mod)

    small = dict(g_pre_mix=g_pre_mix, g_post_mix=g_post_mix, g_pre_ffn=g_pre_ffn, g_post_ffn=g_post_ffn,
                 lam_q1=lam_q1, lam_k1=lam_k1, lam_q2=lam_q2, lam_k2=lam_k2, g_a_sub=g_a_sub, g_b_q=g_b_q,
                 g_b_kv=g_b_kv, ssm_d=ssm_d)
    rope_tabs = _rope_tables(dec_n, A_DIM, 0) + _rope_tables(dec_n, B_ROPE, B_KR_LANE)

    layers = []
    for l in range(DEPTH):
        lw = _layer_weights(l, w_in, w_gate, w_b_uq, w_b_ukv, sink_c, w_glu, w_branch, w_o, w_ff1, w_ff3, w_ff2)
        sp = {k: v[l] for k, v in small.items()}
        s5_mats = _s5_prepare(ssm_a_re[l], ssm_a_im[l], ssm_log_dt[l], ssm_b_re[l], ssm_b_im[l],
                              ssm_c_re[l], ssm_c_im[l])
        mod_l = mod[l].reshape(8, 6, D_MODEL)
        layers.append((lw, sp, s5_mats, mod_l))

    y_prompt = x_prompt
    ctx_out = []
    for l, (lw, sp, s5_mats, mod_l) in enumerate(layers):
        y_prompt, new_ctx = _trunk_layer(l, y_prompt, mod_l[0:1], lw, sp, s5_mats, None, None)
        ctx_out.append(new_ctx)

    y_sample = x_sample
    for l, (lw, sp, s5_mats, mod_l) in enumerate(layers):
        cached = {'a_k': cache_a_k[:, l], 'a_v': cache_a_v[:, l], 'b_ckv': cache_b_ckv[:, l],
                  'b_kr': cache_b_kr[:, l], 'c_k': cache_c_k[:, l], 'c_v': cache_c_v[:, l],
                  'd_re': state_d_re[:, l], 'd_im': state_d_im[:, l]}
        y_sample, _ = _trunk_layer(l, y_sample, mod_l[1:1 + dec_b], lw, sp, s5_mats, rope_tabs, cached)

    stack = lambda name: jnp.stack([cx[name] for cx in ctx_out], axis=1)
    return (y_prompt, y_sample, stack('a_k'), stack('a_v'), stack('b_ckv'), stack('b_kr'),
            stack('c_k'), stack('c_v'), stack('d_re'), stack('d_im'))
```

```python
import functools
import math

import jax
import jax.numpy as jnp
import numpy as np
from jax import lax
from jax.experimental import pallas as pl
from jax.experimental.pallas import tpu as pltpu

F32 = jnp.float32
BF16 = jnp.bfloat16

D_MODEL = 1024
DEPTH = 2
GRID_W = 64
Q_BLOCK = 128
ROPE_BASE = 10000.0
RMS_EPS = 1e-6
NEG_INF = -1e30
LOG2E = math.log2(math.e)
MIX_W = D_MODEL // 2
N_BRANCH = 4
A_HEADS = 4
A_DIM = 64
B_HEADS = 8
B_NOPE = 64
B_ROPE = 32
B_V = 64
B_QLORA = 256
B_KVLORA = 128
C_HEADS = 8
C_KV_HEADS = 2
C_REP = C_HEADS // C_KV_HEADS
C_DIM = 64
D_GROUP = 16
D_GROUPS = MIX_W // D_GROUP
D_STATE = 64
N_DIRS = 2
FF_HIDDEN = ((8 * D_MODEL // 3 + 255) // 256) * 256

LANES = 128
S5_CHUNK = 16
S5_SG = LANES // D_GROUP
S5_FLAT = S5_CHUNK * LANES
S5_SLB = S5_SG * D_STATE // LANES
S5_ROWS = 256
VMEM_LIMIT = 56 * 1024 * 1024
ROW_TILE = 512
Q_TILE = 256
KEY_CHUNK = 512
A_HPS = 2
B_HPS = 4

COL_AQ, COL_AK, COL_AV = 0, 512, 1024
COL_CQ, COL_CK, COL_CV = 1536, 2048, 2176
COL_DU = 2304
COL_BCQ, COL_BCKV, COL_BKR = 2816, 3072, 3200
IN_COLS = 3328
B_KR_LANE = B_NOPE


def _cparams(sem):
    return pltpu.CompilerParams(dimension_semantics=sem, vmem_limit_bytes=VMEM_LIMIT)


def _resident(a):
    return pl.BlockSpec(a.shape, lambda i: (0,) * a.ndim, pipeline_mode=pl.Buffered(1))


def _row_tile(m):
    return min(ROW_TILE, m)


def _dot(a, b):
    return jnp.dot(a, b, preferred_element_type=F32)


def _dot_nt(a, b):
    return lax.dot_general(a, b, (((1,), (1,)), ((), ())), preferred_element_type=F32)


def _rms(x, g):
    return x * lax.rsqrt(jnp.mean(x * x, axis=-1, keepdims=True) + RMS_EPS) * g


def _rope(x, c, sa, sb, half):
    w = x.shape[-1]
    return x * c + pltpu.roll(x, w - half, 1) * sa + pltpu.roll(x, half, 1) * sb


def _mod_kernel(c_ref, w_ref, b_ref, o_ref):
    c = c_ref[...]
    o_ref[0] = _dot(jax.nn.silu(c).astype(BF16), w_ref[0].astype(BF16)) + b_ref[0]


def _modulation(cond, w_mod, b_mod):
    nblk = 6
    return pl.pallas_call(
        _mod_kernel,
        grid=(DEPTH, nblk),
        in_specs=[pl.BlockSpec((8, D_MODEL), lambda l, j: (0, 0)),
                  pl.BlockSpec((1, D_MODEL, D_MODEL), lambda l, j: (l, 0, j)),
                  pl.BlockSpec((1, 1, D_MODEL), lambda l, j: (l, 0, j))],
        out_specs=pl.BlockSpec((1, 8, D_MODEL), lambda l, j: (l, 0, j)),
        out_shape=jax.ShapeDtypeStruct((DEPTH, 8, 6 * D_MODEL), F32),
        compiler_params=_cparams(("arbitrary", "arbitrary")),
        name="modulation",
    )(cond, w_mod, b_mod.reshape(DEPTH, 1, 6 * D_MODEL))


def _premix_kernel(latent, *refs):
    if latent:
        (x_ref, mod_ref, g_ref, win_ref, gbq_ref, wuq_ref, gbkv_ref, wukv_ref,
         ca_ref, saa_ref, sba_ref, cb_ref, sab_ref, sbb_ref,
         hb_ref, qa_ref, ka_ref, va_ref, qb_ref, kb_ref, vb_ref, qc_ref, kc_ref, vc_ref, du_ref) = refs
    else:
        (x_ref, mod_ref, g_ref, win_ref, gbq_ref, wuq_ref, gbkv_ref, wukv_ref,
         hb_ref, qa_ref, ka_ref, va_ref, qb_ref, kb_ref, vb_ref, qc_ref, kc_ref, vc_ref, du_ref,
         akf_ref, avf_ref, ckvf_ref, krf_ref, ckf_ref, cvf_ref) = refs

    x = x_ref[...]
    shift = mod_ref[0, 0:1, :]
    scale = mod_ref[0, 1:2, :]
    h = _rms(x, g_ref[...]) * (1.0 + scale) + shift
    hb = h.astype(BF16)
    hb_ref[...] = hb
    proj = _dot(hb, win_ref[...])

    if latent:
        ca, saa, sba = ca_ref[...], saa_ref[...], sba_ref[...]
        cb, sab, sbb = cb_ref[...], sab_ref[...], sbb_ref[...]
        rope_a = lambda t: _rope(t, ca, saa, sba, A_DIM // 2)
        rope_b = lambda t: _rope(t, cb, sab, sbb, B_ROPE // 2)
    else:
        rope_a = rope_b = lambda t: t

    def blk(col, i):
        return proj[:, col + i * LANES: col + (i + 1) * LANES]

    a_scale = A_DIM ** -0.5 * LOG2E
    for i in range(A_HEADS):
        sl = slice(i * LANES, (i + 1) * LANES)
        qa_ref[:, sl] = (rope_a(blk(COL_AQ, i)) * a_scale).astype(BF16)
        ka_ref[:, sl] = rope_a(blk(COL_AK, i)).astype(BF16)
    va_ref[...] = proj[:, COL_AV:COL_AV + 512].astype(BF16)

    c_scale = C_DIM ** -0.5 * LOG2E
    for i in range(C_REP):
        sl = slice(i * LANES, (i + 1) * LANES)
        qc_ref[:, sl] = (rope_a(blk(COL_CQ, i)) * c_scale).astype(BF16)
    kc_ref[...] = rope_a(blk(COL_CK, 0)).astype(BF16)
    vc_ref[...] = blk(COL_CV, 0).astype(BF16)

    du_ref[...] = proj[:, COL_DU:COL_DU + MIX_W]

    b_scale = (B_NOPE + B_ROPE) ** -0.5 * LOG2E
    cqn = _rms(proj[:, COL_BCQ:COL_BCQ + B_QLORA], gbq_ref[...])
    qb = _dot(cqn.astype(BF16), wuq_ref[...])
    ckv = _rms(proj[:, COL_BCKV:COL_BCKV + B_KVLORA], gbkv_ref[...])
    kvb = _dot(ckv.astype(BF16), wukv_ref[...])
    kr_pad = rope_b(blk(COL_BKR, 0))
    for i in range(B_HEADS):
        sl = slice(i * LANES, (i + 1) * LANES)
        qb_ref[:, sl] = (rope_b(qb[:, sl]) * b_scale).astype(BF16)
        kb_ref[:, sl] = (kvb[:, sl] + kr_pad).astype(BF16)
    vb_ref[...] = kvb[:, B_HEADS * LANES:].astype(BF16)

    if not latent:
        akf_ref[...] = proj[:, COL_AK:COL_AK + 512]
        avf_ref[...] = proj[:, COL_AV:COL_AV + 512]
        ckvf_ref[...] = ckv
        krf_ref[...] = kr_pad[:, B_KR_LANE:B_KR_LANE + B_ROPE]
        ckf_ref[...] = blk(COL_CK, 0)
        cvf_ref[...] = blk(COL_CV, 0)


def _premix(latent, x2, mod6, g_pre, w_in, g_bq, w_uq, g_bkv, w_ukv, rope_tabs, seq):
    m = x2.shape[0]
    tm = _row_tile(m)
    tiles_per_seq = max(seq // tm, 1)
    nb_mod = mod6.shape[0]
    assert nb_mod == 1 or seq % tm == 0
    row = lambda w: pl.BlockSpec((tm, w), lambda i: (i, 0))
    full = _resident
    if nb_mod == 1:
        mod_spec = pl.BlockSpec((1, 6, D_MODEL), lambda i: (0, 0, 0))
    else:
        mod_spec = pl.BlockSpec((1, 6, D_MODEL), lambda i: (i // tiles_per_seq, 0, 0))
    in_specs = [row(D_MODEL), mod_spec, full(g_pre), full(w_in), full(g_bq), full(w_uq), full(g_bkv), full(w_ukv)]
    args = [x2, mod6, g_pre, w_in, g_bq, w_uq, g_bkv, w_ukv]
    if latent:
        tab_spec = pl.BlockSpec((tm, LANES), lambda i: (i % tiles_per_seq, 0))
        in_specs += [tab_spec] * 6
        args += list(rope_tabs)
    widths = [(D_MODEL, BF16), (512, BF16), (512, BF16), (512, BF16), (B_HEADS * LANES, BF16),
              (B_HEADS * LANES, BF16), (B_HEADS * B_V, BF16), (512, BF16), (LANES, BF16), (LANES, BF16),
              (MIX_W, F32)]
    if not latent:
        widths += [(512, F32), (512, F32), (B_KVLORA, F32), (B_ROPE, F32), (LANES, F32), (LANES, F32)]
    out_specs = [row(w) for w, _ in widths]
    out_shape = [jax.ShapeDtypeStruct((m, w), dt) for w, dt in widths]
    return pl.pallas_call(
        functools.partial(_premix_kernel, latent),
        grid=(m // tm,),
        in_specs=in_specs, out_specs=out_specs, out_shape=out_shape,
        compiler_params=_cparams(("arbitrary",)),
        name="premix_latent" if latent else "premix_context",
    )(*args)


def _kvprep_kernel(ckv_ref, kr_ref, wukv_ref, kb_ref, vb_ref):
    kvb = _dot(ckv_ref[...].astype(BF16), wukv_ref[...])
    kr_pad = kr_ref[...]
    for i in range(B_HEADS):
        sl = slice(i * LANES, (i + 1) * LANES)
        kb_ref[:, sl] = (kvb[:, sl] + kr_pad).astype(BF16)
    vb_ref[...] = kvb[:, B_HEADS * LANES:].astype(BF16)


def _kvprep(ckv2, kr_pad2, w_ukv):
    m = ckv2.shape[0]
    tm = _row_tile(m)
    return pl.pallas_call(
        _kvprep_kernel,
        grid=(m // tm,),
        in_specs=[pl.BlockSpec((tm, B_KVLORA), lambda i: (i, 0)),
                  pl.BlockSpec((tm, LANES), lambda i: (i, 0)),
                  _resident(w_ukv)],
        out_specs=[pl.BlockSpec((tm, B_HEADS * LANES), lambda i: (i, 0)),
                   pl.BlockSpec((tm, B_HEADS * B_V), lambda i: (i, 0))],
        out_shape=[jax.ShapeDtypeStruct((m, B_HEADS * LANES), BF16),
                   jax.ShapeDtypeStruct((m, B_HEADS * B_V), BF16)],
        compiler_params=_cparams(("arbitrary",)),
        name="kvprep_b",
    )(ckv2, kr_pad2, w_ukv)


def _scores(mp, ch):
    s = _dot_nt(mp[0], ch[0](mp[1]))
    return s if ch[2] is None else s + ch[2](mp[1])


def _lane_fold(x, op):
    parts = [x[:, c:c + LANES] for c in range(0, x.shape[-1], LANES)]
    return functools.reduce(op, parts)


def _attend_tiles(nt, tile_maps, chunks, s_scr, finish, pair_coef=None):
    nc = len(chunks)
    scr, m_scr = s_scr[:nc], s_scr[nc]
    assert len(s_scr) == nc + 1

    def score_chunk(mp, ci, part, store):
        s = _scores(mp, chunks[ci])
        if store:
            scr[ci][...] = s
        fold = _lane_fold(s, jnp.maximum)
        return s, (fold if part is None else jnp.maximum(part, fold))

    part = None
    for ci in range(nc):
        _, part = score_chunk(tile_maps(0)[0], ci, part, True)
    m_scr[...] = part

    def body(i, carry):
        maps = tile_maps(i)
        tail = tile_maps(jnp.minimum(i + 1, nt - 1))[0] if nt > 1 else None
        cur, cur_part = None, m_scr[...]
        outs = []
        for mi, (_, tag, sink) in enumerate(maps):
            m = jnp.max(cur_part, axis=-1, keepdims=True)
            if sink is not None:
                m = jnp.maximum(m, sink)
            nxt_map = maps[mi + 1] if mi + 1 < len(maps) else tail
            nxt, nxt_part, l, acc, probs = [], None, None, None, []
            for ci, ch in enumerate(chunks):
                if nxt_map is not None:
                    s, nxt_part = score_chunk(nxt_map, ci, nxt_part, mi + 1 == len(maps))
                    nxt.append(s)
                p = jnp.exp2((scr[ci][...] if cur is None else cur[ci]) - m)
                ls = _lane_fold(p, jnp.add)
                l = ls if l is None else l + ls
                if pair_coef is None:
                    pv = _dot(p.astype(BF16), ch[1](tag))
                    acc = pv if acc is None else acc + pv
                else:
                    probs.append(p.astype(BF16))
            denom = jnp.sum(l, axis=-1, keepdims=True)
            if sink is not None:
                denom = denom + jnp.exp2(sink - m)
            if pair_coef is None:
                outs.append(acc / denom)
            elif mi % 2 == 0:
                first_probs, first_denom = probs, denom
            else:
                wa, wb = (1.0 / first_denom).astype(BF16), (pair_coef / denom).astype(BF16)
                for ci, ch in enumerate(chunks):
                    pv = _dot(first_probs[ci] * wa - probs[ci] * wb, ch[1](tag))
                    acc = pv if acc is None else acc + pv
                outs.append(acc)
            if mi + 1 < len(maps):
                cur, cur_part = nxt, nxt_part
            elif tail is not None:
                m_scr[...] = nxt_part
        finish(i, outs)
        return carry

    lax.fori_loop(0, nt, body, 0)


def _score_buffers(kv, tq):
    sizes = []
    for a in kv[::2]:
        t = a.shape[1]
        assert t % KEY_CHUNK == 0 or t < KEY_CHUNK
        sizes += [min(KEY_CHUNK, t)] * (t // min(KEY_CHUNK, t))
    return [pltpu.VMEM((tq, size), F32) for size in sizes] + [pltpu.VMEM((tq, LANES), F32)]


def _ref_chunks(kv_refs):
    chunks = []
    for i in range(0, len(kv_refs), 2):
        k_ref, v_ref = kv_refs[i], kv_refs[i + 1]
        t = k_ref.shape[1]
        size = min(KEY_CHUNK, t)
        for c in range(0, t, size):
            chunks.append((lambda tag, r=k_ref, c=c, n=size: r[0, c:c + n, tag[0]],
                           lambda tag, r=v_ref, c=c, n=size: r[0, c:c + n, tag[1]], None))
    return chunks


def _attn_a_kernel(lam_init, nkv, tq, q_ref, *refs):
    kv_refs = refs[:2 * nkv]
    lq1_ref, lk1_ref, lq2_ref, lk2_ref, gsub_ref, o_ref = refs[2 * nkv:2 * nkv + 6]
    s_scr = refs[2 * nkv + 6:]
    lam = (jnp.exp(jnp.sum(lq1_ref[...] * lk1_ref[...], axis=-1, keepdims=True))
           - jnp.exp(jnp.sum(lq2_ref[...] * lk2_ref[...], axis=-1, keepdims=True)) + lam_init)
    lane = lax.broadcasted_iota(jnp.int32, (tq, LANES), 1)
    tile_rows = lambda i: pl.ds(pl.multiple_of(i * tq, tq), tq)

    def tile_maps(i):
        maps = []
        for h in range(A_HPS):
            sl = slice(h * LANES, (h + 1) * LANES)
            q = q_ref[0, tile_rows(i), sl]
            zero = jnp.zeros_like(q)
            maps.append((jnp.where(lane < A_DIM, q, zero), (sl, sl), None))
            maps.append((jnp.where(lane >= A_DIM, q, zero), (sl, sl), None))
        return maps

    def finish(i, outs):
        for h, o in enumerate(outs):
            o_ref[0, tile_rows(i), h * LANES:(h + 1) * LANES] = (
                _rms(o, gsub_ref[...]) * (1.0 - lam_init)).astype(BF16)

    _attend_tiles(q_ref.shape[1] // tq, tile_maps, _ref_chunks(kv_refs), s_scr, finish, pair_coef=lam)


def _attn_a(lam_init, q, kv, lq1, lk1, lq2, lk2, g_sub):
    bsz, n, _ = q.shape
    tq = min(Q_TILE, n)
    w = A_HPS * LANES
    small = lambda a: pl.BlockSpec(a.shape, lambda b, h: (0, 0))
    return pl.pallas_call(
        functools.partial(_attn_a_kernel, lam_init, len(kv) // 2, tq),
        grid=(bsz, A_HEADS // A_HPS),
        in_specs=[pl.BlockSpec((1, n, w), lambda b, h: (b, 0, h))]
        + [pl.BlockSpec((1, a.shape[1], w), lambda b, h: (b, 0, h)) for a in kv]
        + [small(lq1), small(lk1), small(lq2), small(lk2), small(g_sub)],
        out_specs=pl.BlockSpec((1, n, w), lambda b, h: (b, 0, h)),
        out_shape=jax.ShapeDtypeStruct((bsz, n, A_HEADS * LANES), BF16),
        scratch_shapes=_score_buffers(kv, tq),
        compiler_params=_cparams(("arbitrary", "arbitrary")),
        name="attn_a",
    )(q, *kv, lq1, lk1, lq2, lk2, g_sub)


def _attn_b_kernel(nkv, tq, q_ref, *refs):
    kv_refs, o_ref, s_scr = refs[:2 * nkv], refs[2 * nkv], refs[2 * nkv + 1:]
    lane = lax.broadcasted_iota(jnp.int32, (tq, LANES), 1)
    tile_rows = lambda i: pl.ds(pl.multiple_of(i * tq, tq), tq)

    def tile_maps(i):
        maps = []
        for h in range(B_HPS):
            sl = slice(h * LANES, (h + 1) * LANES)
            vsl = slice((h // 2) * LANES, (h // 2 + 1) * LANES)
            maps.append((q_ref[0, tile_rows(i), sl], (sl, vsl), None))
        return maps

    def finish(i, outs):
        for pair in range(B_HPS // 2):
            o_ref[0, tile_rows(i), pair * LANES:(pair + 1) * LANES] = (
                jnp.where(lane < B_V, outs[2 * pair], outs[2 * pair + 1]).astype(BF16))

    _attend_tiles(q_ref.shape[1] // tq, tile_maps, _ref_chunks(kv_refs), s_scr, finish)


def _attn_b(q, kv):
    bsz, n, _ = q.shape
    tq = min(Q_TILE, n)
    kv_specs = []
    for i, a in enumerate(kv):
        w = B_HPS * (LANES if i % 2 == 0 else B_V)
        kv_specs.append(pl.BlockSpec((1, a.shape[1], w), lambda b, h: (b, 0, h)))
    return pl.pallas_call(
        functools.partial(_attn_b_kernel, len(kv) // 2, tq),
        grid=(bsz, B_HEADS // B_HPS),
        in_specs=[pl.BlockSpec((1, n, B_HPS * LANES), lambda b, h: (b, 0, h))] + kv_specs,
        out_specs=pl.BlockSpec((1, n, B_HPS * B_V), lambda b, h: (b, 0, h)),
        out_shape=jax.ShapeDtypeStruct((bsz, n, B_HEADS * B_V), BF16),
        scratch_shapes=_score_buffers(kv, tq),
        compiler_params=_cparams(("arbitrary", "arbitrary")),
        name="attn_b",
    )(q, *kv)


def _attn_c_kernel(latent, nblk, *refs):
    if latent:
        q_ref, kx_ref, vx_ref, sink_ref, kw_ref, vw_ref, bias_ref, o_ref = refs[:8]
    else:
        q_ref, kx_ref, vx_ref, sink_ref, o_ref = refs[:5]
    s_scr = refs[8 if latent else 5:]
    lane = lax.broadcasted_iota(jnp.int32, (Q_BLOCK, LANES), 1)
    block_rows = lambda j: pl.ds(pl.multiple_of(j * Q_BLOCK, Q_BLOCK), Q_BLOCK)

    chunks = [(lambda j: kx_ref[0], lambda j: vx_ref[0], None)]
    if latent:
        win_rows = lambda j: pl.ds(pl.multiple_of(j * Q_BLOCK, Q_BLOCK), 3 * Q_BLOCK)

        def bias_of(j):
            first_i = (jnp.asarray(j) == 0).astype(jnp.int32)
            last_i = (jnp.asarray(j) == nblk - 1).astype(jnp.int32)
            return bias_ref[1 - first_i + last_i + 2 * first_i * last_i]

        chunks.append((lambda j: kw_ref[0, win_rows(j), :], lambda j: vw_ref[0, win_rows(j), :], bias_of))

    def tile_maps(j):
        maps = []
        for r in range(C_REP):
            qr = q_ref[0, block_rows(j), r * LANES:(r + 1) * LANES]
            halves = [jnp.where((lane >= g * C_DIM) & (lane < (g + 1) * C_DIM), qr, jnp.zeros_like(qr))
                      for g in range(C_KV_HEADS)]
            rows = slice(C_KV_HEADS * r * Q_BLOCK, C_KV_HEADS * (r + 1) * Q_BLOCK)
            maps.append((jnp.concatenate(halves, axis=0), j, sink_ref[rows, :] * LOG2E))
        return maps

    def finish(j, outs):
        for r, o in enumerate(outs):
            o_ref[0, block_rows(j), r * LANES:(r + 1) * LANES] = (
                jnp.where(lane < C_DIM, o[:Q_BLOCK], o[Q_BLOCK:]).astype(BF16))

    _attend_tiles(nblk, tile_maps, chunks, s_scr, finish)


def _attn_c(q, kx, vx, sink_rows, k_lat=None, v_lat=None):
    bsz, n, _ = q.shape
    tc = kx.shape[1]
    nblk = n // Q_BLOCK
    latent = k_lat is not None
    rows = C_KV_HEADS * Q_BLOCK
    whole = lambda a: pl.BlockSpec((1,) + a.shape[1:], lambda b: (b,) + (0,) * (a.ndim - 1))
    const = lambda a: pl.BlockSpec(a.shape, lambda b: (0,) * a.ndim)
    in_specs = [whole(q), whole(kx), whole(vx), const(sink_rows)]
    args = [q, kx, vx, sink_rows]
    scratch = [pltpu.VMEM((rows, tc), F32)]
    if latent:
        pad = ((0, 0), (Q_BLOCK, Q_BLOCK), (0, 0))
        kw, vw = jnp.pad(k_lat, pad), jnp.pad(v_lat, pad)
        row = (np.arange(rows) % Q_BLOCK)[:, None]
        col = np.arange(3 * Q_BLOCK)[None, :]
        in_prev, in_next = col < Q_BLOCK, col >= 2 * Q_BLOCK
        band = np.where(in_prev, col >= row, np.where(in_next, col - 2 * Q_BLOCK <= row, True))
        variants = [band & ~in_prev, band, band & ~in_next, band & ~in_prev & ~in_next]
        bias = np.stack([np.where(ok, 0.0, NEG_INF).astype(np.float32) for ok in variants], axis=0)
        in_specs += [whole(kw), whole(vw), const(bias)]
        args += [kw, vw, bias]
        scratch.append(pltpu.VMEM((rows, 3 * Q_BLOCK), F32))
    scratch.append(pltpu.VMEM((rows, LANES), F32))
    return pl.pallas_call(
        functools.partial(_attn_c_kernel, latent, nblk),
        grid=(bsz,),
        in_specs=in_specs,
        out_specs=whole(q),
        out_shape=jax.ShapeDtypeStruct((bsz, n, 512), BF16),
        scratch_shapes=scratch,
        compiler_params=_cparams(("arbitrary",)),
        name="attn_c_latent" if latent else "attn_c_context",
    )(*args)


def _s5_prepare_kernel(are_ref, aim_ref, ldt_ref, btre_ref, btim_ref, cre_ref, cim_ref,
                       arec_ref, aimc_ref, ctre_ref, ctim_ref,
                       td_ref, wd_ref, vd_ref, a16re_ref, a16im_ref):
    p, n = D_GROUP, D_STATE
    gl = pl.program_id(0) % S5_SG

    def one_hot(shape, target):
        r = lax.broadcasted_iota(jnp.int32, shape, 0)
        c = lax.broadcasted_iota(jnp.int32, shape, 1)
        return jnp.where(c == target(r), 1.0, 0.0).astype(BF16)

    place_p = one_hot((p, LANES), lambda r: gl * p + r)
    place_n = one_hot((n, S5_SLB * LANES), lambda r: gl * n + r)
    pbits = p.bit_length() - 1
    place_tp = one_hot((S5_CHUNK * p, S5_FLAT), lambda r: (r >> pbits) * LANES + gl * p + (r & (p - 1)))

    refs = (are_ref, aim_ref, ldt_ref, btre_ref, btim_ref, cre_ref, cim_ref, a16re_ref, a16im_ref)
    kt_fwd, w_fwd = _s5_prepare_direction(0, *refs)
    kt_bwd, w_bwd = _s5_prepare_direction(1, *refs)
    pieces = [kt_bwd[(S5_CHUNK - 1 - i) * p:(S5_CHUNK - i) * p] for i in range(S5_CHUNK - 1)]
    pieces.append(kt_fwd[0:p] + kt_bwd[0:p])
    pieces.append(kt_fwd[p:])
    lagk = _dot(jnp.concatenate(pieces, axis=0).astype(BF16), place_p)
    for t_in in range(S5_CHUNK):
        row = [lagk[(t_out - t_in + S5_CHUNK - 1) * p:(t_out - t_in + S5_CHUNK) * p] for t_out in range(S5_CHUNK)]
        td_ref[0, t_in, 0] = jnp.concatenate(row, axis=1).astype(BF16)

    w_sets = [w_fwd[0][::-1], w_fwd[1][::-1], w_bwd[0], w_bwd[1]]
    for x, rows in enumerate(w_sets):
        wexp = _dot(jnp.concatenate(rows, axis=0).astype(BF16), place_n)
        for t in range(S5_CHUNK):
            wd_ref[0, t, 0, :, x * S5_SLB * LANES:(x + 1) * S5_SLB * LANES] = wexp[t * p:(t + 1) * p].astype(BF16)

    tt = lax.broadcasted_iota(jnp.int32, (n, S5_CHUNK * p), 1) >> pbits
    for d in range(N_DIRS):
        power = (tt + 1 if d == 0 else S5_CHUNK - tt).astype(F32)
        dt = jnp.exp(ldt_ref[d, 0])
        mag = jnp.exp(power * (arec_ref[d, 0] * dt))
        ang = power * (aimc_ref[d, 0] * dt)
        pr, pi = mag * jnp.cos(ang), mag * jnp.sin(ang)
        ctre, ctim = ctre_ref[d, 0], ctim_ref[d, 0]
        vd_ref[0, 2 * d, 0] = _dot((ctre * pr - ctim * pi).astype(BF16), place_tp).astype(BF16)
        vd_ref[0, 2 * d + 1, 0] = _dot((-(ctre * pi + ctim * pr)).astype(BF16), place_tp).astype(BF16)


def _s5_prepare_direction(d, are_ref, aim_ref, ldt_ref, btre_ref, btim_ref, cre_ref, cim_ref, a16re_ref, a16im_ref):
    hi = lax.Precision.HIGHEST
    are = are_ref[d, 0]
    aim = aim_ref[d, 0]
    dt = jnp.exp(ldt_ref[d, 0])
    nj = S5_CHUNK + 1
    jj = lax.broadcasted_iota(jnp.int32, (nj, D_STATE), 0).astype(F32)
    mag = jnp.exp(jj * (are * dt))
    ang = jj * (aim * dt)
    pre = mag * jnp.cos(ang)
    pim = mag * jnp.sin(ang)
    xr = pre[1:2] - 1.0
    xi = pim[1:2]
    den = are * are + aim * aim
    fr = (xr * are + xi * aim) / den
    fi = (xi * are - xr * aim) / den
    btre = btre_ref[d, 0]
    btim = btim_ref[d, 0]
    bbre = fr * btre - fi * btim
    bbim = fr * btim + fi * btre
    cre = cre_ref[d, 0]
    cim = cim_ref[d, 0]
    wj_re, wj_im = [], []
    for j in range(S5_CHUNK):
        pr = pre[j:j + 1]
        pi = pim[j:j + 1]
        wj_re.append(pr * bbre - pi * bbim)
        wj_im.append(pr * bbim + pi * bbre)
    dn = (((1,), (1,)), ((), ()))
    a16re_ref[d, 0] = pre[S5_CHUNK:S5_CHUNK + 1]
    a16im_ref[d, 0] = pim[S5_CHUNK:S5_CHUNK + 1]
    kt = (lax.dot_general(jnp.concatenate(wj_re, axis=0), cre, dn, precision=hi, preferred_element_type=F32)
          - lax.dot_general(jnp.concatenate(wj_im, axis=0), cim, dn, precision=hi, preferred_element_type=F32))
    return kt, (wj_re, wj_im)


def _s5_prepare(a_re, a_im, log_dt, b_re, b_im, c_re, c_im):
    g, n, p = D_GROUPS, D_STATE, D_GROUP
    nsg = g // S5_SG
    v4 = lambda a: a.reshape(N_DIRS, g, 1, a.shape[-1])
    col = lambda a: a.reshape(N_DIRS, g, n, 1)
    spec = lambda *s: pl.BlockSpec((N_DIRS, 1) + s, lambda i: (0, i) + (0,) * len(s))
    grp = lambda *s: pl.BlockSpec((1, s[0], 1) + s[1:], lambda i: (i // S5_SG, 0, i % S5_SG, 0, 0))
    bt_re = jnp.swapaxes(b_re, -1, -2)
    bt_im = jnp.swapaxes(b_im, -1, -2)
    ct_re = jnp.tile(jnp.swapaxes(c_re, -1, -2), (1, 1, 1, S5_CHUNK))
    ct_im = jnp.tile(jnp.swapaxes(c_im, -1, -2), (1, 1, 1, S5_CHUNK))
    outs = pl.pallas_call(
        _s5_prepare_kernel,
        grid=(g,),
        in_specs=[spec(1, n), spec(1, n), spec(1, 1), spec(p, n), spec(p, n), spec(p, n), spec(p, n),
                  spec(n, 1), spec(n, 1), spec(n, S5_CHUNK * p), spec(n, S5_CHUNK * p)],
        out_specs=[grp(S5_CHUNK, p, S5_FLAT), grp(S5_CHUNK, p, 4 * S5_SLB * LANES), grp(4, n, S5_FLAT),
                   spec(1, n), spec(1, n)],
        out_shape=[jax.ShapeDtypeStruct((nsg, S5_CHUNK, S5_SG, p, S5_FLAT), BF16),
                   jax.ShapeDtypeStruct((nsg, S5_CHUNK, S5_SG, p, 4 * S5_SLB * LANES), BF16),
                   jax.ShapeDtypeStruct((nsg, 4, S5_SG, n, S5_FLAT), BF16)]
        + [jax.ShapeDtypeStruct((N_DIRS, g, 1, n), F32)] * 2,
        compiler_params=_cparams(("arbitrary",)),
        name="s5_prepare",
    )(v4(a_re), v4(a_im), log_dt.reshape(N_DIRS, g, 1, 1), bt_re, bt_im, c_re, c_im,
      col(a_re), col(a_im), ct_re, ct_im)
    td, wd, vd, a16re, a16im = outs
    t_sg, w_sg, v_sg = td, wd, vd
    a16 = [a16re[0].reshape(1, g * n), a16im[0].reshape(1, g * n),
           a16re[1].reshape(1, g * n), a16im[1].reshape(1, g * n)]
    return t_sg, w_sg, v_sg, a16


def _s5_matrix_spec(a):
    return pl.BlockSpec((1,) + a.shape[1:], lambda s, r: (s,) + (0,) * (a.ndim - 1), pipeline_mode=pl.Buffered(1))


def _rows2d(ref):
    m = ref[0]
    return m.reshape(-1, m.shape[-1])


def _s5_gather_chunks(du_ref, u_scr, bsz, ck):
    for b in range(bsz):
        for t in range(S5_CHUNK):
            u_scr[b * ck:(b + 1) * ck, t * LANES:(t + 1) * LANES] = (
                du_ref[b, pl.ds(t, ck, stride=S5_CHUNK), :].astype(BF16))


def _s5_state_in_kernel(bsz, bpad, ck, du_ref, w_ref, o0_ref, o1_ref, o2_ref, o3_ref, u_scr):
    _s5_gather_chunks(du_ref, u_scr, bsz, ck)
    s = _dot(u_scr[...], _rows2d(w_ref))
    for x, o_ref in enumerate((o0_ref, o1_ref, o2_ref, o3_ref)):
        for j in range(S5_SLB):
            col = (x * S5_SLB + j) * LANES
            for b in range(bpad):
                if b < bsz:
                    o_ref[j, pl.ds(b, ck, stride=bpad), :] = s[b * ck:(b + 1) * ck, col:col + LANES]
                else:
                    o_ref[j, pl.ds(b, ck, stride=bpad), :] = jnp.zeros((ck, LANES), F32)


def _s5_state_in(du, w_sg, bpad, ck):
    bsz, n, _ = du.shape
    nchunk = n // S5_CHUNK
    nsg = w_sg.shape[0]
    return pl.pallas_call(
        functools.partial(_s5_state_in_kernel, bsz, bpad, ck),
        grid=(nsg, nchunk // ck),
        in_specs=[pl.BlockSpec((bsz, ck * S5_CHUNK, LANES), lambda s, r: (0, r, s)),
                  _s5_matrix_spec(w_sg)],
        out_specs=[pl.BlockSpec((S5_SLB, ck * bpad, LANES), lambda s, r: (s, r, 0))] * 4,
        out_shape=[jax.ShapeDtypeStruct((nsg * S5_SLB, nchunk * bpad, LANES), F32)] * 4,
        scratch_shapes=[pltpu.VMEM((bsz * ck, S5_FLAT), BF16)],
        compiler_params=_cparams(("arbitrary", "arbitrary")),
        name="s5_state_in",
    )(du, w_sg)


def _s5_scan_kernel(nchunk, bpad, s0re_ref, s0im_ref, s1re_ref, s1im_ref, a0re_ref, a0im_ref, a1re_ref, a1im_ref,
                    h0re_ref, h0im_ref, h1re_ref, h1im_ref,
                    p0re_ref, p0im_ref, p1re_ref, p1im_ref, f0re_ref, f0im_ref, f1re_ref, f1im_ref):
    a0re, a0im, a1re, a1im = a0re_ref[...], a0im_ref[...], a1re_ref[...], a1im_ref[...]

    def body(i, carry):
        r0, i0, r1, i1 = carry
        k = pl.ds(pl.multiple_of(i * bpad, bpad), bpad)
        kb = pl.ds(pl.multiple_of((nchunk - 1 - i) * bpad, bpad), bpad)
        p0re_ref[:, k, :] = r0
        p0im_ref[:, k, :] = i0
        p1re_ref[:, kb, :] = r1
        p1im_ref[:, kb, :] = i1
        n_r0 = a0re * r0 - a0im * i0 + s0re_ref[:, k, :]
        n_i0 = a0re * i0 + a0im * r0 + s0im_ref[:, k, :]
        n_r1 = a1re * r1 - a1im * i1 + s1re_ref[:, kb, :]
        n_i1 = a1re * i1 + a1im * r1 + s1im_ref[:, kb, :]
        return n_r0, n_i0, n_r1, n_i1

    r0, i0, r1, i1 = lax.fori_loop(0, nchunk, body,
                                   (h0re_ref[...], h0im_ref[...], h1re_ref[...], h1im_ref[...]))
    f0re_ref[...] = r0
    f0im_ref[...] = i0
    f1re_ref[...] = r1
    f1im_ref[...] = i1


def _s5_scan(s_in, a16, h0, nchunk, bpad):
    nlb = D_GROUPS * D_STATE // LANES
    lb = 2
    seq_spec = pl.BlockSpec((lb, nchunk * bpad, LANES), lambda i: (i, 0, 0))
    a_spec = pl.BlockSpec((lb, 1, LANES), lambda i: (i, 0, 0))
    h_spec = pl.BlockSpec((lb, bpad, LANES), lambda i: (i, 0, 0))
    outs = pl.pallas_call(
        functools.partial(_s5_scan_kernel, nchunk, bpad),
        grid=(nlb // lb,),
        in_specs=[seq_spec] * 4 + [a_spec] * 4 + [h_spec] * 4,
        out_specs=[seq_spec] * 4 + [h_spec] * 4,
        out_shape=[jax.ShapeDtypeStruct((nlb, nchunk * bpad, LANES), F32)] * 4
        + [jax.ShapeDtypeStruct((nlb, bpad, LANES), F32)] * 4,
        compiler_params=_cparams(("arbitrary",)),
        name="s5_scan",
    )(*s_in, *a16, *h0)
    return outs[:4], outs[4:]


def _s5_output_kernel(bsz, bpad, ck, du_ref, t_ref, v_ref, p0_ref, p1_ref, p2_ref, p3_ref, y_ref, u_scr, p_scr):
    _s5_gather_chunks(du_ref, u_scr, bsz, ck)
    for x, p_ref in enumerate((p0_ref, p1_ref, p2_ref, p3_ref)):
        for j in range(S5_SLB):
            col = (x * S5_SLB + j) * LANES
            for b in range(bsz):
                p_scr[b * ck:(b + 1) * ck, col:col + LANES] = (
                    p_ref[j, pl.ds(b, ck, stride=bpad), :].astype(BF16))
    y = _dot(u_scr[...], _rows2d(t_ref)) + _dot(p_scr[...], _rows2d(v_ref))
    for b in range(bsz):
        for t in range(S5_CHUNK):
            y_ref[b, pl.ds(t, ck, stride=S5_CHUNK), :] = y[b * ck:(b + 1) * ck, t * LANES:(t + 1) * LANES]


def _s5_output(du, t_sg, v_sg, p_states, bpad, ck):
    bsz, n, _ = du.shape
    nchunk = n // S5_CHUNK
    nsg = t_sg.shape[0]
    tok = pl.BlockSpec((bsz, ck * S5_CHUNK, LANES), lambda s, r: (0, r, s))
    return pl.pallas_call(
        functools.partial(_s5_output_kernel, bsz, bpad, ck),
        grid=(nsg, nchunk // ck),
        in_specs=[tok, _s5_matrix_spec(t_sg), _s5_matrix_spec(v_sg)]
        + [pl.BlockSpec((S5_SLB, ck * bpad, LANES), lambda s, r: (s, r, 0))] * 4,
        out_specs=tok,
        out_shape=jax.ShapeDtypeStruct((bsz, n, MIX_W), F32),
        scratch_shapes=[pltpu.VMEM((bsz * ck, S5_FLAT), BF16), pltpu.VMEM((bsz * ck, S5_FLAT), BF16)],
        compiler_params=_cparams(("arbitrary", "arbitrary")),
        name="s5_output",
    )(du, t_sg, v_sg, *p_states)


def _s5_mixer(du, mats, h0):
    t_sg, w_sg, v_sg, a16 = mats
    bsz, n, _ = du.shape
    nchunk = n // S5_CHUNK
    bpad = -(-bsz // 8) * 8
    ck = min(nchunk, max(S5_ROWS // bsz, 1))
    assert nchunk % ck == 0 and (ck * S5_CHUNK) % 8 == 0
    to_blocks = lambda h: h.reshape(h.shape[0], -1, LANES).transpose(1, 0, 2)
    h0 = [to_blocks(jnp.pad(h, ((0, bpad - bsz), (0, 0)))) for h in h0]
    a16 = [to_blocks(a) for a in a16]
    s_in = _s5_state_in(du, w_sg, bpad, ck)
    p_states, finals = _s5_scan(s_in, a16, h0, nchunk, bpad)
    y = _s5_output(du, t_sg, v_sg, p_states, bpad, ck)
    return y, [f.transpose(1, 0, 2).reshape(bpad, -1)[:bsz] for f in finals]


def _merge_kernel(x_ref, hb_ref, oa_ref, ob_ref, oc_ref, y_ref, u_ref, mod_ref, dskip_ref, wglu_ref,
                  wgate_ref, wbr_ref, wo_ref, gpost_ref, xo_ref):
    yv = y_ref[...] + dskip_ref[...] * u_ref[...]
    yg = jax.nn.gelu(yv)
    od = yg * jax.nn.sigmoid(_dot(yg.astype(BF16), wglu_ref[...]))
    hb = hb_ref[...]
    branches = (oa_ref[...], ob_ref[...], oc_ref[...], od.astype(BF16))
    merged = None
    for k, o in enumerate(branches):
        gate = jax.nn.sigmoid(_dot(hb, wgate_ref[:, k * D_MODEL:(k + 1) * D_MODEL]))
        term = gate * _dot(o, wbr_ref[k])
        merged = term if merged is None else merged + term
    z = _dot(merged.astype(BF16), wo_ref[...])
    gate_m = mod_ref[0, 2:3, :]
    xo_ref[...] = x_ref[...] + gate_m * _rms(z, gpost_ref[...])


def _merge(x2, hb, oa, ob, oc, y, du, mod6, d_skip, w_glu, w_gate, w_branch, w_o, g_post, seq):
    m = x2.shape[0]
    tm = _row_tile(m)
    tiles_per_seq = max(seq // tm, 1)
    assert mod6.shape[0] == 1 or seq % tm == 0
    row = lambda w: pl.BlockSpec((tm, w), lambda i: (i, 0))
    full = _resident
    if mod6.shape[0] == 1:
        mod_spec = pl.BlockSpec((1, 6, D_MODEL), lambda i: (0, 0, 0))
    else:
        mod_spec = pl.BlockSpec((1, 6, D_MODEL), lambda i: (i // tiles_per_seq, 0, 0))
    return pl.pallas_call(
        _merge_kernel,
        grid=(m // tm,),
        in_specs=[row(D_MODEL), row(D_MODEL), row(MIX_W), row(MIX_W), row(MIX_W), row(MIX_W), row(MIX_W),
                  mod_spec, full(d_skip), full(w_glu), full(w_gate), full(w_branch), full(w_o), full(g_post)],
        out_specs=row(D_MODEL),
        out_shape=jax.ShapeDtypeStruct((m, D_MODEL), F32),
        compiler_params=_cparams(("arbitrary",)),
        name="merge",
    )(x2, hb, oa, ob, oc, y, du, mod6, d_skip, w_glu, w_gate, w_branch, w_o, g_post)


def _ffn_kernel(x_ref, mod_ref, gpre_ref, w1_ref, w3_ref, w2_ref, gpost_ref, xo_ref):
    x = x_ref[...]
    shift = mod_ref[0, 3:4, :]
    scale = mod_ref[0, 4:5, :]
    gate = mod_ref[0, 5:6, :]
    h2 = (_rms(x, gpre_ref[...]) * (1.0 + scale) + shift).astype(BF16)
    a = _dot(h2, w1_ref[...])
    b = _dot(h2, w3_ref[...])
    f = _dot((jax.nn.silu(a) * b).astype(BF16), w2_ref[...])
    xo_ref[...] = x + gate * _rms(f, gpost_ref[...])


def _ffn(x2, mod6, g_pre, w1, w3, w2, g_post, seq):
    m = x2.shape[0]
    tm = _row_tile(m)
    tiles_per_seq = max(seq // tm, 1)
    assert mod6.shape[0] == 1 or seq % tm == 0
    row = lambda w: pl.BlockSpec((tm, w), lambda i: (i, 0))
    full = _resident
    if mod6.shape[0] == 1:
        mod_spec = pl.BlockSpec((1, 6, D_MODEL), lambda i: (0, 0, 0))
    else:
        mod_spec = pl.BlockSpec((1, 6, D_MODEL), lambda i: (i // tiles_per_seq, 0, 0))
    return pl.pallas_call(
        _ffn_kernel,
        grid=(m // tm,),
        in_specs=[row(D_MODEL), mod_spec, full(g_pre), full(w1), full(w3), full(w2), full(g_post)],
        out_specs=row(D_MODEL),
        out_shape=jax.ShapeDtypeStruct((m, D_MODEL), F32),
        compiler_params=_cparams(("arbitrary",)),
        name="ffn",
    )(x2, mod6, g_pre, w1, w3, w2, g_post)


def _rope_tables(n, rot_dim, lane_off):
    f32 = np.float32
    rows = n // GRID_W
    pos_row = np.repeat(np.arange(rows, dtype=f32), GRID_W)
    pos_col = np.tile(np.arange(GRID_W, dtype=f32), rows)
    n_freq = rot_dim // 4
    inv_freq = (ROPE_BASE ** (-np.arange(n_freq, dtype=f32) / f32(n_freq))).astype(f32)
    ang = np.concatenate([pos_row[:, None] * inv_freq, pos_col[:, None] * inv_freq], axis=-1)
    cos, sin = np.cos(ang).astype(f32), np.sin(ang).astype(f32)
    zero = np.zeros_like(sin)
    c = np.concatenate([cos, cos], axis=-1)
    sa = np.concatenate([-sin, zero], axis=-1)
    sb = np.concatenate([zero, sin], axis=-1)
    if rot_dim == LANES // 2 and lane_off == 0:
        return tuple(np.tile(t, (1, 2)) for t in (c, sa, sb))
    pad = lambda t, fill: np.concatenate(
        [np.full((n, lane_off), fill, f32), t, np.full((n, LANES - lane_off - rot_dim), fill, f32)], axis=-1)
    return pad(c, 1.0), pad(sa, 0.0), pad(sb, 0.0)


def _layer_weights(l, w_in, w_gate, w_b_uq, w_b_ukv, sink_c, w_glu, w_branch, w_o, w_ff1, w_ff3, w_ff2):
    wi = w_in[l]
    aq, ak, av = wi[:, 0:512], wi[:, 512:1024], wi[:, 1024:1536]
    bcq, bckv, bkr = wi[:, 1536:1792], wi[:, 1792:1920], wi[:, 1920:1952]
    cq, ck, cv, du = wi[:, 1952:2464], wi[:, 2464:2592], wi[:, 2592:2720], wi[:, 2720:3232]
    cq_perm = cq.reshape(D_MODEL, C_KV_HEADS, C_REP, C_DIM).transpose(0, 2, 1, 3).reshape(D_MODEL, 512)
    zeros = lambda w: jnp.zeros((D_MODEL, w), F32)
    bkr_pad = jnp.concatenate([zeros(B_KR_LANE), bkr, zeros(LANES - B_KR_LANE - B_ROPE)], axis=1)
    w_in_r = jnp.concatenate([aq, ak, av, cq_perm, ck, cv, du, bcq, bckv, bkr_pad], axis=1).astype(BF16)

    uq = w_b_uq[l].reshape(B_QLORA, B_HEADS, B_NOPE + B_ROPE)
    uq = jnp.concatenate([uq, jnp.zeros((B_QLORA, B_HEADS, LANES - B_NOPE - B_ROPE), F32)], axis=-1)
    w_uq_r = uq.reshape(B_QLORA, B_HEADS * LANES).astype(BF16)
    ukv = w_b_ukv[l].reshape(B_KVLORA, B_HEADS, B_NOPE + B_V)
    kn = jnp.concatenate([ukv[..., :B_NOPE], jnp.zeros((B_KVLORA, B_HEADS, LANES - B_NOPE), F32)], axis=-1)
    w_ukv_r = jnp.concatenate([kn.reshape(B_KVLORA, B_HEADS * LANES),
                               ukv[..., B_NOPE:].reshape(B_KVLORA, B_HEADS * B_V)], axis=1).astype(BF16)

    sink_perm = sink_c[l].reshape(C_KV_HEADS, C_REP).T.reshape(C_HEADS)
    sink_rows = jnp.repeat(sink_perm, Q_BLOCK)[:, None]

    wbr = w_branch[l]
    wbr_c = wbr[2].reshape(C_KV_HEADS, C_REP, C_DIM, D_MODEL).transpose(1, 0, 2, 3).reshape(MIX_W, D_MODEL)
    w_branch_r = jnp.stack([wbr[0], wbr[1], wbr_c, wbr[3]], axis=0).astype(BF16)
    return dict(w_in=w_in_r, w_uq=w_uq_r, w_ukv=w_ukv_r, sink_rows=sink_rows,
                w_gate=w_gate[l].astype(BF16), w_glu=w_glu[l].astype(BF16), w_branch=w_branch_r,
                w_o=w_o[l].astype(BF16), w_ff1=w_ff1[l].astype(BF16), w_ff3=w_ff3[l].astype(BF16),
                w_ff2=w_ff2[l].astype(BF16))


def _trunk_layer(l, x, mod6, lw, sp, s5_mats, rope_tabs, ctx):
    latent = ctx is not None
    bsz, n, _ = x.shape
    m = bsz * n
    x2 = x.reshape(m, D_MODEL)
    row1 = lambda v: v.reshape(1, -1)

    outs = _premix(latent, x2, mod6, row1(sp['g_pre_mix']), lw['w_in'], row1(sp['g_b_q']), lw['w_uq'],
                   row1(sp['g_b_kv']), lw['w_ukv'], rope_tabs, n)
    hb, qa, ka, va, qb, kb, vb, qc, kc, vc, du = outs[:11]
    r3 = lambda t: t.reshape(bsz, n, t.shape[-1])
    qa, ka, va, qb, kb, vb, qc, kc, vc, du = map(r3, (qa, ka, va, qb, kb, vb, qc, kc, vc, du))

    lam_init = 0.8 - 0.6 * math.exp(-0.3 * l)
    lam_args = (row1(sp['lam_q1']), row1(sp['lam_k1']), row1(sp['lam_q2']), row1(sp['lam_k2']), row1(sp['g_a_sub']))
    if latent:
        past = ctx['a_k'].shape[1]
        ka_ctx = ctx['a_k'].reshape(bsz, past, 512).astype(BF16)
        va_ctx = ctx['a_v'].reshape(bsz, past, 512).astype(BF16)
        kr_pad = jnp.pad(ctx['b_kr'].reshape(bsz * past, B_ROPE),
                         ((0, 0), (B_KR_LANE, LANES - B_KR_LANE - B_ROPE)))
        kb_ctx, vb_ctx = _kvprep(ctx['b_ckv'].reshape(bsz * past, B_KVLORA), kr_pad, lw['w_ukv'])
        o_a = _attn_a(lam_init, qa, [ka, va, ka_ctx, va_ctx], *lam_args)
        o_b = _attn_b(qb, [kb, vb, kb_ctx.reshape(bsz, past, -1), vb_ctx.reshape(bsz, past, -1)])
        o_c = _attn_c(qc, ctx['c_k'].reshape(bsz, past, LANES).astype(BF16),
                      ctx['c_v'].reshape(bsz, past, LANES).astype(BF16), lw['sink_rows'], kc, vc)
        gn = D_GROUPS * D_STATE
        h0 = [ctx['d_re'][:, 0].reshape(bsz, gn), ctx['d_im'][:, 0].reshape(bsz, gn),
              ctx['d_re'][:, 1].reshape(bsz, gn), ctx['d_im'][:, 1].reshape(bsz, gn)]
    else:
        o_a = _attn_a(lam_init, qa, [ka, va], *lam_args)
        o_b = _attn_b(qb, [kb, vb])
        o_c = _attn_c(qc, kc, vc, lw['sink_rows'])
        h0 = [jnp.zeros((bsz, D_GROUPS * D_STATE), F32)] * 4
    y_s5, finals = _s5_mixer(du, s5_mats, h0)

    f2 = lambda t: t.reshape(m, t.shape[-1])
    x2 = _merge(x2, hb, f2(o_a), f2(o_b), f2(o_c), f2(y_s5), f2(du), mod6, row1(sp['ssm_d']), lw['w_glu'],
                lw['w_gate'], lw['w_branch'], lw['w_o'], row1(sp['g_post_mix']), n)
    x2 = _ffn(x2, mod6, row1(sp['g_pre_ffn']), lw['w_ff1'], lw['w_ff3'], lw['w_ff2'], row1(sp['g_post_ffn']), n)
    x = x2.reshape(bsz, n, D_MODEL)
    if latent:
        return x, None
    akf, avf, ckvf, krf, ckf, cvf = outs[11:]
    st = lambda t: t.reshape(bsz, D_GROUPS, D_STATE)
    new_ctx = {'a_k': akf.reshape(bsz, n, A_HEADS, 2 * A_DIM), 'a_v': avf.reshape(bsz, n, A_HEADS, 2 * A_DIM),
               'b_ckv': ckvf.reshape(bsz, n, B_KVLORA), 'b_kr': krf.reshape(bsz, n, B_ROPE),
               'c_k': ckf.reshape(bsz, n, C_KV_HEADS, C_DIM), 'c_v': cvf.reshape(bsz, n, C_KV_HEADS, C_DIM),
               'd_re': jnp.stack([st(finals[0]), st(finals[2])], axis=1),
               'd_im': jnp.stack([st(finals[1]), st(finals[3])], axis=1)}
    return x, new_ctx


def kernel(x_prompt, x_sample, cache_a_k, cache_a_v, cache_b_ckv, cache_b_kr, cache_c_k, cache_c_v, state_d_re, state_d_im, c, c_ctx, w_mod, b_mod, g_pre_mix, g_post_mix, g_pre_ffn, g_post_ffn, w_in, w_gate, lam_q1, lam_k1, lam_q2, lam_k2, g_a_sub, g_b_q, g_b_kv, w_b_uq, w_b_ukv, sink_c, ssm_a_re, ssm_a_im, ssm_log_dt, ssm_b_re, ssm_b_im, ssm_c_re, ssm_c_im, ssm_d, w_glu, w_branch, w_o, w_ff1, w_ff3, w_ff2):
    dec_b, dec_n, _ = x_sample.shape
    assert dec_b + 1 <= 8

    cond = jnp.concatenate([c_ctx[None, :], c, jnp.zeros((8 - 1 - dec_b, D_MODEL), F32)], axis=0)
    mod = _modulation(cond, w_mod, b_mod)

    small = dict(g_pre_mix=g_pre_mix, g_post_mix=g_post_mix, g_pre_ffn=g_pre_ffn, g_post_ffn=g_post_ffn,
                 lam_q1=lam_q1, lam_k1=lam_k1, lam_q2=lam_q2, lam_k2=lam_k2, g_a_sub=g_a_sub, g_b_q=g_b_q,
                 g_b_kv=g_b_kv, ssm_d=ssm_d)
    rope_tabs = _rope_tables(dec_n, A_DIM, 0) + _rope_tables(dec_n, B_ROPE, B_KR_LANE)

    layers = []
    for l in range(DEPTH):
        lw = _layer_weights(l, w_in, w_gate, w_b_uq, w_b_ukv, sink_c, w_glu, w_branch, w_o, w_ff1, w_ff3, w_ff2)
        sp = {k: v[l] for k, v in small.items()}
        s5_mats = _s5_prepare(ssm_a_re[l], ssm_a_im[l], ssm_log_dt[l], ssm_b_re[l], ssm_b_im[l],
                              ssm_c_re[l], ssm_c_im[l])
        mod_l = mod[l].reshape(8, 6, D_MODEL)
        layers.append((lw, sp, s5_mats, mod_l))

    y_prompt = x_prompt
    ctx_out = []
    for l, (lw, sp, s5_mats, mod_l) in enumerate(layers):
        y_prompt, new_ctx = _trunk_layer(l, y_prompt, mod_l[0:1], lw, sp, s5_mats, None, None)
        ctx_out.append(new_ctx)

    y_sample = x_sample
    for l, (lw, sp, s5_mats, mod_l) in enumerate(layers):
        cached = {'a_k': cache_a_k[:, l], 'a_v': cache_a_v[:, l], 'b_ckv': cache_b_ckv[:, l],
                  'b_kr': cache_b_kr[:, l], 'c_k': cache_c_k[:, l], 'c_v': cache_c_v[:, l],
                  'd_re': state_d_re[:, l], 'd_im': state_d_im[:, l]}
        y_sample, _ = _trunk_layer(l, y_sample, mod_l[1:1 + dec_b], lw, sp, s5_mats, rope_tabs, cached)

    stack = lambda name: jnp.stack([cx[name] for cx in ctx_out], axis=1)
    return (y_prompt, y_sample, stack('a_k'), stack('a_v'), stack('b_ckv'), stack('b_kr'),
            stack('c_k'), stack('c_v'), stack('d_re'), stack('d_im'))
```

```python
import functools
import math

import jax
import jax.numpy as jnp
import numpy as np
from jax import lax
from jax.experimental import pallas as pl
from jax.experimental.pallas import tpu as pltpu

F32 = jnp.float32
BF16 = jnp.bfloat16

D_MODEL = 1024
DEPTH = 2
GRID_W = 64
Q_BLOCK = 128
ROPE_BASE = 10000.0
RMS_EPS = 1e-6
NEG_INF = -1e30
LOG2E = math.log2(math.e)
MIX_W = D_MODEL // 2
N_BRANCH = 4
A_HEADS = 4
A_DIM = 64
B_HEADS = 8
B_NOPE = 64
B_ROPE = 32
B_V = 64
B_QLORA = 256
B_KVLORA = 128
C_HEADS = 8
C_KV_HEADS = 2
C_REP = C_HEADS // C_KV_HEADS
C_DIM = 64
D_GROUP = 16
D_GROUPS = MIX_W // D_GROUP
D_STATE = 64
N_DIRS = 2
FF_HIDDEN = ((8 * D_MODEL // 3 + 255) // 256) * 256

LANES = 128
S5_CHUNK = 16
S5_SG = LANES // D_GROUP
S5_FLAT = S5_CHUNK * LANES
S5_SLB = S5_SG * D_STATE // LANES
S5_ROWS = 256
VMEM_LIMIT = 56 * 1024 * 1024
ROW_TILE = 512
Q_TILE = 256
KEY_CHUNK = 512
A_HPS = 2
B_HPS = 4

COL_AQ, COL_AK, COL_AV = 0, 512, 1024
COL_CQ, COL_CK, COL_CV = 1536, 2048, 2176
COL_DU = 2304
COL_BCQ, COL_BCKV, COL_BKR = 2816, 3072, 3200
IN_COLS = 3328
B_KR_LANE = B_NOPE


def _cparams(sem):
    return pltpu.CompilerParams(dimension_semantics=sem, vmem_limit_bytes=VMEM_LIMIT)


def _resident(a):
    return pl.BlockSpec(a.shape, lambda i: (0,) * a.ndim, pipeline_mode=pl.Buffered(1))


def _row_tile(m):
    return min(ROW_TILE, m)


def _dot(a, b):
    return jnp.dot(a, b, preferred_element_type=F32)


def _dot_nt(a, b):
    return lax.dot_general(a, b, (((1,), (1,)), ((), ())), preferred_element_type=F32)


def _rms(x, g):
    return x * lax.rsqrt(jnp.mean(x * x, axis=-1, keepdims=True) + RMS_EPS) * g


def _rope(x, c, sa, sb, half):
    w = x.shape[-1]
    return x * c + pltpu.roll(x, w - half, 1) * sa + pltpu.roll(x, half, 1) * sb


def _mod_kernel(c_ref, w_ref, b_ref, o_ref):
    c = c_ref[...]
    o_ref[0] = _dot(jax.nn.silu(c).astype(BF16), w_ref[0].astype(BF16)) + b_ref[0]


def _modulation(cond, w_mod, b_mod):
    nblk = 6
    return pl.pallas_call(
        _mod_kernel,
        grid=(DEPTH, nblk),
        in_specs=[pl.BlockSpec((8, D_MODEL), lambda l, j: (0, 0)),
                  pl.BlockSpec((1, D_MODEL, D_MODEL), lambda l, j: (l, 0, j)),
                  pl.BlockSpec((1, 1, D_MODEL), lambda l, j: (l, 0, j))],
        out_specs=pl.BlockSpec((1, 8, D_MODEL), lambda l, j: (l, 0, j)),
        out_shape=jax.ShapeDtypeStruct((DEPTH, 8, 6 * D_MODEL), F32),
        compiler_params=_cparams(("arbitrary", "arbitrary")),
        name="modulation",
    )(cond, w_mod, b_mod.reshape(DEPTH, 1, 6 * D_MODEL))


def _premix_kernel(latent, *refs):
    if latent:
        (x_ref, mod_ref, g_ref, win_ref, gbq_ref, wuq_ref, gbkv_ref, wukv_ref,
         ca_ref, saa_ref, sba_ref, cb_ref, sab_ref, sbb_ref,
         hb_ref, qa_ref, ka_ref, va_ref, qb_ref, kb_ref, vb_ref, qc_ref, kc_ref, vc_ref, du_ref) = refs
    else:
        (x_ref, mod_ref, g_ref, win_ref, gbq_ref, wuq_ref, gbkv_ref, wukv_ref,
         hb_ref, qa_ref, ka_ref, va_ref, qb_ref, kb_ref, vb_ref, qc_ref, kc_ref, vc_ref, du_ref,
         akf_ref, avf_ref, ckvf_ref, krf_ref, ckf_ref, cvf_ref) = refs

    x = x_ref[...]
    shift = mod_ref[0, 0:1, :]
    scale = mod_ref[0, 1:2, :]
    h = _rms(x, g_ref[...]) * (1.0 + scale) + shift
    hb = h.astype(BF16)
    hb_ref[...] = hb
    proj = _dot(hb, win_ref[...])

    if latent:
        ca, saa, sba = ca_ref[...], saa_ref[...], sba_ref[...]
        cb, sab, sbb = cb_ref[...], sab_ref[...], sbb_ref[...]
        rope_a = lambda t: _rope(t, ca, saa, sba, A_DIM // 2)
        rope_b = lambda t: _rope(t, cb, sab, sbb, B_ROPE // 2)
    else:
        rope_a = rope_b = lambda t: t

    def blk(col, i):
        return proj[:, col + i * LANES: col + (i + 1) * LANES]

    a_scale = A_DIM ** -0.5 * LOG2E
    for i in range(A_HEADS):
        sl = slice(i * LANES, (i + 1) * LANES)
        qa_ref[:, sl] = (rope_a(blk(COL_AQ, i)) * a_scale).astype(BF16)
        ka_ref[:, sl] = rope_a(blk(COL_AK, i)).astype(BF16)
    va_ref[...] = proj[:, COL_AV:COL_AV + 512].astype(BF16)

    c_scale = C_DIM ** -0.5 * LOG2E
    for i in range(C_REP):
        sl = slice(i * LANES, (i + 1) * LANES)
        qc_ref[:, sl] = (rope_a(blk(COL_CQ, i)) * c_scale).astype(BF16)
    kc_ref[...] = rope_a(blk(COL_CK, 0)).astype(BF16)
    vc_ref[...] = blk(COL_CV, 0).astype(BF16)

    du_ref[...] = proj[:, COL_DU:COL_DU + MIX_W]

    b_scale = (B_NOPE + B_ROPE) ** -0.5 * LOG2E
    cqn = _rms(proj[:, COL_BCQ:COL_BCQ + B_QLORA], gbq_ref[...])
    qb = _dot(cqn.astype(BF16), wuq_ref[...])
    ckv = _rms(proj[:, COL_BCKV:COL_BCKV + B_KVLORA], gbkv_ref[...])
    kvb = _dot(ckv.astype(BF16), wukv_ref[...])
    kr_pad = rope_b(blk(COL_BKR, 0))
    for i in range(B_HEADS):
        sl = slice(i * LANES, (i + 1) * LANES)
        qb_ref[:, sl] = (rope_b(qb[:, sl]) * b_scale).astype(BF16)
        kb_ref[:, sl] = (kvb[:, sl] + kr_pad).astype(BF16)
    vb_ref[...] = kvb[:, B_HEADS * LANES:].astype(BF16)

    if not latent:
        akf_ref[...] = proj[:, COL_AK:COL_AK + 512]
        avf_ref[...] = proj[:, COL_AV:COL_AV + 512]
        ckvf_ref[...] = ckv
        krf_ref[...] = kr_pad[:, B_KR_LANE:B_KR_LANE + B_ROPE]
        ckf_ref[...] = blk(COL_CK, 0)
        cvf_ref[...] = blk(COL_CV, 0)


def _premix(latent, x2, mod6, g_pre, w_in, g_bq, w_uq, g_bkv, w_ukv, rope_tabs, seq):
    m = x2.shape[0]
    tm = _row_tile(m)
    tiles_per_seq = max(seq // tm, 1)
    nb_mod = mod6.shape[0]
    assert nb_mod == 1 or seq % tm == 0
    row = lambda w: pl.BlockSpec((tm, w), lambda i: (i, 0))
    full = _resident
    if nb_mod == 1:
        mod_spec = pl.BlockSpec((1, 6, D_MODEL), lambda i: (0, 0, 0))
    else:
        mod_spec = pl.BlockSpec((1, 6, D_MODEL), lambda i: (i // tiles_per_seq, 0, 0))
    in_specs = [row(D_MODEL), mod_spec, full(g_pre), full(w_in), full(g_bq), full(w_uq), full(g_bkv), full(w_ukv)]
    args = [x2, mod6, g_pre, w_in, g_bq, w_uq, g_bkv, w_ukv]
    if latent:
        tab_spec = pl.BlockSpec((tm, LANES), lambda i: (i % tiles_per_seq, 0))
        in_specs += [tab_spec] * 6
        args += list(rope_tabs)
    widths = [(D_MODEL, BF16), (512, BF16), (512, BF16), (512, BF16), (B_HEADS * LANES, BF16),
              (B_HEADS * LANES, BF16), (B_HEADS * B_V, BF16), (512, BF16), (LANES, BF16), (LANES, BF16),
              (MIX_W, F32)]
    if not latent:
        widths += [(512, F32), (512, F32), (B_KVLORA, F32), (B_ROPE, F32), (LANES, F32), (LANES, F32)]
    out_specs = [row(w) for w, _ in widths]
    out_shape = [jax.ShapeDtypeStruct((m, w), dt) for w, dt in widths]
    return pl.pallas_call(
        functools.partial(_premix_kernel, latent),
        grid=(m // tm,),
        in_specs=in_specs, out_specs=out_specs, out_shape=out_shape,
        compiler_params=_cparams(("arbitrary",)),
        name="premix_latent" if latent else "premix_context",
    )(*args)


def _kvprep_kernel(ckv_ref, kr_ref, wukv_ref, kb_ref, vb_ref):
    kvb = _dot(ckv_ref[...].astype(BF16), wukv_ref[...])
    kr_pad = kr_ref[...]
    for i in range(B_HEADS):
        sl = slice(i * LANES, (i + 1) * LANES)
        kb_ref[:, sl] = (kvb[:, sl] + kr_pad).astype(BF16)
    vb_ref[...] = kvb[:, B_HEADS * LANES:].astype(BF16)


def _kvprep(ckv2, kr_pad2, w_ukv):
    m = ckv2.shape[0]
    tm = _row_tile(m)
    return pl.pallas_call(
        _kvprep_kernel,
        grid=(m // tm,),
        in_specs=[pl.BlockSpec((tm, B_KVLORA), lambda i: (i, 0)),
                  pl.BlockSpec((tm, LANES), lambda i: (i, 0)),
                  _resident(w_ukv)],
        out_specs=[pl.BlockSpec((tm, B_HEADS * LANES), lambda i: (i, 0)),
                   pl.BlockSpec((tm, B_HEADS * B_V), lambda i: (i, 0))],
        out_shape=[jax.ShapeDtypeStruct((m, B_HEADS * LANES), BF16),
                   jax.ShapeDtypeStruct((m, B_HEADS * B_V), BF16)],
        compiler_params=_cparams(("arbitrary",)),
        name="kvprep_b",
    )(ckv2, kr_pad2, w_ukv)


def _scores(mp, ch):
    s = _dot_nt(mp[0], ch[0](mp[1]))
    return s if ch[2] is None else s + ch[2](mp[1])


def _lane_fold(x, op):
    parts = [x[:, c:c + LANES] for c in range(0, x.shape[-1], LANES)]
    return functools.reduce(op, parts)


def _attend_tiles(nt, tile_maps, chunks, s_scr, finish, pair_coef=None):
    nc = len(chunks)
    scr, m_scr = s_scr[:nc], s_scr[nc]
    assert len(s_scr) == nc + 1

    def score_chunk(mp, ci, part, store):
        s = _scores(mp, chunks[ci])
        if store:
            scr[ci][...] = s
        fold = _lane_fold(s, jnp.maximum)
        return s, (fold if part is None else jnp.maximum(part, fold))

    part = None
    for ci in range(nc):
        _, part = score_chunk(tile_maps(0)[0], ci, part, True)
    m_scr[...] = part

    def body(i, carry):
        maps = tile_maps(i)
        tail = tile_maps(jnp.minimum(i + 1, nt - 1))[0] if nt > 1 else None
        cur, cur_part = None, m_scr[...]
        outs = []
        for mi, (_, tag, sink) in enumerate(maps):
            m = jnp.max(cur_part, axis=-1, keepdims=True)
            if sink is not None:
                m = jnp.maximum(m, sink)
            nxt_map = maps[mi + 1] if mi + 1 < len(maps) else tail
            nxt, nxt_part, l, acc, probs = [], None, None, None, []
            for ci, ch in enumerate(chunks):
                if nxt_map is not None:
                    s, nxt_part = score_chunk(nxt_map, ci, nxt_part, mi + 1 == len(maps))
                    nxt.append(s)
                z = (scr[ci][...] if cur is None else cur[ci]) - m
                if pair_coef is None:
                    p = jnp.exp2(z.astype(BF16))
                    ls = _lane_fold(p, jnp.add).astype(F32)
                    pv = _dot(p, ch[1](tag))
                    acc = pv if acc is None else acc + pv
                else:
                    p = jnp.exp2(z)
                    ls = _lane_fold(p, jnp.add)
                    probs.append(p.astype(BF16))
                l = ls if l is None else l + ls
            denom = jnp.sum(l, axis=-1, keepdims=True)
            if sink is not None:
                denom = denom + jnp.exp2(sink - m)
            if pair_coef is None:
                outs.append(acc / denom)
            elif mi % 2 == 0:
                first_probs, first_denom = probs, denom
            else:
                wa, wb = (1.0 / first_denom).astype(BF16), (pair_coef / denom).astype(BF16)
                for ci, ch in enumerate(chunks):
                    pv = _dot(first_probs[ci] * wa - probs[ci] * wb, ch[1](tag))
                    acc = pv if acc is None else acc + pv
                outs.append(acc)
            if mi + 1 < len(maps):
                cur, cur_part = nxt, nxt_part
            elif tail is not None:
                m_scr[...] = nxt_part
        finish(i, outs)
        return carry

    lax.fori_loop(0, nt, body, 0)


def _score_buffers(kv, tq):
    sizes = []
    for a in kv[::2]:
        t = a.shape[1]
        assert t % KEY_CHUNK == 0 or t < KEY_CHUNK
        sizes += [min(KEY_CHUNK, t)] * (t // min(KEY_CHUNK, t))
    return [pltpu.VMEM((tq, size), F32) for size in sizes] + [pltpu.VMEM((tq, LANES), F32)]


def _ref_chunks(kv_refs):
    chunks = []
    for i in range(0, len(kv_refs), 2):
        k_ref, v_ref = kv_refs[i], kv_refs[i + 1]
        t = k_ref.shape[1]
        size = min(KEY_CHUNK, t)
        for c in range(0, t, size):
            chunks.append((lambda tag, r=k_ref, c=c, n=size: r[0, c:c + n, tag[0]],
                           lambda tag, r=v_ref, c=c, n=size: r[0, c:c + n, tag[1]], None))
    return chunks


def _attn_a_kernel(lam_init, nkv, tq, q_ref, *refs):
    kv_refs = refs[:2 * nkv]
    lq1_ref, lk1_ref, lq2_ref, lk2_ref, gsub_ref, o_ref = refs[2 * nkv:2 * nkv + 6]
    s_scr = refs[2 * nkv + 6:]
    lam = (jnp.exp(jnp.sum(lq1_ref[...] * lk1_ref[...], axis=-1, keepdims=True))
           - jnp.exp(jnp.sum(lq2_ref[...] * lk2_ref[...], axis=-1, keepdims=True)) + lam_init)
    lane = lax.broadcasted_iota(jnp.int32, (tq, LANES), 1)
    tile_rows = lambda i: pl.ds(pl.multiple_of(i * tq, tq), tq)

    def tile_maps(i):
        maps = []
        for h in range(A_HPS):
            sl = slice(h * LANES, (h + 1) * LANES)
            q = q_ref[0, tile_rows(i), sl]
            zero = jnp.zeros_like(q)
            maps.append((jnp.where(lane < A_DIM, q, zero), (sl, sl), None))
            maps.append((jnp.where(lane >= A_DIM, q, zero), (sl, sl), None))
        return maps

    def finish(i, outs):
        for h, o in enumerate(outs):
            o_ref[0, tile_rows(i), h * LANES:(h + 1) * LANES] = (
                _rms(o, gsub_ref[...]) * (1.0 - lam_init)).astype(BF16)

    _attend_tiles(q_ref.shape[1] // tq, tile_maps, _ref_chunks(kv_refs), s_scr, finish, pair_coef=lam)


def _attn_a(lam_init, q, kv, lq1, lk1, lq2, lk2, g_sub):
    bsz, n, _ = q.shape
    tq = min(Q_TILE, n)
    w = A_HPS * LANES
    small = lambda a: pl.BlockSpec(a.shape, lambda b, h: (0, 0))
    return pl.pallas_call(
        functools.partial(_attn_a_kernel, lam_init, len(kv) // 2, tq),
        grid=(bsz, A_HEADS // A_HPS),
        in_specs=[pl.BlockSpec((1, n, w), lambda b, h: (b, 0, h))]
        + [pl.BlockSpec((1, a.shape[1], w), lambda b, h: (b, 0, h)) for a in kv]
        + [small(lq1), small(lk1), small(lq2), small(lk2), small(g_sub)],
        out_specs=pl.BlockSpec((1, n, w), lambda b, h: (b, 0, h)),
        out_shape=jax.ShapeDtypeStruct((bsz, n, A_HEADS * LANES), BF16),
        scratch_shapes=_score_buffers(kv, tq),
        compiler_params=_cparams(("arbitrary", "arbitrary")),
        name="attn_a",
    )(q, *kv, lq1, lk1, lq2, lk2, g_sub)


def _attn_b_kernel(nkv, tq, q_ref, *refs):
    kv_refs, o_ref, s_scr = refs[:2 * nkv], refs[2 * nkv], refs[2 * nkv + 1:]
    lane = lax.broadcasted_iota(jnp.int32, (tq, LANES), 1)
    tile_rows = lambda i: pl.ds(pl.multiple_of(i * tq, tq), tq)

    def tile_maps(i):
        maps = []
        for h in range(B_HPS):
            sl = slice(h * LANES, (h + 1) * LANES)
            vsl = slice((h // 2) * LANES, (h // 2 + 1) * LANES)
            maps.append((q_ref[0, tile_rows(i), sl], (sl, vsl), None))
        return maps

    def finish(i, outs):
        for pair in range(B_HPS // 2):
            o_ref[0, tile_rows(i), pair * LANES:(pair + 1) * LANES] = (
                jnp.where(lane < B_V, outs[2 * pair], outs[2 * pair + 1]).astype(BF16))

    _attend_tiles(q_ref.shape[1] // tq, tile_maps, _ref_chunks(kv_refs), s_scr, finish)


def _attn_b(q, kv):
    bsz, n, _ = q.shape
    tq = min(Q_TILE, n)
    kv_specs = []
    for i, a in enumerate(kv):
        w = B_HPS * (LANES if i % 2 == 0 else B_V)
        kv_specs.append(pl.BlockSpec((1, a.shape[1], w), lambda b, h: (b, 0, h)))
    return pl.pallas_call(
        functools.partial(_attn_b_kernel, len(kv) // 2, tq),
        grid=(bsz, B_HEADS // B_HPS),
        in_specs=[pl.BlockSpec((1, n, B_HPS * LANES), lambda b, h: (b, 0, h))] + kv_specs,
        out_specs=pl.BlockSpec((1, n, B_HPS * B_V), lambda b, h: (b, 0, h)),
        out_shape=jax.ShapeDtypeStruct((bsz, n, B_HEADS * B_V), BF16),
        scratch_shapes=_score_buffers(kv, tq),
        compiler_params=_cparams(("arbitrary", "arbitrary")),
        name="attn_b",
    )(q, *kv)


def _attn_c_kernel(latent, nblk, *refs):
    if latent:
        q_ref, kx_ref, vx_ref, sink_ref, kw_ref, vw_ref, bias_ref, o_ref = refs[:8]
    else:
        q_ref, kx_ref, vx_ref, sink_ref, o_ref = refs[:5]
    s_scr = refs[8 if latent else 5:]
    lane = lax.broadcasted_iota(jnp.int32, (Q_BLOCK, LANES), 1)
    block_rows = lambda j: pl.ds(pl.multiple_of(j * Q_BLOCK, Q_BLOCK), Q_BLOCK)

    chunks = [(lambda j: kx_ref[0], lambda j: vx_ref[0], None)]
    if latent:
        win_rows = lambda j: pl.ds(pl.multiple_of(j * Q_BLOCK, Q_BLOCK), 3 * Q_BLOCK)

        def bias_of(j):
            first_i = (jnp.asarray(j) == 0).astype(jnp.int32)
            last_i = (jnp.asarray(j) == nblk - 1).astype(jnp.int32)
            return bias_ref[1 - first_i + last_i + 2 * first_i * last_i]

        chunks.append((lambda j: kw_ref[0, win_rows(j), :], lambda j: vw_ref[0, win_rows(j), :], bias_of))

    def tile_maps(j):
        maps = []
        for r in range(C_REP):
            qr = q_ref[0, block_rows(j), r * LANES:(r + 1) * LANES]
            halves = [jnp.where((lane >= g * C_DIM) & (lane < (g + 1) * C_DIM), qr, jnp.zeros_like(qr))
                      for g in range(C_KV_HEADS)]
            rows = slice(C_KV_HEADS * r * Q_BLOCK, C_KV_HEADS * (r + 1) * Q_BLOCK)
            maps.append((jnp.concatenate(halves, axis=0), j, sink_ref[rows, :] * LOG2E))
        return maps

    def finish(j, outs):
        for r, o in enumerate(outs):
            o_ref[0, block_rows(j), r * LANES:(r + 1) * LANES] = (
                jnp.where(lane < C_DIM, o[:Q_BLOCK], o[Q_BLOCK:]).astype(BF16))

    _attend_tiles(nblk, tile_maps, chunks, s_scr, finish)


def _attn_c(q, kx, vx, sink_rows, k_lat=None, v_lat=None):
    bsz, n, _ = q.shape
    tc = kx.shape[1]
    nblk = n // Q_BLOCK
    latent = k_lat is not None
    rows = C_KV_HEADS * Q_BLOCK
    whole = lambda a: pl.BlockSpec((1,) + a.shape[1:], lambda b: (b,) + (0,) * (a.ndim - 1))
    const = lambda a: pl.BlockSpec(a.shape, lambda b: (0,) * a.ndim)
    in_specs = [whole(q), whole(kx), whole(vx), const(sink_rows)]
    args = [q, kx, vx, sink_rows]
    scratch = [pltpu.VMEM((rows, tc), F32)]
    if latent:
        pad = ((0, 0), (Q_BLOCK, Q_BLOCK), (0, 0))
        kw, vw = jnp.pad(k_lat, pad), jnp.pad(v_lat, pad)
        row = (np.arange(rows) % Q_BLOCK)[:, None]
        col = np.arange(3 * Q_BLOCK)[None, :]
        in_prev, in_next = col < Q_BLOCK, col >= 2 * Q_BLOCK
        band = np.where(in_prev, col >= row, np.where(in_next, col - 2 * Q_BLOCK <= row, True))
        variants = [band & ~in_prev, band, band & ~in_next, band & ~in_prev & ~in_next]
        bias = np.stack([np.where(ok, 0.0, NEG_INF).astype(np.float32) for ok in variants], axis=0)
        in_specs += [whole(kw), whole(vw), const(bias)]
        args += [kw, vw, bias]
        scratch.append(pltpu.VMEM((rows, 3 * Q_BLOCK), F32))
    scratch.append(pltpu.VMEM((rows, LANES), F32))
    return pl.pallas_call(
        functools.partial(_attn_c_kernel, latent, nblk),
        grid=(bsz,),
        in_specs=in_specs,
        out_specs=whole(q),
        out_shape=jax.ShapeDtypeStruct((bsz, n, 512), BF16),
        scratch_shapes=scratch,
        compiler_params=_cparams(("arbitrary",)),
        name="attn_c_latent" if latent else "attn_c_context",
    )(*args)


def _s5_prepare_kernel(are_ref, aim_ref, ldt_ref, btre_ref, btim_ref, cre_ref, cim_ref,
                       arec_ref, aimc_ref, ctre_ref, ctim_ref,
                       td_ref, wd_ref, vd_ref, a16re_ref, a16im_ref):
    p, n = D_GROUP, D_STATE
    gl = pl.program_id(0) % S5_SG

    def one_hot(shape, target):
        r = lax.broadcasted_iota(jnp.int32, shape, 0)
        c = lax.broadcasted_iota(jnp.int32, shape, 1)
        return jnp.where(c == target(r), 1.0, 0.0).astype(BF16)

    place_p = one_hot((p, LANES), lambda r: gl * p + r)
    place_n = one_hot((n, S5_SLB * LANES), lambda r: gl * n + r)
    pbits = p.bit_length() - 1
    place_tp = one_hot((S5_CHUNK * p, S5_FLAT), lambda r: (r >> pbits) * LANES + gl * p + (r & (p - 1)))

    refs = (are_ref, aim_ref, ldt_ref, btre_ref, btim_ref, cre_ref, cim_ref, a16re_ref, a16im_ref)
    kt_fwd, w_fwd = _s5_prepare_direction(0, *refs)
    kt_bwd, w_bwd = _s5_prepare_direction(1, *refs)
    pieces = [kt_bwd[(S5_CHUNK - 1 - i) * p:(S5_CHUNK - i) * p] for i in range(S5_CHUNK - 1)]
    pieces.append(kt_fwd[0:p] + kt_bwd[0:p])
    pieces.append(kt_fwd[p:])
    lagk = _dot(jnp.concatenate(pieces, axis=0).astype(BF16), place_p)
    for t_in in range(S5_CHUNK):
        row = [lagk[(t_out - t_in + S5_CHUNK - 1) * p:(t_out - t_in + S5_CHUNK) * p] for t_out in range(S5_CHUNK)]
        td_ref[0, t_in, 0] = jnp.concatenate(row, axis=1).astype(BF16)

    w_sets = [w_fwd[0][::-1], w_fwd[1][::-1], w_bwd[0], w_bwd[1]]
    for x, rows in enumerate(w_sets):
        wexp = _dot(jnp.concatenate(rows, axis=0).astype(BF16), place_n)
        for t in range(S5_CHUNK):
            wd_ref[0, t, 0, :, x * S5_SLB * LANES:(x + 1) * S5_SLB * LANES] = wexp[t * p:(t + 1) * p].astype(BF16)

    tt = lax.broadcasted_iota(jnp.int32, (n, S5_CHUNK * p), 1) >> pbits
    for d in range(N_DIRS):
        power = (tt + 1 if d == 0 else S5_CHUNK - tt).astype(F32)
        dt = jnp.exp(ldt_ref[d, 0])
        mag = jnp.exp(power * (arec_ref[d, 0] * dt))
        ang = power * (aimc_ref[d, 0] * dt)
        pr, pi = mag * jnp.cos(ang), mag * jnp.sin(ang)
        ctre, ctim = ctre_ref[d, 0], ctim_ref[d, 0]
        vd_ref[0, 2 * d, 0] = _dot((ctre * pr - ctim * pi).astype(BF16), place_tp).astype(BF16)
        vd_ref[0, 2 * d + 1, 0] = _dot((-(ctre * pi + ctim * pr)).astype(BF16), place_tp).astype(BF16)


def _s5_prepare_direction(d, are_ref, aim_ref, ldt_ref, btre_ref, btim_ref, cre_ref, cim_ref, a16re_ref, a16im_ref):
    hi = lax.Precision.HIGHEST
    are = are_ref[d, 0]
    aim = aim_ref[d, 0]
    dt = jnp.exp(ldt_ref[d, 0])
    nj = S5_CHUNK + 1
    jj = lax.broadcasted_iota(jnp.int32, (nj, D_STATE), 0).astype(F32)
    mag = jnp.exp(jj * (are * dt))
    ang = jj * (aim * dt)
    pre = mag * jnp.cos(ang)
    pim = mag * jnp.sin(ang)
    xr = pre[1:2] - 1.0
    xi = pim[1:2]
    den = are * are + aim * aim
    fr = (xr * are + xi * aim) / den
    fi = (xi * are - xr * aim) / den
    btre = btre_ref[d, 0]
    btim = btim_ref[d, 0]
    bbre = fr * btre - fi * btim
    bbim = fr * btim + fi * btre
    cre = cre_ref[d, 0]
    cim = cim_ref[d, 0]
    wj_re, wj_im = [], []
    for j in range(S5_CHUNK):
        pr = pre[j:j + 1]
        pi = pim[j:j + 1]
        wj_re.append(pr * bbre - pi * bbim)
        wj_im.append(pr * bbim + pi * bbre)
    dn = (((1,), (1,)), ((), ()))
    a16re_ref[d, 0] = pre[S5_CHUNK:S5_CHUNK + 1]
    a16im_ref[d, 0] = pim[S5_CHUNK:S5_CHUNK + 1]
    kt = (lax.dot_general(jnp.concatenate(wj_re, axis=0), cre, dn, precision=hi, preferred_element_type=F32)
          - lax.dot_general(jnp.concatenate(wj_im, axis=0), cim, dn, precision=hi, preferred_element_type=F32))
    return kt, (wj_re, wj_im)


def _s5_prepare(a_re, a_im, log_dt, b_re, b_im, c_re, c_im):
    g, n, p = D_GROUPS, D_STATE, D_GROUP
    nsg = g // S5_SG
    v4 = lambda a: a.reshape(N_DIRS, g, 1, a.shape[-1])
    col = lambda a: a.reshape(N_DIRS, g, n, 1)
    spec = lambda *s: pl.BlockSpec((N_DIRS, 1) + s, lambda i: (0, i) + (0,) * len(s))
    grp = lambda *s: pl.BlockSpec((1, s[0], 1) + s[1:], lambda i: (i // S5_SG, 0, i % S5_SG, 0, 0))
    bt_re = jnp.swapaxes(b_re, -1, -2)
    bt_im = jnp.swapaxes(b_im, -1, -2)
    ct_re = jnp.tile(jnp.swapaxes(c_re, -1, -2), (1, 1, 1, S5_CHUNK))
    ct_im = jnp.tile(jnp.swapaxes(c_im, -1, -2), (1, 1, 1, S5_CHUNK))
    outs = pl.pallas_call(
        _s5_prepare_kernel,
        grid=(g,),
        in_specs=[spec(1, n), spec(1, n), spec(1, 1), spec(p, n), spec(p, n), spec(p, n), spec(p, n),
                  spec(n, 1), spec(n, 1), spec(n, S5_CHUNK * p), spec(n, S5_CHUNK * p)],
        out_specs=[grp(S5_CHUNK, p, S5_FLAT), grp(S5_CHUNK, p, 4 * S5_SLB * LANES), grp(4, n, S5_FLAT),
                   spec(1, n), spec(1, n)],
        out_shape=[jax.ShapeDtypeStruct((nsg, S5_CHUNK, S5_SG, p, S5_FLAT), BF16),
                   jax.ShapeDtypeStruct((nsg, S5_CHUNK, S5_SG, p, 4 * S5_SLB * LANES), BF16),
                   jax.ShapeDtypeStruct((nsg, 4, S5_SG, n, S5_FLAT), BF16)]
        + [jax.ShapeDtypeStruct((N_DIRS, g, 1, n), F32)] * 2,
        compiler_params=_cparams(("arbitrary",)),
        name="s5_prepare",
    )(v4(a_re), v4(a_im), log_dt.reshape(N_DIRS, g, 1, 1), bt_re, bt_im, c_re, c_im,
      col(a_re), col(a_im), ct_re, ct_im)
    td, wd, vd, a16re, a16im = outs
    t_sg = td.reshape(nsg, S5_FLAT, S5_FLAT)
    w_sg = wd.reshape(nsg, S5_FLAT, 4 * S5_SLB * LANES)
    v_sg = vd.reshape(nsg, 4 * S5_SG * n, S5_FLAT)
    a16 = [a16re[0].reshape(1, g * n), a16im[0].reshape(1, g * n),
           a16re[1].reshape(1, g * n), a16im[1].reshape(1, g * n)]
    return t_sg, w_sg, v_sg, a16


def _s5_gather_chunks(du_ref, u_scr, bsz, ck):
    for b in range(bsz):
        for t in range(S5_CHUNK):
            u_scr[b * ck:(b + 1) * ck, t * LANES:(t + 1) * LANES] = (
                du_ref[b, pl.ds(t, ck, stride=S5_CHUNK), :].astype(BF16))


def _s5_state_in_kernel(bsz, bpad, ck, du_ref, w_ref, o0_ref, o1_ref, o2_ref, o3_ref, u_scr):
    _s5_gather_chunks(du_ref, u_scr, bsz, ck)
    s = _dot(u_scr[...], w_ref[0])
    for x, o_ref in enumerate((o0_ref, o1_ref, o2_ref, o3_ref)):
        for j in range(S5_SLB):
            col = (x * S5_SLB + j) * LANES
            for b in range(bpad):
                if b < bsz:
                    o_ref[j, pl.ds(b, ck, stride=bpad), :] = s[b * ck:(b + 1) * ck, col:col + LANES]
                else:
                    o_ref[j, pl.ds(b, ck, stride=bpad), :] = jnp.zeros((ck, LANES), F32)


def _s5_state_in(du, w_sg, bpad, ck):
    bsz, n, _ = du.shape
    nchunk = n // S5_CHUNK
    nsg = w_sg.shape[0]
    return pl.pallas_call(
        functools.partial(_s5_state_in_kernel, bsz, bpad, ck),
        grid=(nsg, nchunk // ck),
        in_specs=[pl.BlockSpec((bsz, ck * S5_CHUNK, LANES), lambda s, r: (0, r, s)),
                  pl.BlockSpec((1, S5_FLAT, 4 * S5_SLB * LANES), lambda s, r: (s, 0, 0),
                               pipeline_mode=pl.Buffered(1))],
        out_specs=[pl.BlockSpec((S5_SLB, ck * bpad, LANES), lambda s, r: (s, r, 0))] * 4,
        out_shape=[jax.ShapeDtypeStruct((nsg * S5_SLB, nchunk * bpad, LANES), F32)] * 4,
        scratch_shapes=[pltpu.VMEM((bsz * ck, S5_FLAT), BF16)],
        compiler_params=_cparams(("arbitrary", "arbitrary")),
        name="s5_state_in",
    )(du, w_sg)


def _s5_scan_kernel(nchunk, bpad, s0re_ref, s0im_ref, s1re_ref, s1im_ref, a0re_ref, a0im_ref, a1re_ref, a1im_ref,
                    h0re_ref, h0im_ref, h1re_ref, h1im_ref,
                    p0re_ref, p0im_ref, p1re_ref, p1im_ref, f0re_ref, f0im_ref, f1re_ref, f1im_ref):
    a0re, a0im, a1re, a1im = a0re_ref[...], a0im_ref[...], a1re_ref[...], a1im_ref[...]

    def body(i, carry):
        r0, i0, r1, i1 = carry
        k = pl.ds(pl.multiple_of(i * bpad, bpad), bpad)
        kb = pl.ds(pl.multiple_of((nchunk - 1 - i) * bpad, bpad), bpad)
        p0re_ref[:, k, :] = r0
        p0im_ref[:, k, :] = i0
        p1re_ref[:, kb, :] = r1
        p1im_ref[:, kb, :] = i1
        n_r0 = a0re * r0 - a0im * i0 + s0re_ref[:, k, :]
        n_i0 = a0re * i0 + a0im * r0 + s0im_ref[:, k, :]
        n_r1 = a1re * r1 - a1im * i1 + s1re_ref[:, kb, :]
        n_i1 = a1re * i1 + a1im * r1 + s1im_ref[:, kb, :]
        return n_r0, n_i0, n_r1, n_i1

    r0, i0, r1, i1 = lax.fori_loop(0, nchunk, body,
                                   (h0re_ref[...], h0im_ref[...], h1re_ref[...], h1im_ref[...]))
    f0re_ref[...] = r0
    f0im_ref[...] = i0
    f1re_ref[...] = r1
    f1im_ref[...] = i1


def _s5_scan(s_in, a16, h0, nchunk, bpad):
    nlb = D_GROUPS * D_STATE // LANES
    lb = 2
    seq_spec = pl.BlockSpec((lb, nchunk * bpad, LANES), lambda i: (i, 0, 0))
    a_spec = pl.BlockSpec((lb, 1, LANES), lambda i: (i, 0, 0))
    h_spec = pl.BlockSpec((lb, bpad, LANES), lambda i: (i, 0, 0))
    outs = pl.pallas_call(
        functools.partial(_s5_scan_kernel, nchunk, bpad),
        grid=(nlb // lb,),
        in_specs=[seq_spec] * 4 + [a_spec] * 4 + [h_spec] * 4,
        out_specs=[seq_spec] * 4 + [h_spec] * 4,
        out_shape=[jax.ShapeDtypeStruct((nlb, nchunk * bpad, LANES), F32)] * 4
        + [jax.ShapeDtypeStruct((nlb, bpad, LANES), F32)] * 4,
        compiler_params=_cparams(("arbitrary",)),
        name="s5_scan",
    )(*s_in, *a16, *h0)
    return outs[:4], outs[4:]


def _s5_output_kernel(bsz, bpad, ck, du_ref, t_ref, v_ref, p0_ref, p1_ref, p2_ref, p3_ref, y_ref, u_scr, p_scr):
    _s5_gather_chunks(du_ref, u_scr, bsz, ck)
    for x, p_ref in enumerate((p0_ref, p1_ref, p2_ref, p3_ref)):
        for j in range(S5_SLB):
            col = (x * S5_SLB + j) * LANES
            for b in range(bsz):
                p_scr[b * ck:(b + 1) * ck, col:col + LANES] = (
                    p_ref[j, pl.ds(b, ck, stride=bpad), :].astype(BF16))
    y = _dot(u_scr[...], t_ref[0]) + _dot(p_scr[...], v_ref[0])
    for b in range(bsz):
        for t in range(S5_CHUNK):
            y_ref[b, pl.ds(t, ck, stride=S5_CHUNK), :] = y[b * ck:(b + 1) * ck, t * LANES:(t + 1) * LANES]


def _s5_output(du, t_sg, v_sg, p_states, bpad, ck):
    bsz, n, _ = du.shape
    nchunk = n // S5_CHUNK
    nsg = t_sg.shape[0]
    tok = pl.BlockSpec((bsz, ck * S5_CHUNK, LANES), lambda s, r: (0, r, s))
    mat = pl.BlockSpec((1, S5_FLAT, S5_FLAT), lambda s, r: (s, 0, 0), pipeline_mode=pl.Buffered(1))
    return pl.pallas_call(
        functools.partial(_s5_output_kernel, bsz, bpad, ck),
        grid=(nsg, nchunk // ck),
        in_specs=[tok, mat, mat] + [pl.BlockSpec((S5_SLB, ck * bpad, LANES), lambda s, r: (s, r, 0))] * 4,
        out_specs=tok,
        out_shape=jax.ShapeDtypeStruct((bsz, n, MIX_W), F32),
        scratch_shapes=[pltpu.VMEM((bsz * ck, S5_FLAT), BF16), pltpu.VMEM((bsz * ck, S5_FLAT), BF16)],
        compiler_params=_cparams(("arbitrary", "arbitrary")),
        name="s5_output",
    )(du, t_sg, v_sg, *p_states)


def _s5_mixer(du, mats, h0):
    t_sg, w_sg, v_sg, a16 = mats
    bsz, n, _ = du.shape
    nchunk = n // S5_CHUNK
    bpad = -(-bsz // 8) * 8
    ck = min(nchunk, max(S5_ROWS // bsz, 1))
    assert nchunk % ck == 0 and (ck * S5_CHUNK) % 8 == 0
    to_blocks = lambda h: h.reshape(h.shape[0], -1, LANES).transpose(1, 0, 2)
    h0 = [to_blocks(jnp.pad(h, ((0, bpad - bsz), (0, 0)))) for h in h0]
    a16 = [to_blocks(a) for a in a16]
    s_in = _s5_state_in(du, w_sg, bpad, ck)
    p_states, finals = _s5_scan(s_in, a16, h0, nchunk, bpad)
    y = _s5_output(du, t_sg, v_sg, p_states, bpad, ck)
    return y, [f.transpose(1, 0, 2).reshape(bpad, -1)[:bsz] for f in finals]


def _merge_kernel(x_ref, hb_ref, oa_ref, ob_ref, oc_ref, y_ref, u_ref, mod_ref, dskip_ref, wglu_ref,
                  wgate_ref, wbr_ref, wo_ref, gpost_ref, xo_ref):
    yv = y_ref[...] + dskip_ref[...] * u_ref[...]
    yg = jax.nn.gelu(yv)
    od = yg * jax.nn.sigmoid(_dot(yg.astype(BF16), wglu_ref[...]))
    hb = hb_ref[...]
    branches = (oa_ref[...], ob_ref[...], oc_ref[...], od.astype(BF16))
    merged = None
    for k, o in enumerate(branches):
        gate = jax.nn.sigmoid(_dot(hb, wgate_ref[:, k * D_MODEL:(k + 1) * D_MODEL]))
        term = gate * _dot(o, wbr_ref[k])
        merged = term if merged is None else merged + term
    z = _dot(merged.astype(BF16), wo_ref[...])
    gate_m = mod_ref[0, 2:3, :]
    xo_ref[...] = x_ref[...] + gate_m * _rms(z, gpost_ref[...])


def _merge(x2, hb, oa, ob, oc, y, du, mod6, d_skip, w_glu, w_gate, w_branch, w_o, g_post, seq):
    m = x2.shape[0]
    tm = _row_tile(m)
    tiles_per_seq = max(seq // tm, 1)
    assert mod6.shape[0] == 1 or seq % tm == 0
    row = lambda w: pl.BlockSpec((tm, w), lambda i: (i, 0))
    full = _resident
    if mod6.shape[0] == 1:
        mod_spec = pl.BlockSpec((1, 6, D_MODEL), lambda i: (0, 0, 0))
    else:
        mod_spec = pl.BlockSpec((1, 6, D_MODEL), lambda i: (i // tiles_per_seq, 0, 0))
    return pl.pallas_call(
        _merge_kernel,
        grid=(m // tm,),
        in_specs=[row(D_MODEL), row(D_MODEL), row(MIX_W), row(MIX_W), row(MIX_W), row(MIX_W), row(MIX_W),
                  mod_spec, full(d_skip), full(w_glu), full(w_gate), full(w_branch), full(w_o), full(g_post)],
        out_specs=row(D_MODEL),
        out_shape=jax.ShapeDtypeStruct((m, D_MODEL), F32),
        compiler_params=_cparams(("arbitrary",)),
        name="merge",
    )(x2, hb, oa, ob, oc, y, du, mod6, d_skip, w_glu, w_gate, w_branch, w_o, g_post)


def _ffn_kernel(x_ref, mod_ref, gpre_ref, w1_ref, w3_ref, w2_ref, gpost_ref, xo_ref):
    x = x_ref[...]
    shift = mod_ref[0, 3:4, :]
    scale = mod_ref[0, 4:5, :]
    gate = mod_ref[0, 5:6, :]
    h2 = (_rms(x, gpre_ref[...]) * (1.0 + scale) + shift).astype(BF16)
    a = _dot(h2, w1_ref[...])
    b = _dot(h2, w3_ref[...])
    f = _dot((jax.nn.silu(a) * b).astype(BF16), w2_ref[...])
    xo_ref[...] = x + gate * _rms(f, gpost_ref[...])


def _ffn(x2, mod6, g_pre, w1, w3, w2, g_post, seq):
    m = x2.shape[0]
    tm = _row_tile(m)
    tiles_per_seq = max(seq // tm, 1)
    assert mod6.shape[0] == 1 or seq % tm == 0
    row = lambda w: pl.BlockSpec((tm, w), lambda i: (i, 0))
    full = _resident
    if mod6.shape[0] == 1:
        mod_spec = pl.BlockSpec((1, 6, D_MODEL), lambda i: (0, 0, 0))
    else:
        mod_spec = pl.BlockSpec((1, 6, D_MODEL), lambda i: (i // tiles_per_seq, 0, 0))
    return pl.pallas_call(
        _ffn_kernel,
        grid=(m // tm,),
        in_specs=[row(D_MODEL), mod_spec, full(g_pre), full(w1), full(w3), full(w2), full(g_post)],
        out_specs=row(D_MODEL),
        out_shape=jax.ShapeDtypeStruct((m, D_MODEL), F32),
        compiler_params=_cparams(("arbitrary",)),
        name="ffn",
    )(x2, mod6, g_pre, w1, w3, w2, g_post)


def _rope_tables(n, rot_dim, lane_off):
    f32 = np.float32
    rows = n // GRID_W
    pos_row = np.repeat(np.arange(rows, dtype=f32), GRID_W)
    pos_col = np.tile(np.arange(GRID_W, dtype=f32), rows)
    n_freq = rot_dim // 4
    inv_freq = (ROPE_BASE ** (-np.arange(n_freq, dtype=f32) / f32(n_freq))).astype(f32)
    ang = np.concatenate([pos_row[:, None] * inv_freq, pos_col[:, None] * inv_freq], axis=-1)
    cos, sin = np.cos(ang).astype(f32), np.sin(ang).astype(f32)
    zero = np.zeros_like(sin)
    c = np.concatenate([cos, cos], axis=-1)
    sa = np.concatenate([-sin, zero], axis=-1)
    sb = np.concatenate([zero, sin], axis=-1)
    if rot_dim == LANES // 2 and lane_off == 0:
        return tuple(np.tile(t, (1, 2)) for t in (c, sa, sb))
    pad = lambda t, fill: np.concatenate(
        [np.full((n, lane_off), fill, f32), t, np.full((n, LANES - lane_off - rot_dim), fill, f32)], axis=-1)
    return pad(c, 1.0), pad(sa, 0.0), pad(sb, 0.0)


def _layer_weights(l, w_in, w_gate, w_b_uq, w_b_ukv, sink_c, w_glu, w_branch, w_o, w_ff1, w_ff3, w_ff2):
    wi = w_in[l]
    aq, ak, av = wi[:, 0:512], wi[:, 512:1024], wi[:, 1024:1536]
    bcq, bckv, bkr = wi[:, 1536:1792], wi[:, 1792:1920], wi[:, 1920:1952]
    cq, ck, cv, du = wi[:, 1952:2464], wi[:, 2464:2592], wi[:, 2592:2720], wi[:, 2720:3232]
    cq_perm = cq.reshape(D_MODEL, C_KV_HEADS, C_REP, C_DIM).transpose(0, 2, 1, 3).reshape(D_MODEL, 512)
    zeros = lambda w: jnp.zeros((D_MODEL, w), F32)
    bkr_pad = jnp.concatenate([zeros(B_KR_LANE), bkr, zeros(LANES - B_KR_LANE - B_ROPE)], axis=1)
    w_in_r = jnp.concatenate([aq, ak, av, cq_perm, ck, cv, du, bcq, bckv, bkr_pad], axis=1).astype(BF16)

    uq = w_b_uq[l].reshape(B_QLORA, B_HEADS, B_NOPE + B_ROPE)
    uq = jnp.concatenate([uq, jnp.zeros((B_QLORA, B_HEADS, LANES - B_NOPE - B_ROPE), F32)], axis=-1)
    w_uq_r = uq.reshape(B_QLORA, B_HEADS * LANES).astype(BF16)
    ukv = w_b_ukv[l].reshape(B_KVLORA, B_HEADS, B_NOPE + B_V)
    kn = jnp.concatenate([ukv[..., :B_NOPE], jnp.zeros((B_KVLORA, B_HEADS, LANES - B_NOPE), F32)], axis=-1)
    w_ukv_r = jnp.concatenate([kn.reshape(B_KVLORA, B_HEADS * LANES),
                               ukv[..., B_NOPE:].reshape(B_KVLORA, B_HEADS * B_V)], axis=1).astype(BF16)

    sink_perm = sink_c[l].reshape(C_KV_HEADS, C_REP).T.reshape(C_HEADS)
    sink_rows = jnp.repeat(sink_perm, Q_BLOCK)[:, None]

    wbr = w_branch[l]
    wbr_c = wbr[2].reshape(C_KV_HEADS, C_REP, C_DIM, D_MODEL).transpose(1, 0, 2, 3).reshape(MIX_W, D_MODEL)
    w_branch_r = jnp.stack([wbr[0], wbr[1], wbr_c, wbr[3]], axis=0).astype(BF16)
    return dict(w_in=w_in_r, w_uq=w_uq_r, w_ukv=w_ukv_r, sink_rows=sink_rows,
                w_gate=w_gate[l].astype(BF16), w_glu=w_glu[l].astype(BF16), w_branch=w_branch_r,
                w_o=w_o[l].astype(BF16), w_ff1=w_ff1[l].astype(BF16), w_ff3=w_ff3[l].astype(BF16),
                w_ff2=w_ff2[l].astype(BF16))


def _trunk_layer(l, x, mod6, lw, sp, s5_mats, rope_tabs, ctx):
    latent = ctx is not None
    bsz, n, _ = x.shape
    m = bsz * n
    x2 = x.reshape(m, D_MODEL)
    row1 = lambda v: v.reshape(1, -1)

    outs = _premix(latent, x2, mod6, row1(sp['g_pre_mix']), lw['w_in'], row1(sp['g_b_q']), lw['w_uq'],
                   row1(sp['g_b_kv']), lw['w_ukv'], rope_tabs, n)
    hb, qa, ka, va, qb, kb, vb, qc, kc, vc, du = outs[:11]
    r3 = lambda t: t.reshape(bsz, n, t.shape[-1])
    qa, ka, va, qb, kb, vb, qc, kc, vc, du = map(r3, (qa, ka, va, qb, kb, vb, qc, kc, vc, du))

    lam_init = 0.8 - 0.6 * math.exp(-0.3 * l)
    lam_args = (row1(sp['lam_q1']), row1(sp['lam_k1']), row1(sp['lam_q2']), row1(sp['lam_k2']), row1(sp['g_a_sub']))
    if latent:
        past = ctx['a_k'].shape[1]
        ka_ctx = ctx['a_k'].reshape(bsz, past, 512).astype(BF16)
        va_ctx = ctx['a_v'].reshape(bsz, past, 512).astype(BF16)
        kr_pad = jnp.pad(ctx['b_kr'].reshape(bsz * past, B_ROPE),
                         ((0, 0), (B_KR_LANE, LANES - B_KR_LANE - B_ROPE)))
        kb_ctx, vb_ctx = _kvprep(ctx['b_ckv'].reshape(bsz * past, B_KVLORA), kr_pad, lw['w_ukv'])
        o_a = _attn_a(lam_init, qa, [ka, va, ka_ctx, va_ctx], *lam_args)
        o_b = _attn_b(qb, [kb, vb, kb_ctx.reshape(bsz, past, -1), vb_ctx.reshape(bsz, past, -1)])
        o_c = _attn_c(qc, ctx['c_k'].reshape(bsz, past, LANES).astype(BF16),
                      ctx['c_v'].reshape(bsz, past, LANES).astype(BF16), lw['sink_rows'], kc, vc)
        gn = D_GROUPS * D_STATE
        h0 = [ctx['d_re'][:, 0].reshape(bsz, gn), ctx['d_im'][:, 0].reshape(bsz, gn),
              ctx['d_re'][:, 1].reshape(bsz, gn), ctx['d_im'][:, 1].reshape(bsz, gn)]
    else:
        o_a = _attn_a(lam_init, qa, [ka, va], *lam_args)
        o_b = _attn_b(qb, [kb, vb])
        o_c = _attn_c(qc, kc, vc, lw['sink_rows'])
        h0 = [jnp.zeros((bsz, D_GROUPS * D_STATE), F32)] * 4
    y_s5, finals = _s5_mixer(du, s5_mats, h0)

    f2 = lambda t: t.reshape(m, t.shape[-1])
    x2 = _merge(x2, hb, f2(o_a), f2(o_b), f2(o_c), f2(y_s5), f2(du), mod6, row1(sp['ssm_d']), lw['w_glu'],
                lw['w_gate'], lw['w_branch'], lw['w_o'], row1(sp['g_post_mix']), n)
    x2 = _ffn(x2, mod6, row1(sp['g_pre_ffn']), lw['w_ff1'], lw['w_ff3'], lw['w_ff2'], row1(sp['g_post_ffn']), n)
    x = x2.reshape(bsz, n, D_MODEL)
    if latent:
        return x, None
    akf, avf, ckvf, krf, ckf, cvf = outs[11:]
    st = lambda t: t.reshape(bsz, D_GROUPS, D_STATE)
    new_ctx = {'a_k': akf.reshape(bsz, n, A_HEADS, 2 * A_DIM), 'a_v': avf.reshape(bsz, n, A_HEADS, 2 * A_DIM),
               'b_ckv': ckvf.reshape(bsz, n, B_KVLORA), 'b_kr': krf.reshape(bsz, n, B_ROPE),
               'c_k': ckf.reshape(bsz, n, C_KV_HEADS, C_DIM), 'c_v': cvf.reshape(bsz, n, C_KV_HEADS, C_DIM),
               'd_re': jnp.stack([st(finals[0]), st(finals[2])], axis=1),
               'd_im': jnp.stack([st(finals[1]), st(finals[3])], axis=1)}
    return x, new_ctx


def kernel(x_prompt, x_sample, cache_a_k, cache_a_v, cache_b_ckv, cache_b_kr, cache_c_k, cache_c_v, state_d_re, state_d_im, c, c_ctx, w_mod, b_mod, g_pre_mix, g_post_mix, g_pre_ffn, g_post_ffn, w_in, w_gate, lam_q1, lam_k1, lam_q2, lam_k2, g_a_sub, g_b_q, g_b_kv, w_b_uq, w_b_ukv, sink_c, ssm_a_re, ssm_a_im, ssm_log_dt, ssm_b_re, ssm_b_im, ssm_c_re, ssm_c_im, ssm_d, w_glu, w_branch, w_o, w_ff1, w_ff3, w_ff2):
    dec_b, dec_n, _ = x_sample.shape
    assert dec_b + 1 <= 8

    cond = jnp.concatenate([c_ctx[None, :], c, jnp.zeros((8 - 1 - dec_b, D_MODEL), F32)], axis=0)
    mod = _modulation(cond, w_mod, b_mod)

    small = dict(g_pre_mix=g_pre_mix, g_post_mix=g_post_mix, g_pre_ffn=g_pre_ffn, g_post_ffn=g_post_ffn,
                 lam_q1=lam_q1, lam_k1=lam_k1, lam_q2=lam_q2, lam_k2=lam_k2, g_a_sub=g_a_sub, g_b_q=g_b_q,
                 g_b_kv=g_b_kv, ssm_d=ssm_d)
    rope_tabs = _rope_tables(dec_n, A_DIM, 0) + _rope_tables(dec_n, B_ROPE, B_KR_LANE)

    layers = []
    for l in range(DEPTH):
        lw = _layer_weights(l, w_in, w_gate, w_b_uq, w_b_ukv, sink_c, w_glu, w_branch, w_o, w_ff1, w_ff3, w_ff2)
        sp = {k: v[l] for k, v in small.items()}
        s5_mats = _s5_prepare(ssm_a_re[l], ssm_a_im[l], ssm_log_dt[l], ssm_b_re[l], ssm_b_im[l],
                              ssm_c_re[l], ssm_c_im[l])
        mod_l = mod[l].reshape(8, 6, D_MODEL)
        layers.append((lw, sp, s5_mats, mod_l))

    y_prompt = x_prompt
    ctx_out = []
    for l, (lw, sp, s5_mats, mod_l) in enumerate(layers):
        y_prompt, new_ctx = _trunk_layer(l, y_prompt, mod_l[0:1], lw, sp, s5_mats, None, None)
        ctx_out.append(new_ctx)

    y_sample = x_sample
    for l, (lw, sp, s5_mats, mod_l) in enumerate(layers):
        cached = {'a_k': cache_a_k[:, l], 'a_v': cache_a_v[:, l], 'b_ckv': cache_b_ckv[:, l],
                  'b_kr': cache_b_kr[:, l], 'c_k': cache_c_k[:, l], 'c_v': cache_c_v[:, l],
                  'd_re': state_d_re[:, l], 'd_im': state_d_im[:, l]}
        y_sample, _ = _trunk_layer(l, y_sample, mod_l[1:1 + dec_b], lw, sp, s5_mats, rope_tabs, cached)

    stack = lambda name: jnp.stack([cx[name] for cx in ctx_out], axis=1)
    return (y_prompt, y_sample, stack('a_k'), stack('a_v'), stack('b_ckv'), stack('b_kr'),
            stack('c_k'), stack('c_v'), stack('d_re'), stack('d_im'))
```
